```python
import math
import jax
import jax.numpy as jnp
from jax import lax
import numpy as np


D_MODEL = 2048
BATCH = 4
SEQ = 2048
DEPTH = 4
DEC_BATCH = 4
DEC_SEQ = 4096
PAST_LEN = 128

GRID_W = 64
CHUNK = 128
Q_BLOCK = 128
N_GROUPS = 4
D_MIX = D_MODEL
W_GROUP = D_MIX // N_GROUPS
A_HEADS = 4
A_HEAD_W = W_GROUP // A_HEADS
B_W = W_GROUP
HYENA_ORDER = 2
POS_BANDS = 16
POS_EMB = 2 * POS_BANDS + 1
FILT_H = 64
HYENA_TARGET = 1e-2
HYENA_DECAY_SHORT = 0.3
HYENA_DECAY_LONG = 1.5
C_HEADS = 4
C_KV_HEADS = 2
C_HD = W_GROUP // C_HEADS
D_HEADS = 4
Q_LORA = D_MODEL // 4
KV_LORA = D_MODEL // 8
NOPE = 128
ROPE_D = 64
V_HD = W_GROUP // D_HEADS
ROPE_THETA = 10000.0
D_FF = 11 * D_MODEL // 4
EPS = 1e-6
IN_A = 2 * W_GROUP
IN_B = 3 * B_W
IN_C = (C_HEADS + 2 * C_KV_HEADS) * C_HD
IN_D = Q_LORA + KV_LORA + ROPE_D
IN_COLS = IN_A + IN_B + IN_C + IN_D

kernel_name = 'hybrid_bidir_encoder_two_groups'


def rms_norm(x, g):
    xf = x.astype(jnp.float32)
    y = xf * lax.rsqrt(jnp.mean(xf * xf, axis=-1, keepdims=True) + EPS)
    return y.astype(x.dtype) * g


def layer_norm(x, g):
    xf = x.astype(jnp.float32)
    xc = xf - jnp.mean(xf, axis=-1, keepdims=True)
    y = xc * lax.rsqrt(jnp.mean(xc * xc, axis=-1, keepdims=True) + EPS)
    return y.astype(x.dtype) * g


def dwconv3(x, w, b):
    xp = jnp.pad(x, ((0, 0), (1, 1), (0, 0)))
    return xp[:, :-2] * w[0] + xp[:, 1:-1] * w[1] + xp[:, 2:] * w[2] + b


def rope_half_split(x, ang):
    cos = jnp.cos(ang)[:, None, :].astype(x.dtype)
    sin = jnp.sin(ang)[:, None, :].astype(x.dtype)
    x1, x2 = jnp.split(x, 2, axis=-1)
    return jnp.concatenate([x1 * cos - x2 * sin, x2 * cos + x1 * sin], axis=-1)


def axial_rope(x, row, col):
    sec = x.shape[-1] // 2
    inv = ROPE_THETA ** (-jnp.arange(0, sec, 2, dtype=jnp.float32) / sec)
    return jnp.concatenate([rope_half_split(x[..., :sec], row[:, None] * inv),
                            rope_half_split(x[..., sec:], col[:, None] * inv)], axis=-1)


def block_attention(q, k, v):
    b, s, hkv, g, dk = q.shape
    nb = s // Q_BLOCK
    scale = 1.0 / math.sqrt(dk)
    qb = jnp.moveaxis(q.reshape(b, nb, Q_BLOCK, hkv, g, dk), 1, 0)

    def one_block(qi):
        sc = jnp.einsum('bqhgd,bkhd->bhgqk', qi, k, preferred_element_type=jnp.float32) * scale
        p = jax.nn.softmax(sc, axis=-1).astype(v.dtype)
        return jnp.einsum('bhgqk,bkhe->bqhge', p, v)

    o = lax.map(one_block, qb)
    return jnp.moveaxis(o, 0, 1).reshape(b, s, hkv * g * v.shape[-1])


def gmlp_mixer(proj, vnorm_g, ws, bs):
    z = jax.nn.gelu(proj, approximate=False)
    u, v = jnp.split(z, 2, axis=-1)
    v = layer_norm(v, vnorm_g)
    b, s, _ = v.shape
    vc = v.reshape(b, s // CHUNK, CHUNK, A_HEADS, A_HEAD_W)
    mixed = jnp.einsum('hpq,bnqhc->bnphc', ws, vc) + bs.T[None, None, :, :, None]
    return u * mixed.reshape(b, s, W_GROUP)


def hyena_filters(L, w1, b1, f1, w2, b2, f2, w3, decay):
    pos = jnp.arange(L, dtype=jnp.float32)
    t = jnp.linspace(0.0, 1.0, L, dtype=jnp.float32)
    bands = jnp.linspace(1e-4, POS_BANDS - 1, POS_BANDS, dtype=jnp.float32)
    ang = (2.0 * math.pi / L) * pos[:, None] * bands[None, :]
    feats = jnp.concatenate([t[:, None], jnp.cos(ang), -jnp.sin(ang)], axis=-1)
    h = jnp.sin(f1 * (feats @ w1 + b1))
    h = jnp.sin(f2 * (h @ w2 + b2))
    h = h @ w3
    window = jnp.exp(-t[:, None] * jnp.abs(decay).astype(jnp.float32))
    return (h * window).astype(jnp.float32).reshape(L, HYENA_ORDER, 2, B_W)


def bidir_long_conv(u, h_fwd, h_bwd, skip):
    L, ch = h_fwd.shape
    k_circ = jnp.concatenate([h_fwd, jnp.zeros((1, ch), jnp.float32), h_bwd[:0:-1]], axis=0)
    kf = jnp.fft.rfft(k_circ, n=2 * L, axis=0)
    uf32 = u.astype(jnp.float32)
    uf = jnp.fft.rfft(uf32, n=2 * L, axis=1)
    y = jnp.fft.irfft(uf * kf[None], n=2 * L, axis=1)[:, :L]
    return (y + uf32 * skip.astype(jnp.float32)).astype(u.dtype)


def hyena_mixer(proj, conv_w, conv_b, w1, b1, f1, w2, b2, f2, w3, decay, skip):
    L = proj.shape[1]
    z = dwconv3(proj, conv_w, conv_b)
    x1, x2, v = jnp.split(z, 3, axis=-1)
    h = hyena_filters(L, w1, b1, f1, w2, b2, f2, w3, decay)
    y = x1 * bidir_long_conv(v, h[:, 0, 0], h[:, 0, 1], skip[0])
    y = x2 * bidir_long_conv(y, h[:, 1, 0], h[:, 1, 1], skip[1])
    return y


def gqa_mixer(proj, qn_g, kn_g, row, col):
    b, s, _ = proj.shape
    q, k, v = jnp.split(proj, [C_HEADS * C_HD, (C_HEADS + C_KV_HEADS) * C_HD], axis=-1)
    q = q.reshape(b, s, C_HEADS, C_HD)
    k = k.reshape(b, s, C_KV_HEADS, C_HD)
    v = v.reshape(b, s, C_KV_HEADS, C_HD)
    q = axial_rope(rms_norm(q, qn_g), row, col)
    k = axial_rope(rms_norm(k, kn_g), row, col)
    q = q.reshape(b, s, C_KV_HEADS, C_HEADS // C_KV_HEADS, C_HD)
    return block_attention(q, k, v)


def mla_mixer(proj, q_a_g, w_q_b, kv_a_g, w_kv_b, qn_g, kn_g, row, col):
    b, s, _ = proj.shape
    q_a, c_kv, k_rope = jnp.split(proj, [Q_LORA, Q_LORA + KV_LORA], axis=-1)
    q = (rms_norm(q_a, q_a_g) @ w_q_b).reshape(b, s, D_HEADS, NOPE + ROPE_D)
    kv = (rms_norm(c_kv, kv_a_g) @ w_kv_b).reshape(b, s, D_HEADS, NOPE + V_HD)
    k_nope, v = jnp.split(kv, [NOPE], axis=-1)
    k = jnp.concatenate([k_nope, jnp.broadcast_to(k_rope[:, :, None, :], (b, s, D_HEADS, ROPE_D))], axis=-1)
    q = rms_norm(q, qn_g)
    k = rms_norm(k, kn_g)
    q = jnp.concatenate([q[..., :NOPE], axial_rope(q[..., NOPE:], row, col)], axis=-1)
    k = jnp.concatenate([k[..., :NOPE], axial_rope(k[..., NOPE:], row, col)], axis=-1)
    return block_attention(q[:, :, :, None, :], k, v)


def run_trunk(x, c, p):
    b, s, _ = x.shape
    rows = s // GRID_W
    row = jnp.repeat(jnp.arange(rows, dtype=jnp.float32), GRID_W)
    col = jnp.tile(jnp.arange(GRID_W, dtype=jnp.float32), rows)
    for l in range(DEPTH):
        mod = jax.nn.silu(c) @ p['ada_w'][l] + p['ada_b'][l]
        sh1, sc1, g1, sh2, sc2, g2 = [m[:, None, :] for m in jnp.split(mod, 6, axis=-1)]
        h = rms_norm(x, p['norm1_g'][l]) * (1 + sc1) + sh1
        proj = h @ p['w_in'][l]
        pa, pb, pc, pd = jnp.split(proj, [IN_A, IN_A + IN_B, IN_A + IN_B + IN_C], axis=-1)
        ya = gmlp_mixer(pa, p['gm_vnorm_g'][l], p['gm_spatial_w'][l], p['gm_spatial_b'][l])
        yb = hyena_mixer(pb, p['hy_conv_w'][l], p['hy_conv_b'][l], p['hy_w1'][l], p['hy_b1'][l], p['hy_f1'][l],
                         p['hy_w2'][l], p['hy_b2'][l], p['hy_f2'][l], p['hy_w3'][l], p['hy_decay'][l], p['hy_skip'][l])
        yc = gqa_mixer(pc, p['gqa_qn_g'][l], p['gqa_kn_g'][l], row, col)
        yd = mla_mixer(pd, p['mla_q_a_g'][l], p['mla_w_q_b'][l], p['mla_kv_a_g'][l], p['mla_w_kv_b'][l],
                       p['mla_qn_g'][l], p['mla_kn_g'][l], row, col)
        ycat = jnp.stack([ya, yb, yc, yd], axis=2)
        ycat = rms_norm(ycat, p['group_norm_g'][l].reshape(N_GROUPS, W_GROUP)).reshape(b, s, D_MIX)
        x = x + g1 * (ycat @ p['w_out'][l])
        h = rms_norm(x, p['norm2_g'][l]) * (1 + sc2) + sh2
        u = dwconv3(h @ p['ffn_w_up'][l], p['ffn_conv_w'][l], p['ffn_conv_b'][l])
        gate, up = jnp.split(u, 2, axis=-1)
        x = x + g2 * ((jax.nn.silu(gate) * up) @ p['ffn_w_down'][l])
    return x


def setup_inputs(seed: int = 0) -> dict:
    key = jax.random.key(seed)
    ks = iter(jax.random.split(key, 48))

    def nrm(shape, scale):
        return jax.random.normal(next(ks), shape, jnp.float32) * scale

    def gain(shape):
        return 1.0 + nrm(shape, 0.02)

    L = DEPTH
    decay_base = jnp.abs(jnp.linspace(math.log(HYENA_TARGET) / HYENA_DECAY_LONG,
                                      math.log(HYENA_TARGET) / HYENA_DECAY_SHORT, B_W, dtype=jnp.float32))
    decay_base = jnp.tile(decay_base, HYENA_ORDER * 2)
    return {
        'x_prompt': nrm((BATCH, SEQ, D_MODEL), 1.0),
        'x_sample': nrm((DEC_BATCH, DEC_SEQ, D_MODEL), 1.0),
        'c_prompt': nrm((BATCH, D_MODEL), 1.0),
        'c_sample': nrm((DEC_BATCH, D_MODEL), 1.0),
        'ada_w': nrm((L, D_MODEL, 6 * D_MODEL), 0.5 * D_MODEL ** -0.5),
        'ada_b': nrm((L, 6 * D_MODEL), 0.02),
        'norm1_g': gain((L, D_MODEL)),
        'w_in': nrm((L, D_MODEL, IN_COLS), D_MODEL ** -0.5),
        'gm_vnorm_g': gain((L, W_GROUP)),
        'gm_spatial_w': nrm((L, A_HEADS, CHUNK, CHUNK), CHUNK ** -0.5),
        'gm_spatial_b': gain((L, A_HEADS, CHUNK)),
        'hy_conv_w': nrm((L, 3, IN_B), 3 ** -0.5),
        'hy_conv_b': nrm((L, IN_B), 0.02),
        'hy_w1': nrm((L, POS_EMB, FILT_H), POS_EMB ** -0.5),
        'hy_b1': nrm((L, FILT_H), 0.1),
        'hy_f1': gain((L, FILT_H)),
        'hy_w2': nrm((L, FILT_H, FILT_H), FILT_H ** -0.5),
        'hy_b2': nrm((L, FILT_H), 0.1),
        'hy_f2': gain((L, FILT_H)),
        'hy_w3': nrm((L, FILT_H, HYENA_ORDER * 2 * B_W), 0.1 * FILT_H ** -0.5),
        'hy_decay': decay_base * (1.0 + nrm((L, HYENA_ORDER * 2 * B_W), 0.05)),
        'hy_skip': nrm((L, HYENA_ORDER, B_W), 0.5),
        'gqa_qn_g': gain((L, C_HD)),
        'gqa_kn_g': gain((L, C_HD)),
        'mla_q_a_g': gain((L, Q_LORA)),
        'mla_w_q_b': nrm((L, Q_LORA, D_HEADS * (NOPE + ROPE_D)), Q_LORA ** -0.5),
        'mla_kv_a_g': gain((L, KV_LORA)),
        'mla_w_kv_b': nrm((L, KV_LORA, D_HEADS * (NOPE + V_HD)), KV_LORA ** -0.5),
        'mla_qn_g': gain((L, NOPE + ROPE_D)),
        'mla_kn_g': gain((L, NOPE + ROPE_D)),
        'group_norm_g': gain((L, D_MIX)),
        'w_out': nrm((L, D_MIX, D_MODEL), D_MIX ** -0.5),
        'norm2_g': gain((L, D_MODEL)),
        'ffn_w_up': nrm((L, D_MODEL, 2 * D_FF), D_MODEL ** -0.5),
        'ffn_conv_w': nrm((L, 3, 2 * D_FF), 3 ** -0.5),
        'ffn_conv_b': nrm((L, 2 * D_FF), 0.02),
        'ffn_w_down': nrm((L, D_FF, D_MODEL), D_FF ** -0.5),
    }


def reference(x_prompt, x_sample, c_prompt, c_sample, ada_w, ada_b, norm1_g, w_in, gm_vnorm_g, gm_spatial_w,
              gm_spatial_b, hy_conv_w, hy_conv_b, hy_w1, hy_b1, hy_f1, hy_w2, hy_b2, hy_f2, hy_w3, hy_decay, hy_skip,
              gqa_qn_g, gqa_kn_g, mla_q_a_g, mla_w_q_b, mla_kv_a_g, mla_w_kv_b, mla_qn_g, mla_kn_g, group_norm_g,
              w_out, norm2_g, ffn_w_up, ffn_conv_w, ffn_conv_b, ffn_w_down):
    params = {
        'ada_w': ada_w, 'ada_b': ada_b, 'norm1_g': norm1_g, 'w_in': w_in,
        'gm_vnorm_g': gm_vnorm_g, 'gm_spatial_w': gm_spatial_w, 'gm_spatial_b': gm_spatial_b,
        'hy_conv_w': hy_conv_w, 'hy_conv_b': hy_conv_b, 'hy_w1': hy_w1, 'hy_b1': hy_b1, 'hy_f1': hy_f1,
        'hy_w2': hy_w2, 'hy_b2': hy_b2, 'hy_f2': hy_f2, 'hy_w3': hy_w3, 'hy_decay': hy_decay, 'hy_skip': hy_skip,
        'gqa_qn_g': gqa_qn_g, 'gqa_kn_g': gqa_kn_g,
        'mla_q_a_g': mla_q_a_g, 'mla_w_q_b': mla_w_q_b, 'mla_kv_a_g': mla_kv_a_g, 'mla_w_kv_b': mla_w_kv_b,
        'mla_qn_g': mla_qn_g, 'mla_kn_g': mla_kn_g,
        'group_norm_g': group_norm_g, 'w_out': w_out, 'norm2_g': norm2_g,
        'ffn_w_up': ffn_w_up, 'ffn_conv_w': ffn_conv_w, 'ffn_conv_b': ffn_conv_b, 'ffn_w_down': ffn_w_down,
    }
    y_prompt = run_trunk(x_prompt, c_prompt, params)
    y_sample = run_trunk(x_sample, c_sample, params)
    return (y_prompt, y_sample)
```

```python
import functools
import math

import jax
import jax.numpy as jnp
from jax import lax
from jax.experimental import pallas as pl
from jax.experimental.pallas import tpu as pltpu

F32 = jnp.float32
BF16 = jnp.bfloat16

D_MODEL = 2048
DEPTH = 4
GRID_W = 64
CHUNK = 128
W_GROUP = 512
A_HEADS = 4
POS_BANDS = 16
POS_EMB = 2 * POS_BANDS + 1
FILT_H = 64
C_HEADS = 4
C_KV_HEADS = 2
C_HD = 128
D_HEADS = 4
Q_LORA = 512
KV_LORA = 256
NOPE = 128
ROPE_D = 64
V_HD = 128
ROPE_THETA = 10000.0
D_FF = 5632
EPS = 1e-6
IN_A = 1024
IN_B = 1536
IN_C = 1024
IN_D = 832
IN_COLS = IN_A + IN_B + IN_C + IN_D
IN_PAD = 4608

LANES = 128
SUBLANES = 8
BF16_ROWS = 16
VMEM_LIMIT = 56 * 1024 * 1024

HY_P = 512
FFN_HALO = BF16_ROWS
MLA_SLOT = 256


def _params(sem, vmem=VMEM_LIMIT):
    return pltpu.CompilerParams(dimension_semantics=sem, vmem_limit_bytes=vmem)


def _single(block_shape, index_map):
    return pl.BlockSpec(block_shape, index_map, pipeline_mode=pl.Buffered(1))


def _ada_kernel(c_ref, w_ref, b_ref, o_ref):
    s = jax.nn.silu(c_ref[...]).astype(BF16)
    o_ref[0] = jnp.dot(s, w_ref[0].astype(BF16), preferred_element_type=F32) + b_ref[0]


def _ada_mod(c_all, ada_w, ada_b):
    nb = c_all.shape[0]
    tn = 1024
    return pl.pallas_call(
        _ada_kernel,
        grid=(DEPTH, 6 * D_MODEL // tn),
        in_specs=[
            pl.BlockSpec((nb, D_MODEL), lambda l, j: (0, 0)),
            pl.BlockSpec((1, D_MODEL, tn), lambda l, j: (l, 0, j)),
            pl.BlockSpec((1, 1, tn), lambda l, j: (l, 0, j)),
        ],
        out_specs=pl.BlockSpec((1, nb, tn), lambda l, j: (l, 0, j)),
        out_shape=jax.ShapeDtypeStruct((DEPTH, nb, 6 * D_MODEL), F32),
        compiler_params=_params(("parallel", "parallel")),
        name="ada_mod",
    )(c_all, ada_w, ada_b.reshape(DEPTH, 1, 6 * D_MODEL))


def _mod_norm_rows(x, g, shift, scale):
    ms = jnp.mean(x * x, axis=-1, keepdims=True)
    return (x * lax.rsqrt(ms + EPS) * g) * (1.0 + scale) + shift


def _inproj_kernel(x_ref, mod_ref, g_ref, w_ref, o_ref, h_ref, *, tm):
    rc = 128

    @pl.when(pl.program_id(1) == 0)
    def _():
        g = g_ref[...]
        shift = mod_ref[0, 0:1, :]
        scale = mod_ref[0, 1:2, :]

        def body(c, carry):
            rows = pl.ds(pl.multiple_of(c * rc, rc), rc)
            h_ref[rows, :] = _mod_norm_rows(x_ref[rows, :], g, shift, scale).astype(BF16)
            return carry

        lax.fori_loop(0, tm // rc, body, 0)

    o_ref[...] = jnp.dot(h_ref[...], w_ref[...], preferred_element_type=F32)


def _in_proj(x, mod_l, b0, seq, g, w_in_b, layer):
    t = x.shape[0]
    tm, tn = 1024, 768
    return pl.pallas_call(
        functools.partial(_inproj_kernel, tm=tm),
        grid=(t // tm, IN_PAD // tn),
        in_specs=[
            pl.BlockSpec((tm, D_MODEL), lambda i, j: (i, 0)),
            pl.BlockSpec((1, 6, D_MODEL), lambda i, j: (b0 + (i * tm) // seq, 0, 0)),
            pl.BlockSpec((None, 1, D_MODEL), lambda i, j: (layer, 0, 0)),
            pl.BlockSpec((None, D_MODEL, tn), lambda i, j: (layer, 0, j)),
        ],
        out_specs=pl.BlockSpec((tm, tn), lambda i, j: (i, j)),
        out_shape=jax.ShapeDtypeStruct((t, IN_PAD), F32),
        scratch_shapes=[pltpu.VMEM((tm, D_MODEL), BF16)],
        compiler_params=_params(("parallel", "arbitrary")),
        name="in_proj",
    )(x, mod_l, g, w_in_b)


def _gelu(x):
    return 0.5 * x * (1.0 + lax.erf(x * (1.0 / math.sqrt(2.0))))


def _gmlp_kernel(u_ref, v_ref, g_ref, ws_ref, bias_ref, o_ref, *, tm):
    g = g_ref[...]
    for n in range(tm // CHUNK):
        rows = slice(n * CHUNK, (n + 1) * CHUNK)
        u = _gelu(u_ref[rows, :])
        v = _gelu(v_ref[rows, :])
        vc = v - jnp.mean(v, axis=-1, keepdims=True)
        vn = (vc * lax.rsqrt(jnp.mean(vc * vc, axis=-1, keepdims=True) + EPS) * g).astype(BF16)
        for h in range(A_HEADS):
            cols = slice(h * LANES, (h + 1) * LANES)
            mixed = jnp.dot(ws_ref[h], vn[:, cols], preferred_element_type=F32) + bias_ref[:, cols]
            o_ref[rows, cols] = (u[:, cols] * mixed).astype(o_ref.dtype)


def _gmlp(proj, g, ws_b, bias_full, layer):
    t = proj.shape[0]
    tm = 512
    return pl.pallas_call(
        functools.partial(_gmlp_kernel, tm=tm),
        grid=(t // tm,),
        in_specs=[
            pl.BlockSpec((tm, W_GROUP), lambda i: (i, 0)),
            pl.BlockSpec((tm, W_GROUP), lambda i: (i, 1)),
            pl.BlockSpec((None, 1, W_GROUP), lambda i: (layer, 0, 0)),
            pl.BlockSpec((None, A_HEADS, CHUNK, CHUNK), lambda i: (layer, 0, 0, 0)),
            pl.BlockSpec((None, CHUNK, W_GROUP), lambda i: (layer, 0, 0)),
        ],
        out_specs=pl.BlockSpec((tm, W_GROUP), lambda i: (i, 0)),
        out_shape=jax.ShapeDtypeStruct((t, W_GROUP), BF16),
        compiler_params=_params(("parallel",)),
        name="gmlp",
    )(proj, proj, g, ws_b, bias_full)


def _hy_filter_kernel(f_ref, w1_ref, b1_ref, f1_ref, w2_ref, b2_ref, f2_ref, w3_ref, dec_ref, o_ref):
    hp = lax.Precision.HIGHEST
    feats = f_ref[...]
    h = jnp.sin(f1_ref[...] * (jnp.dot(feats, w1_ref[...], precision=hp, preferred_element_type=F32) + b1_ref[...]))
    h = jnp.sin(f2_ref[...] * (jnp.dot(h, w2_ref[...], precision=hp, preferred_element_type=F32) + b2_ref[...]))
    h = jnp.dot(h, w3_ref[0], precision=hp, preferred_element_type=F32)
    tcol = feats[:, 0:1]
    valid = feats[:, LANES - 1:LANES]
    k = h * jnp.exp(-tcol * jnp.abs(dec_ref[0])) * valid
    o_ref[0] = k[:, :W_GROUP]
    o_ref[1] = k[:, W_GROUP:]


def _hy_filters(feats, w1p, b1p, f1p, w2p, b2p, f2p, w3d, decd, layer, seq):
    rt = 512
    half = seq // rt
    c2 = lambda i: (layer, 0, 0)
    return pl.pallas_call(
        _hy_filter_kernel,
        grid=(2 * seq // rt,),
        in_specs=[
            pl.BlockSpec((rt, LANES), lambda i: (i, 0)),
            pl.BlockSpec((None, LANES, LANES), c2),
            pl.BlockSpec((None, 1, LANES), c2),
            pl.BlockSpec((None, 1, LANES), c2),
            pl.BlockSpec((None, LANES, LANES), c2),
            pl.BlockSpec((None, 1, LANES), c2),
            pl.BlockSpec((None, 1, LANES), c2),
            pl.BlockSpec((None, 1, LANES, 2 * W_GROUP), lambda i: (layer, jnp.where(i < half, 1, 0), 0, 0)),
            pl.BlockSpec((None, 1, 1, 2 * W_GROUP), lambda i: (layer, jnp.where(i < half, 1, 0), 0, 0)),
        ],
        out_specs=pl.BlockSpec((2, rt, W_GROUP), lambda i: (0, i, 0)),
        out_shape=jax.ShapeDtypeStruct((2, 2 * seq, W_GROUP), F32),
        compiler_params=_params(("parallel",)),
        name="hy_filter",
    )(feats, w1p, b1p, f1p, w2p, b2p, f2p, w3d, decd)


def _hy_fdft_kernel(k_ref, fwd_ref, o_ref, prev_ref):
    e = pl.program_id(1)
    cur = jnp.dot(fwd_ref[...], k_ref[0].astype(BF16), preferred_element_type=F32)

    @pl.when(e > 0)
    def _():
        prev = prev_ref[...]
        odd = (lax.broadcasted_iota(jnp.int32, prev.shape, 0) & 1) == 1
        o_ref[0, 0] = cur + jnp.where(odd, -prev, prev)

    prev_ref[...] = cur


def _hy_filter_dft(kseq, fwd_b, seq):
    p = HY_P
    nb = seq // p
    return pl.pallas_call(
        _hy_fdft_kernel,
        grid=(2, 2 * nb),
        in_specs=[
            pl.BlockSpec((1, p, W_GROUP), lambda o, e: (o, e, 0)),
            pl.BlockSpec((2 * p, p), lambda o, e: (0, 0)),
        ],
        out_specs=pl.BlockSpec((1, 1, 2 * p, W_GROUP), lambda o, e: (o, jnp.maximum(e - 1, 0), 0, 0)),
        out_shape=jax.ShapeDtypeStruct((2, 2 * nb - 1, 2 * p, W_GROUP), F32),
        scratch_shapes=[pltpu.VMEM((2 * p, W_GROUP), F32)],
        compiler_params=_params(("arbitrary", "arbitrary")),
        name="hy_filter_dft",
    )(kseq, fwd_b)


def _dwconv_rows(src_ref, r0, rc, total, w, b):
    xc = src_ref[pl.ds(r0, rc), :]
    prev = src_ref[pl.ds(jnp.maximum(r0 - SUBLANES, 0), SUBLANES), :]
    nxt = src_ref[pl.ds(jnp.minimum(r0 + rc, total - SUBLANES), SUBLANES), :]
    prev = jnp.where(r0 > 0, prev, 0.0)
    nxt = jnp.where(r0 + rc < total, nxt, 0.0)
    cat = jnp.concatenate([prev, xc, nxt], axis=0)
    n = rc + 2 * SUBLANES
    dn = pltpu.roll(cat, 1, 0)[SUBLANES:SUBLANES + rc]
    up = pltpu.roll(cat, n - 1, 0)[SUBLANES:SUBLANES + rc]
    return dn * w[0:1] + xc * w[1:2] + up * w[2:3] + b


def _hyena_kernel(p1_ref, p2_ref, pv_ref, cw_ref, cb_ref, skip_ref, h_ref, fwd_ref, inv_ref, o_ref,
                  vz, x1z, x2z, vhat, yhat, *, seq, p):
    nb = seq // p
    rc = 256
    rm = 32

    def conv_body(c, carry):
        r0 = pl.multiple_of(c * rc, rc)
        rows = pl.ds(r0, rc)
        x1z[rows, :] = _dwconv_rows(p1_ref, r0, rc, seq, cw_ref[0], cb_ref[0:1, :])
        x2z[rows, :] = _dwconv_rows(p2_ref, r0, rc, seq, cw_ref[1], cb_ref[1:2, :])
        vz[rows, :] = _dwconv_rows(pv_ref, r0, rc, seq, cw_ref[2], cb_ref[2:3, :])
        return carry

    lax.fori_loop(0, seq // rc, conv_body, 0)

    for order in range(2):
        gate = x1z if order == 0 else x2z
        skip = skip_ref[order:order + 1, :]

        def fwd_body(i, carry):
            rows = pl.ds(pl.multiple_of(i * p, p), p)
            vhat[i] = jnp.dot(fwd_ref[...], vz[rows, :].astype(BF16), preferred_element_type=F32)
            return carry

        lax.fori_loop(0, nb, fwd_body, 0)

        def out_body(i, carry):
            def mac(rows_re, rows_im, packed_dc):
                acc_r = None
                acc_i = None
                for ip in range(nb):
                    d = i - ip + (nb - 1)
                    hr = h_ref[order, d, rows_re, :]
                    hi = h_ref[order, d, rows_im, :]
                    ur = vhat[ip, rows_re, :]
                    ui = vhat[ip, rows_im, :]
                    if packed_dc:
                        first = lax.broadcasted_iota(jnp.int32, hr.shape, 0) == 0
                        tr = hr * ur - jnp.where(first, 0.0, hi * ui)
                        ti = jnp.where(first, hi * ui, hr * ui + hi * ur)
                    else:
                        tr = hr * ur - hi * ui
                        ti = hr * ui + hi * ur
                    acc_r = tr if acc_r is None else acc_r + tr
                    acc_i = ti if acc_i is None else acc_i + ti
                return acc_r, acc_i

            def mac_body(r, c2):
                rows_re = pl.ds(pl.multiple_of(r * rm, rm), rm)
                rows_im = pl.ds(pl.multiple_of(p + r * rm, rm), rm)
                acc_r, acc_i = mac(rows_re, rows_im, False)
                yhat[rows_re, :] = acc_r.astype(BF16)
                yhat[rows_im, :] = acc_i.astype(BF16)
                return c2

            lax.fori_loop(1, p // rm, mac_body, 0)
            acc_r, acc_i = mac(pl.ds(0, rm), pl.ds(p, rm), True)
            yhat[pl.ds(0, rm), :] = acc_r.astype(BF16)
            yhat[pl.ds(p, rm), :] = acc_i.astype(BF16)

            y = jnp.dot(inv_ref[...], yhat[...], preferred_element_type=F32)
            rows = pl.ds(pl.multiple_of(i * p, p), p)
            res = gate[rows, :] * (y + skip * vz[rows, :])
            if order == 0:
                vz[rows, :] = res
            else:
                o_ref[rows, :] = res.astype(o_ref.dtype)
            return carry

        lax.fori_loop(0, nb, out_body, 0)


def _hyena(proj, cw, cb, skip, hfilt, fwd_b, inv_b, layer, seq):
    t = proj.shape[0]
    p = HY_P
    nb = seq // p
    nct = W_GROUP // LANES
    base = IN_A // LANES
    return pl.pallas_call(
        functools.partial(_hyena_kernel, seq=seq, p=p),
        grid=(nct, t // seq),
        in_specs=[
            pl.BlockSpec((seq, LANES), lambda c, b: (b, base + c)),
            pl.BlockSpec((seq, LANES), lambda c, b: (b, base + nct + c)),
            pl.BlockSpec((seq, LANES), lambda c, b: (b, base + 2 * nct + c)),
            pl.BlockSpec((None, 3, 3, LANES), lambda c, b: (layer, 0, 0, c)),
            pl.BlockSpec((None, 3, LANES), lambda c, b: (layer, 0, c)),
            pl.BlockSpec((None, 2, LANES), lambda c, b: (layer, 0, c)),
            _single((2, 2 * nb - 1, 2 * p, LANES), lambda c, b: (0, 0, 0, c)),
            _single((2 * p, p), lambda c, b: (0, 0)),
            _single((p, 2 * p), lambda c, b: (0, 0)),
        ],
        out_specs=pl.BlockSpec((seq, LANES), lambda c, b: (b, c)),
        out_shape=jax.ShapeDtypeStruct((t, W_GROUP), BF16),
        scratch_shapes=[
            pltpu.VMEM((seq, LANES), F32),
            pltpu.VMEM((seq, LANES), F32),
            pltpu.VMEM((seq, LANES), F32),
            pltpu.VMEM((nb, 2 * p, LANES), F32),
            pltpu.VMEM((2 * p, LANES), BF16),
        ],
        compiler_params=_params(("parallel", "parallel")),
        name="hyena",
    )(proj, proj, proj, cw, cb, skip, hfilt, fwd_b, inv_b)


def _rope_lanes(x, cos_t, sin_t, half):
    lane = lax.broadcasted_iota(jnp.int32, x.shape, 1)
    lower = (lane % (2 * half)) < half
    partner = jnp.where(lower, pltpu.roll(x, LANES - half, 1), pltpu.roll(x, half, 1))
    return x * cos_t + partner * sin_t


def _rms_rows(x, g, width):
    ss = jnp.sum(x * x, axis=-1, keepdims=True) * (1.0 / width)
    return x * lax.rsqrt(ss + EPS) * g


def _gqa_prep_kernel(q_ref, k_ref, v_ref, cos_ref, sin_ref, qg_ref, kg_ref, qo_ref, ko_ref, vo_ref):
    cos_t = cos_ref[...]
    sin_t = sin_ref[...]
    scale = 1.0 / math.sqrt(C_HD)
    for h in range(C_HEADS):
        cols = slice(h * C_HD, (h + 1) * C_HD)
        q = _rope_lanes(_rms_rows(q_ref[:, cols], qg_ref[...], C_HD), cos_t, sin_t, C_HD // 4)
        qo_ref[:, cols] = (q * scale).astype(BF16)
    for h in range(C_KV_HEADS):
        cols = slice(h * C_HD, (h + 1) * C_HD)
        k = _rope_lanes(_rms_rows(k_ref[:, cols], kg_ref[...], C_HD), cos_t, sin_t, C_HD // 4)
        ko_ref[:, cols] = k.astype(BF16)
    vo_ref[...] = v_ref[...].astype(BF16)


def _gqa_prep(proj, cos_t, sin_t, qg, kg, layer, seq):
    t = proj.shape[0]
    tm = 512
    qw = C_HEADS * C_HD
    kw = C_KV_HEADS * C_HD
    c0 = IN_A + IN_B
    g2 = lambda i: (layer, 0, 0)
    return pl.pallas_call(
        _gqa_prep_kernel,
        grid=(t // tm,),
        in_specs=[
            pl.BlockSpec((tm, qw), lambda i: (i, c0 // qw)),
            pl.BlockSpec((tm, kw), lambda i: (i, (c0 + qw) // kw)),
            pl.BlockSpec((tm, kw), lambda i: (i, (c0 + qw + kw) // kw)),
            pl.BlockSpec((tm, LANES), lambda i: (i % (seq // tm), 0)),
            pl.BlockSpec((tm, LANES), lambda i: (i % (seq // tm), 0)),
            pl.BlockSpec((None, 1, C_HD), g2),
            pl.BlockSpec((None, 1, C_HD), g2),
        ],
        out_specs=[
            pl.BlockSpec((tm, qw), lambda i: (i, 0)),
            pl.BlockSpec((tm, kw), lambda i: (i, 0)),
            pl.BlockSpec((tm, kw), lambda i: (i, 0)),
        ],
        out_shape=[
            jax.ShapeDtypeStruct((t, qw), BF16),
            jax.ShapeDtypeStruct((t, kw), BF16),
            jax.ShapeDtypeStruct((t, kw), BF16),
        ],
        compiler_params=_params(("parallel",)),
        name="gqa_prep",
    )(proj, proj, proj, cos_t, sin_t, qg, kg)


def _mla_prep_kernel(qa_ref, ckv_ref, kr_ref, cos_ref, sin_ref, qag_ref, kvg_ref, wq_ref, wkv_ref,
                     qn_ref, kn_ref, qo_ref, ko_ref, vo_ref):
    cos_t = cos_ref[...]
    sin_t = sin_ref[...]
    width = NOPE + ROPE_D
    scale = 1.0 / math.sqrt(width)
    qa = _rms_rows(qa_ref[...], qag_ref[...], Q_LORA).astype(BF16)
    q = jnp.dot(qa, wq_ref[...], preferred_element_type=F32)
    ckv = _rms_rows(ckv_ref[...], kvg_ref[...], KV_LORA).astype(BF16)
    kv = jnp.dot(ckv, wkv_ref[...], preferred_element_type=F32)
    kr = kr_ref[...]
    kr_ss = jnp.sum(kr * kr, axis=-1, keepdims=True)
    qn = qn_ref[...]
    kn = kn_ref[...]
    for h in range(D_HEADS):
        base = h * MLA_SLOT
        qh = _rms_rows(q[:, base:base + MLA_SLOT], qn, width)
        qo_ref[:, base:base + NOPE] = (qh[:, :NOPE] * scale).astype(BF16)
        qr = _rope_lanes(qh[:, NOPE:], cos_t, sin_t, ROPE_D // 4)
        qo_ref[:, base + NOPE:base + MLA_SLOT] = (qr * scale).astype(BF16)
        k_nope = kv[:, base:base + NOPE]
        ss = (jnp.sum(k_nope * k_nope, axis=-1, keepdims=True) + kr_ss) * (1.0 / width)
        inv = lax.rsqrt(ss + EPS)
        ko_ref[:, base:base + NOPE] = (k_nope * inv * kn[:, :NOPE]).astype(BF16)
        krn = _rope_lanes(kr * inv * kn[:, NOPE:], cos_t, sin_t, ROPE_D // 4)
        ko_ref[:, base + NOPE:base + MLA_SLOT] = krn.astype(BF16)
        vo_ref[:, h * V_HD:(h + 1) * V_HD] = kv[:, base + NOPE:base + NOPE + V_HD].astype(BF16)


def _mla_prep(proj, cos_t, sin_t, qag, kvg, wq_b, wkv_b, qn_p, kn_p, layer, seq):
    t = proj.shape[0]
    tm = 512
    c0 = IN_A + IN_B + IN_C
    g2 = lambda i: (layer, 0, 0)
    hw = D_HEADS * MLA_SLOT
    return pl.pallas_call(
        _mla_prep_kernel,
        grid=(t // tm,),
        in_specs=[
            pl.BlockSpec((tm, Q_LORA), lambda i: (i, c0 // Q_LORA)),
            pl.BlockSpec((tm, KV_LORA), lambda i: (i, (c0 + Q_LORA) // KV_LORA)),
            pl.BlockSpec((tm, LANES), lambda i: (i, (c0 + Q_LORA + KV_LORA) // LANES)),
            pl.BlockSpec((tm, LANES), lambda i: (i % (seq // tm), 0)),
            pl.BlockSpec((tm, LANES), lambda i: (i % (seq // tm), 0)),
            pl.BlockSpec((None, 1, Q_LORA), g2),
            pl.BlockSpec((None, 1, KV_LORA), g2),
            pl.BlockSpec((None, Q_LORA, hw), g2),
            pl.BlockSpec((None, KV_LORA, hw), g2),
            pl.BlockSpec((None, 1, MLA_SLOT), g2),
            pl.BlockSpec((None, 1, MLA_SLOT), g2),
        ],
        out_specs=[
            pl.BlockSpec((tm, hw), lambda i: (i, 0)),
            pl.BlockSpec((tm, hw), lambda i: (i, 0)),
            pl.BlockSpec((tm, D_HEADS * V_HD), lambda i: (i, 0)),
        ],
        out_shape=[
            jax.ShapeDtypeStruct((t, hw), BF16),
            jax.ShapeDtypeStruct((t, hw), BF16),
            jax.ShapeDtypeStruct((t, D_HEADS * V_HD), BF16),
        ],
        compiler_params=_params(("parallel",)),
        name="mla_prep",
    )(proj, proj, proj, cos_t, sin_t, qag, kvg, wq_b, wkv_b, qn_p, kn_p)


def _attn_kernel(q_ref, k_ref, v_ref, o_ref, *, groups, dk, dv):
    k = k_ref[...]
    v = v_ref[...]
    for g in range(groups):
        q = q_ref[:, g * dk:(g + 1) * dk]
        s = lax.dot_general(q, k, (((1,), (1,)), ((), ())), preferred_element_type=F32)
        m = jnp.max(s, axis=-1, keepdims=True)
        pexp = jnp.exp(s - m)
        denom = jnp.sum(pexp, axis=-1, keepdims=True)
        o = jnp.dot(pexp.astype(BF16), v, preferred_element_type=F32)
        o_ref[:, g * dv:(g + 1) * dv] = (o / denom).astype(o_ref.dtype)


def _attention(q, k, v, seq, kv_heads, groups, dk, dv, name):
    t = q.shape[0]
    tq = 256
    nq = seq // tq
    return pl.pallas_call(
        functools.partial(_attn_kernel, groups=groups, dk=dk, dv=dv),
        grid=(t // seq, kv_heads, nq),
        in_specs=[
            pl.BlockSpec((tq, groups * dk), lambda b, h, i: (b * nq + i, h)),
            pl.BlockSpec((seq, dk), lambda b, h, i: (b, h)),
            pl.BlockSpec((seq, dv), lambda b, h, i: (b, h)),
        ],
        out_specs=pl.BlockSpec((tq, groups * dv), lambda b, h, i: (b * nq + i, h)),
        out_shape=jax.ShapeDtypeStruct((t, kv_heads * groups * dv), BF16),
        compiler_params=_params(("parallel", "parallel", "arbitrary")),
        name=name,
    )(q, k, v)


def _outproj_kernel(ya_ref, yb_ref, yc_ref, yd_ref, x_ref, mod_ref, gg_ref, w_ref, o_ref):
    acc = None
    for gi, y_ref in enumerate((ya_ref, yb_ref, yc_ref, yd_ref)):
        rows = slice(gi * W_GROUP, (gi + 1) * W_GROUP)
        yn = _rms_rows(y_ref[...].astype(F32), gg_ref[:, rows], W_GROUP).astype(BF16)
        part = jnp.dot(yn, w_ref[rows, :], preferred_element_type=F32)
        acc = part if acc is None else acc + part
    o_ref[...] = x_ref[...] + mod_ref[0, 2:3, :] * acc


def _out_proj(ya, yb, yc, yd, x, mod_l, b0, seq, gg, w_out_b, layer):
    t = x.shape[0]
    tm = 512
    yspec = pl.BlockSpec((tm, W_GROUP), lambda i: (i, 0))
    return pl.pallas_call(
        _outproj_kernel,
        grid=(t // tm,),
        in_specs=[
            yspec, yspec, yspec, yspec,
            pl.BlockSpec((tm, D_MODEL), lambda i: (i, 0)),
            pl.BlockSpec((1, 6, D_MODEL), lambda i: (b0 + (i * tm) // seq, 0, 0)),
            pl.BlockSpec((None, 1, D_MODEL), lambda i: (layer, 0, 0)),
            _single((None, D_MODEL, D_MODEL), lambda i: (layer, 0, 0)),
        ],
        out_specs=pl.BlockSpec((tm, D_MODEL), lambda i: (i, 0)),
        out_shape=jax.ShapeDtypeStruct((t, D_MODEL), F32),
        compiler_params=_params(("parallel",)),
        name="out_proj",
    )(ya, yb, yc, yd, x, mod_l, gg, w_out_b)


def _ffn_kernel(x_ref, xp_ref, xn_ref, mod_ref, g_ref, wg_ref, wu_ref, cwg_ref, cwu_ref, cbg_ref, cbu_ref,
                wd_ref, o_ref, h_ref, *, tm, seq):
    i = pl.program_id(0)
    j = pl.program_id(1)
    nj = pl.num_programs(1)
    rc = 128
    halo = FFN_HALO

    @pl.when(j == 0)
    def _():
        g = g_ref[...]
        shift = mod_ref[0, 3:4, :]
        scale = mod_ref[0, 4:5, :]
        has_prev = ((i * tm) % seq) != 0
        has_next = (((i + 1) * tm) % seq) != 0
        hp = _mod_norm_rows(xp_ref[...], g, shift, scale)
        hn = _mod_norm_rows(xn_ref[...], g, shift, scale)
        h_ref[pl.ds(0, halo), :] = jnp.where(has_prev, hp, 0.0).astype(BF16)
        h_ref[pl.ds(halo + tm, halo), :] = jnp.where(has_next, hn, 0.0).astype(BF16)

        def body(c, carry):
            r0 = pl.multiple_of(c * rc, rc)
            h_ref[pl.ds(halo + r0, rc), :] = _mod_norm_rows(x_ref[pl.ds(r0, rc), :], g, shift, scale).astype(BF16)
            return carry

        lax.fori_loop(0, tm // rc, body, 0)

    hfull = h_ref[...]
    n = tm + 2 * halo

    def conv(u, cw_ref, cb_ref):
        dn = pltpu.roll(u, 1, 0)[halo:halo + tm]
        up = pltpu.roll(u, n - 1, 0)[halo:halo + tm]
        return dn * cw_ref[0:1, :] + u[halo:halo + tm] * cw_ref[1:2, :] + up * cw_ref[2:3, :] + cb_ref[...]

    gate = conv(jnp.dot(hfull, wg_ref[...], preferred_element_type=F32), cwg_ref, cbg_ref)
    upv = conv(jnp.dot(hfull, wu_ref[...], preferred_element_type=F32), cwu_ref, cbu_ref)
    act = (jax.nn.silu(gate) * upv).astype(BF16)
    part = jnp.dot(act, wd_ref[...], preferred_element_type=F32)

    @pl.when(j == 0)
    def _():
        o_ref[...] = part

    @pl.when(j > 0)
    def _():
        o_ref[...] += part

    @pl.when(j == nj - 1)
    def _():
        o_ref[...] = x_ref[...] + mod_ref[0, 5:6, :] * o_ref[...]


def _ffn(x, mod_l, b0, seq, g, w_up_b, cw, cb, w_down_b, layer):
    t = x.shape[0]
    tm, tf = 512, 512
    nf = D_FF // tf
    hb = tm // FFN_HALO
    last = t // FFN_HALO - 1
    return pl.pallas_call(
        functools.partial(_ffn_kernel, tm=tm, seq=seq),
        grid=(t // tm, nf),
        in_specs=[
            pl.BlockSpec((tm, D_MODEL), lambda i, j: (i, 0)),
            pl.BlockSpec((FFN_HALO, D_MODEL), lambda i, j: (jnp.maximum(i * hb - 1, 0), 0)),
            pl.BlockSpec((FFN_HALO, D_MODEL), lambda i, j: (jnp.minimum((i + 1) * hb, last), 0)),
            pl.BlockSpec((1, 6, D_MODEL), lambda i, j: (b0 + (i * tm) // seq, 0, 0)),
            pl.BlockSpec((None, 1, D_MODEL), lambda i, j: (layer, 0, 0)),
            pl.BlockSpec((None, D_MODEL, tf), lambda i, j: (layer, 0, j)),
            pl.BlockSpec((None, D_MODEL, tf), lambda i, j: (layer, 0, nf + j)),
            pl.BlockSpec((None, 3, tf), lambda i, j: (layer, 0, j)),
            pl.BlockSpec((None, 3, tf), lambda i, j: (layer, 0, nf + j)),
            pl.BlockSpec((None, 1, tf), lambda i, j: (layer, 0, j)),
            pl.BlockSpec((None, 1, tf), lambda i, j: (layer, 0, nf + j)),
            pl.BlockSpec((None, tf, D_MODEL), lambda i, j: (layer, j, 0)),
        ],
        out_specs=pl.BlockSpec((tm, D_MODEL), lambda i, j: (i, 0)),
        out_shape=jax.ShapeDtypeStruct((t, D_MODEL), F32),
        scratch_shapes=[pltpu.VMEM((tm + 2 * FFN_HALO, D_MODEL), BF16)],
        compiler_params=_params(("parallel", "arbitrary")),
        name="ffn",
    )(x, x, x, mod_l, g, w_up_b, w_up_b, cw, cw, cb, cb, w_down_b)


def _axial_tables(seq, sec, lanes_used):
    pos = jnp.arange(seq, dtype=jnp.int32)
    row = (pos // GRID_W).astype(F32)
    col = (pos % GRID_W).astype(F32)
    inv = ROPE_THETA ** (-jnp.arange(0, sec, 2, dtype=F32) / sec)
    half = sec // 2
    lane = jnp.arange(LANES)
    in_use = lane < lanes_used
    which = (lane // sec) % 2
    freq = inv[lane % half]
    ang = jnp.where(which[None, :] == 0, row[:, None], col[:, None]) * freq[None, :]
    cos_t = jnp.where(in_use[None, :], jnp.cos(ang), 1.0)
    sign = jnp.where((lane % sec) < half, -1.0, 1.0)
    sin_t = jnp.where(in_use[None, :], jnp.sin(ang) * sign[None, :], 0.0)
    return cos_t.astype(F32), sin_t.astype(F32)


def _hyena_feats(seq):
    n = jnp.arange(seq, dtype=jnp.int32)
    posi = jnp.concatenate([seq - n, n])
    valid = (posi < seq).astype(F32)
    posi = jnp.minimum(posi, seq - 1)
    tlin = jnp.linspace(0.0, 1.0, seq, dtype=F32)[posi]
    bands = jnp.linspace(1e-4, POS_BANDS - 1, POS_BANDS, dtype=F32)
    ang = (2.0 * math.pi / seq) * posi.astype(F32)[:, None] * bands[None, :]
    feats = jnp.concatenate([tlin[:, None], jnp.cos(ang), -jnp.sin(ang)], axis=-1)
    pad = jnp.zeros((2 * seq, LANES - POS_EMB - 1), F32)
    return jnp.concatenate([feats, pad, valid[:, None]], axis=-1)


def _dft_mats(p):
    r = jnp.arange(p, dtype=jnp.int32)
    q = (r[:, None] * r[None, :]) % (2 * p)
    ang = q.astype(F32) * (math.pi / p)
    cosm = jnp.cos(ang)
    sinm = jnp.sin(ang)
    alt = jnp.where(r % 2 == 0, 1.0, -1.0).astype(F32)
    im_rows = jnp.where((r == 0)[:, None], alt[None, :], -sinm)
    fwd = jnp.concatenate([cosm, im_rows], axis=0)
    wre = jnp.where((r == 0)[None, :], 0.5, 1.0) / p
    inv_re = cosm * wre
    inv_im = jnp.where((r == 0)[None, :], alt[:, None] * (0.5 / p), -sinm / p)
    inv = jnp.concatenate([inv_re, inv_im], axis=1)
    return fwd.astype(BF16), inv.astype(BF16)


def kernel(x_prompt, x_sample, c_prompt, c_sample, ada_w, ada_b, norm1_g, w_in, gm_vnorm_g, gm_spatial_w, gm_spatial_b, hy_conv_w, hy_conv_b, hy_w1, hy_b1, hy_f1, hy_w2, hy_b2, hy_f2, hy_w3, hy_decay, hy_skip, gqa_qn_g, gqa_kn_g, mla_q_a_g, mla_w_q_b, mla_kv_a_g, mla_w_kv_b, mla_qn_g, mla_kn_g, group_norm_g, w_out, norm2_g, ffn_w_up, ffn_conv_w, ffn_conv_b, ffn_w_down):
    nl = DEPTH
    w_in_b = jnp.pad(w_in, ((0, 0), (0, 0), (0, IN_PAD - IN_COLS))).astype(BF16)
    w_out_b = w_out.astype(BF16)
    w_up_b = ffn_w_up.astype(BF16)
    w_down_b = ffn_w_down.astype(BF16)
    ws_b = gm_spatial_w.astype(BF16)
    gm_bias = jnp.broadcast_to(jnp.swapaxes(gm_spatial_b, 1, 2)[:, :, :, None],
                               (nl, CHUNK, A_HEADS, LANES)).reshape(nl, CHUNK, W_GROUP)
    row3 = lambda a: a.reshape(nl, 1, a.shape[-1])
    norm1 = row3(norm1_g)
    norm2 = row3(norm2_g)
    gm_g = row3(gm_vnorm_g)
    gg = row3(group_norm_g)
    hy_cw = hy_conv_w.reshape(nl, 3, 3, W_GROUP).transpose(0, 2, 1, 3)
    hy_cb = hy_conv_b.reshape(nl, 3, W_GROUP)
    padl = lambda a, rows, cols: jnp.pad(a, ((0, 0), (0, rows - a.shape[1]), (0, cols - a.shape[2])))
    hy_w1p = padl(hy_w1, LANES, LANES)
    hy_w2p = padl(hy_w2, LANES, LANES)
    hy_b1p = padl(row3(hy_b1), 1, LANES)
    hy_f1p = padl(row3(hy_f1), 1, LANES)
    hy_b2p = padl(row3(hy_b2), 1, LANES)
    hy_f2p = padl(row3(hy_f2), 1, LANES)
    hy_w3d = jnp.pad(hy_w3.reshape(nl, FILT_H, 2, 2, W_GROUP).transpose(0, 3, 1, 2, 4)
                     .reshape(nl, 2, FILT_H, 2 * W_GROUP), ((0, 0), (0, 0), (0, LANES - FILT_H), (0, 0)))
    hy_decd = hy_decay.reshape(nl, 2, 2, W_GROUP).transpose(0, 2, 1, 3).reshape(nl, 2, 1, 2 * W_GROUP)
    gqa_qg = row3(gqa_qn_g)
    gqa_kg = row3(gqa_kn_g)
    mla_qag = row3(mla_q_a_g)
    mla_kvg = row3(mla_kv_a_g)
    slot_pad = MLA_SLOT - NOPE - ROPE_D
    wq_b = jnp.pad(mla_w_q_b.reshape(nl, Q_LORA, D_HEADS, NOPE + ROPE_D),
                   ((0, 0), (0, 0), (0, 0), (0, slot_pad))).reshape(nl, Q_LORA, D_HEADS * MLA_SLOT).astype(BF16)
    wkv_b = mla_w_kv_b.astype(BF16)
    mla_qn = jnp.pad(row3(mla_qn_g), ((0, 0), (0, 0), (0, slot_pad)))
    mla_kn = jnp.pad(row3(mla_kn_g), ((0, 0), (0, 0), (0, slot_pad)))
    ffn_cb = row3(ffn_conv_b)

    nbp = x_prompt.shape[0]
    c_all = jnp.concatenate([c_prompt, c_sample], axis=0)
    mod = _ada_mod(c_all, ada_w, ada_b).reshape(nl, c_all.shape[0], 6, D_MODEL)

    fwd_b, inv_b = _dft_mats(HY_P)

    def trunk(x3, b0):
        bsz, seq, _ = x3.shape
        x = x3.reshape(bsz * seq, D_MODEL)
        gcos, gsin = _axial_tables(seq, C_HD // 2, C_HD)
        mcos, msin = _axial_tables(seq, ROPE_D // 2, ROPE_D)
        feats = _hyena_feats(seq)
        for l in range(nl):
            mod_l = mod[l]
            proj = _in_proj(x, mod_l, b0, seq, norm1, w_in_b, l)
            ya = _gmlp(proj, gm_g, ws_b, gm_bias, l)
            kseq = _hy_filters(feats, hy_w1p, hy_b1p, hy_f1p, hy_w2p, hy_b2p, hy_f2p, hy_w3d, hy_decd, l, seq)
            hfilt = _hy_filter_dft(kseq, fwd_b, seq)
            yb = _hyena(proj, hy_cw, hy_cb, hy_skip, hfilt, fwd_b, inv_b, l, seq)
            gq, gk, gv = _gqa_prep(proj, gcos, gsin, gqa_qg, gqa_kg, l, seq)
            yc = _attention(gq, gk, gv, seq, C_KV_HEADS, C_HEADS // C_KV_HEADS, C_HD, C_HD, "gqa_attn")
            mq, mk, mv = _mla_prep(proj, mcos, msin, mla_qag, mla_kvg, wq_b, wkv_b, mla_qn, mla_kn, l, seq)
            yd = _attention(mq, mk, mv, seq, D_HEADS, 1, MLA_SLOT, V_HD, "mla_attn")
            x = _out_proj(ya, yb, yc, yd, x, mod_l, b0, seq, gg, w_out_b, l)
            x = _ffn(x, mod_l, b0, seq, norm2, w_up_b, ffn_conv_w, ffn_cb, w_down_b, l)
        return x.reshape(bsz, seq, D_MODEL)

    return trunk(x_prompt, 0), trunk(x_sample, nbp)
```

```python
import functools
import math

import jax
import jax.numpy as jnp
from jax import lax
from jax.experimental import pallas as pl
from jax.experimental.pallas import tpu as pltpu

F32 = jnp.float32
BF16 = jnp.bfloat16

D_MODEL = 2048
DEPTH = 4
GRID_W = 64
CHUNK = 128
W_GROUP = 512
A_HEADS = 4
POS_BANDS = 16
POS_EMB = 2 * POS_BANDS + 1
FILT_H = 64
C_HEADS = 4
C_KV_HEADS = 2
C_HD = 128
D_HEADS = 4
Q_LORA = 512
KV_LORA = 256
NOPE = 128
ROPE_D = 64
V_HD = 128
ROPE_THETA = 10000.0
D_FF = 5632
EPS = 1e-6
LOG2E = 1.4426950408889634
IN_A = 1024
IN_B = 1536
IN_C = 1024
IN_D = 832
IN_COLS = IN_A + IN_B + IN_C + IN_D
GQA_V_COL = IN_A + IN_B + (C_HEADS + C_KV_HEADS) * C_HD
IN_PAD = 4608

LANES = 128
SUBLANES = 8
BF16_ROWS = 16
VMEM_LIMIT = 56 * 1024 * 1024

ATTN_SCORE_ELEMS = 256 * 4096
HY_P = 512
FFN_HALO = BF16_ROWS
MLA_SLOT = 256


def _params(sem, vmem=VMEM_LIMIT):
    return pltpu.CompilerParams(dimension_semantics=sem, vmem_limit_bytes=vmem)


def _single(block_shape, index_map):
    return pl.BlockSpec(block_shape, index_map, pipeline_mode=pl.Buffered(1))


def _ada_kernel(c_ref, w_ref, b_ref, o_ref):
    s = jax.nn.silu(c_ref[...]).astype(BF16)
    o_ref[0] = jnp.dot(s, w_ref[0].astype(BF16), preferred_element_type=F32) + b_ref[0]


def _ada_mod(c_all, ada_w, ada_b):
    nb = c_all.shape[0]
    tn = 1024
    return pl.pallas_call(
        _ada_kernel,
        grid=(DEPTH, 6 * D_MODEL // tn),
        in_specs=[
            pl.BlockSpec((nb, D_MODEL), lambda l, j: (0, 0)),
            pl.BlockSpec((1, D_MODEL, tn), lambda l, j: (l, 0, j)),
            pl.BlockSpec((1, 1, tn), lambda l, j: (l, 0, j)),
        ],
        out_specs=pl.BlockSpec((1, nb, tn), lambda l, j: (l, 0, j)),
        out_shape=jax.ShapeDtypeStruct((DEPTH, nb, 6 * D_MODEL), F32),
        compiler_params=_params(("parallel", "parallel")),
        name="ada_mod",
    )(c_all, ada_w, ada_b.reshape(DEPTH, 1, 6 * D_MODEL))


def _mod_norm_rows(x, g, shift, scale):
    ms = jnp.mean(x * x, axis=-1, keepdims=True)
    return (x * lax.rsqrt(ms + EPS) * g) * (1.0 + scale) + shift


def _inproj_kernel(xn_ref, x0_ref, modn_ref, mod0_ref, g_ref, w_ref, o_ref, ha_ref, hb_ref, *, tm):
    i = pl.program_id(0)
    rc = 128
    g = g_ref[...]

    def fill(x_ref, mod_ref, dst_ref):
        shift = mod_ref[0, 0:1, :]
        scale = mod_ref[0, 1:2, :]
        for c in range(tm // rc):
            rows = slice(c * rc, (c + 1) * rc)
            dst_ref[rows, :] = _mod_norm_rows(x_ref[rows, :], g, shift, scale).astype(BF16)

    @pl.when(i == 0)
    def _():
        fill(x0_ref, mod0_ref, ha_ref)

    def step(cur_ref, nxt_ref):
        fill(xn_ref, modn_ref, nxt_ref)
        o_ref[...] = jnp.dot(cur_ref[...], w_ref[...], preferred_element_type=F32).astype(o_ref.dtype)

    @pl.when(i % 2 == 0)
    def _():
        step(ha_ref, hb_ref)

    @pl.when(i % 2 == 1)
    def _():
        step(hb_ref, ha_ref)


def _in_proj(x, mod_l, b0, seq, g, w_in_b, layer):
    t = x.shape[0]
    tm = 512
    n = t // tm
    nxt = lambda i: jnp.minimum(i + 1, n - 1)
    return pl.pallas_call(
        functools.partial(_inproj_kernel, tm=tm),
        grid=(n,),
        in_specs=[
            pl.BlockSpec((tm, D_MODEL), lambda i: (nxt(i), 0)),
            _single((tm, D_MODEL), lambda i: (0, 0)),
            pl.BlockSpec((1, 6, D_MODEL), lambda i: (b0 + (nxt(i) * tm) // seq, 0, 0)),
            pl.BlockSpec((1, 6, D_MODEL), lambda i: (b0, 0, 0)),
            pl.BlockSpec((None, 1, D_MODEL), lambda i: (layer, 0, 0)),
            _single((None, D_MODEL, IN_PAD), lambda i: (layer, 0, 0)),
        ],
        out_specs=pl.BlockSpec((tm, IN_PAD), lambda i: (i, 0)),
        out_shape=jax.ShapeDtypeStruct((t, IN_PAD), BF16),
        scratch_shapes=[pltpu.VMEM((tm, D_MODEL), BF16), pltpu.VMEM((tm, D_MODEL), BF16)],
        compiler_params=_params(("arbitrary",)),
        name="in_proj",
    )(x, x, mod_l, mod_l, g, w_in_b)


def _gelu(x):
    return 0.5 * x * (1.0 + lax.erf(x * (1.0 / math.sqrt(2.0))))


def _gmlp_kernel(u_ref, v_ref, g_ref, ws_ref, bias_ref, o_ref, *, tm):
    g = g_ref[...]
    for n in range(tm // CHUNK):
        rows = slice(n * CHUNK, (n + 1) * CHUNK)
        u = _gelu(u_ref[rows, :].astype(F32))
        v = _gelu(v_ref[rows, :].astype(F32))
        vc = v - jnp.mean(v, axis=-1, keepdims=True)
        vn = (vc * lax.rsqrt(jnp.mean(vc * vc, axis=-1, keepdims=True) + EPS) * g).astype(BF16)
        for h in range(A_HEADS):
            cols = slice(h * LANES, (h + 1) * LANES)
            mixed = jnp.dot(ws_ref[h], vn[:, cols], preferred_element_type=F32) + bias_ref[:, cols]
            o_ref[rows, cols] = (u[:, cols] * mixed).astype(o_ref.dtype)


def _gmlp(proj, g, ws_b, bias_full, layer):
    t = proj.shape[0]
    tm = 512
    return pl.pallas_call(
        functools.partial(_gmlp_kernel, tm=tm),
        grid=(t // tm,),
        in_specs=[
            pl.BlockSpec((tm, W_GROUP), lambda i: (i, 0)),
            pl.BlockSpec((tm, W_GROUP), lambda i: (i, 1)),
            pl.BlockSpec((None, 1, W_GROUP), lambda i: (layer, 0, 0)),
            pl.BlockSpec((None, A_HEADS, CHUNK, CHUNK), lambda i: (layer, 0, 0, 0)),
            pl.BlockSpec((None, CHUNK, W_GROUP), lambda i: (layer, 0, 0)),
        ],
        out_specs=pl.BlockSpec((tm, W_GROUP), lambda i: (i, 0)),
        out_shape=jax.ShapeDtypeStruct((t, W_GROUP), BF16),
        compiler_params=_params(("parallel",)),
        name="gmlp",
    )(proj, proj, g, ws_b, bias_full)


def _hy_filter_kernel(f_ref, w1_ref, b1_ref, f1_ref, w2_ref, b2_ref, f2_ref, w3_ref, dec_ref, o_ref):
    hp = lax.Precision.HIGHEST
    feats = f_ref[...]
    h = jnp.sin(f1_ref[...] * (jnp.dot(feats, w1_ref[...], precision=hp, preferred_element_type=F32) + b1_ref[...]))
    h = jnp.sin(f2_ref[...] * (jnp.dot(h, w2_ref[...], precision=hp, preferred_element_type=F32) + b2_ref[...]))
    h = jnp.dot(h, w3_ref[0], precision=hp, preferred_element_type=F32)
    tcol = feats[:, 0:1]
    valid = feats[:, LANES - 1:LANES]
    k = h * jnp.exp(-tcol * jnp.abs(dec_ref[0])) * valid
    o_ref[0] = k[:, :W_GROUP]
    o_ref[1] = k[:, W_GROUP:]


def _hy_filters(feats, w1p, b1p, f1p, w2p, b2p, f2p, w3d, decd, layer, seq):
    rt = 512
    half = seq // rt
    c2 = lambda i: (layer, 0, 0)
    return pl.pallas_call(
        _hy_filter_kernel,
        grid=(2 * seq // rt,),
        in_specs=[
            pl.BlockSpec((rt, LANES), lambda i: (i, 0)),
            pl.BlockSpec((None, LANES, LANES), c2),
            pl.BlockSpec((None, 1, LANES), c2),
            pl.BlockSpec((None, 1, LANES), c2),
            pl.BlockSpec((None, LANES, LANES), c2),
            pl.BlockSpec((None, 1, LANES), c2),
            pl.BlockSpec((None, 1, LANES), c2),
            pl.BlockSpec((None, 1, LANES, 2 * W_GROUP), lambda i: (layer, jnp.where(i < half, 1, 0), 0, 0)),
            pl.BlockSpec((None, 1, 1, 2 * W_GROUP), lambda i: (layer, jnp.where(i < half, 1, 0), 0, 0)),
        ],
        out_specs=pl.BlockSpec((2, rt, W_GROUP), lambda i: (0, i, 0)),
        out_shape=jax.ShapeDtypeStruct((2, 2 * seq, W_GROUP), F32),
        compiler_params=_params(("parallel",)),
        name="hy_filter",
    )(feats, w1p, b1p, f1p, w2p, b2p, f2p, w3d, decd)


def _hy_fdft_kernel(k_ref, fwd_ref, o_ref, prev_ref):
    e = pl.program_id(1)
    cur = jnp.dot(fwd_ref[...], k_ref[0].astype(BF16), preferred_element_type=F32)

    @pl.when(e > 0)
    def _():
        prev = prev_ref[...]
        odd = (lax.broadcasted_iota(jnp.int32, prev.shape, 0) & 1) == 1
        o_ref[0, 0] = cur + jnp.where(odd, -prev, prev)

    prev_ref[...] = cur


def _hy_filter_dft(kseq, fwd_b, seq):
    p = HY_P
    nb = seq // p
    return pl.pallas_call(
        _hy_fdft_kernel,
        grid=(2, 2 * nb),
        in_specs=[
            pl.BlockSpec((1, p, W_GROUP), lambda o, e: (o, e, 0)),
            pl.BlockSpec((2 * p, p), lambda o, e: (0, 0)),
        ],
        out_specs=pl.BlockSpec((1, 1, 2 * p, W_GROUP), lambda o, e: (o, jnp.maximum(e - 1, 0), 0, 0)),
        out_shape=jax.ShapeDtypeStruct((2, 2 * nb - 1, 2 * p, W_GROUP), F32),
        scratch_shapes=[pltpu.VMEM((2 * p, W_GROUP), F32)],
        compiler_params=_params(("arbitrary", "arbitrary")),
        name="hy_filter_dft",
    )(kseq, fwd_b)


def _dwconv_rows(src_ref, r0, rc, total, w, b):
    halo = BF16_ROWS
    xc = src_ref[pl.ds(r0, rc), :].astype(F32)
    prev = src_ref[pl.ds(pl.multiple_of(jnp.maximum(r0 - halo, 0), halo), halo), :].astype(F32)
    nxt = src_ref[pl.ds(pl.multiple_of(jnp.minimum(r0 + rc, total - halo), halo), halo), :].astype(F32)
    prev = jnp.where(r0 > 0, prev, 0.0)
    nxt = jnp.where(r0 + rc < total, nxt, 0.0)
    cat = jnp.concatenate([prev, xc, nxt], axis=0)
    n = rc + 2 * halo
    dn = pltpu.roll(cat, 1, 0)[halo:halo + rc]
    up = pltpu.roll(cat, n - 1, 0)[halo:halo + rc]
    return dn * w[0:1] + xc * w[1:2] + up * w[2:3] + b


def _hyena_kernel(p1_ref, p2_ref, pv_ref, cw_ref, cb_ref, skip_ref, h_ref, fwd_ref, inv_ref, o_ref,
                  vz, x1z, x2z, vhat, yhat, *, seq, p):
    nb = seq // p
    rc = 256
    rm = 32

    def conv_body(c, carry):
        r0 = pl.multiple_of(c * rc, rc)
        rows = pl.ds(r0, rc)
        x1z[rows, :] = _dwconv_rows(p1_ref, r0, rc, seq, cw_ref[0], cb_ref[0:1, :])
        x2z[rows, :] = _dwconv_rows(p2_ref, r0, rc, seq, cw_ref[1], cb_ref[1:2, :])
        vz[rows, :] = _dwconv_rows(pv_ref, r0, rc, seq, cw_ref[2], cb_ref[2:3, :])
        return carry

    lax.fori_loop(0, seq // rc, conv_body, 0)

    for order in range(2):
        gate = x1z if order == 0 else x2z
        skip = skip_ref[order:order + 1, :]

        def fwd_body(i, carry):
            rows = pl.ds(pl.multiple_of(i * p, p), p)
            vhat[i] = jnp.dot(fwd_ref[...], vz[rows, :].astype(BF16), preferred_element_type=F32)
            return carry

        lax.fori_loop(0, nb, fwd_body, 0)

        def out_body(i, carry):
            def mac(rows_re, rows_im, packed_dc):
                acc_r = None
                acc_i = None
                for ip in range(nb):
                    d = i - ip + (nb - 1)
                    hr = h_ref[order, d, rows_re, :]
                    hi = h_ref[order, d, rows_im, :]
                    ur = vhat[ip, rows_re, :]
                    ui = vhat[ip, rows_im, :]
                    if packed_dc:
                        first = lax.broadcasted_iota(jnp.int32, hr.shape, 0) == 0
                        tr = hr * ur - jnp.where(first, 0.0, hi * ui)
                        ti = jnp.where(first, hi * ui, hr * ui + hi * ur)
                    else:
                        tr = hr * ur - hi * ui
                        ti = hr * ui + hi * ur
                    acc_r = tr if acc_r is None else acc_r + tr
                    acc_i = ti if acc_i is None else acc_i + ti
                return acc_r, acc_i

            def mac_body(r, c2):
                rows_re = pl.ds(pl.multiple_of(r * rm, rm), rm)
                rows_im = pl.ds(pl.multiple_of(p + r * rm, rm), rm)
                acc_r, acc_i = mac(rows_re, rows_im, False)
                yhat[rows_re, :] = acc_r.astype(BF16)
                yhat[rows_im, :] = acc_i.astype(BF16)
                return c2

            lax.fori_loop(1, p // rm, mac_body, 0)
            acc_r, acc_i = mac(pl.ds(0, rm), pl.ds(p, rm), True)
            yhat[pl.ds(0, rm), :] = acc_r.astype(BF16)
            yhat[pl.ds(p, rm), :] = acc_i.astype(BF16)

            y = jnp.dot(inv_ref[...], yhat[...], preferred_element_type=F32)
            rows = pl.ds(pl.multiple_of(i * p, p), p)
            res = gate[rows, :] * (y + skip * vz[rows, :])
            if order == 0:
                vz[rows, :] = res
            else:
                o_ref[rows, :] = res.astype(o_ref.dtype)
            return carry

        lax.fori_loop(0, nb, out_body, 0)


def _hyena(proj, cw, cb, skip, hfilt, fwd_b, inv_b, layer, seq):
    t = proj.shape[0]
    p = HY_P
    nb = seq // p
    nct = W_GROUP // LANES
    base = IN_A // LANES
    return pl.pallas_call(
        functools.partial(_hyena_kernel, seq=seq, p=p),
        grid=(nct, t // seq),
        in_specs=[
            pl.BlockSpec((seq, LANES), lambda c, b: (b, base + c)),
            pl.BlockSpec((seq, LANES), lambda c, b: (b, base + nct + c)),
            pl.BlockSpec((seq, LANES), lambda c, b: (b, base + 2 * nct + c)),
            pl.BlockSpec((None, 3, 3, LANES), lambda c, b: (layer, 0, 0, c)),
            pl.BlockSpec((None, 3, LANES), lambda c, b: (layer, 0, c)),
            pl.BlockSpec((None, 2, LANES), lambda c, b: (layer, 0, c)),
            _single((2, 2 * nb - 1, 2 * p, LANES), lambda c, b: (0, 0, 0, c)),
            _single((2 * p, p), lambda c, b: (0, 0)),
            _single((p, 2 * p), lambda c, b: (0, 0)),
        ],
        out_specs=pl.BlockSpec((seq, LANES), lambda c, b: (b, c)),
        out_shape=jax.ShapeDtypeStruct((t, W_GROUP), BF16),
        scratch_shapes=[
            pltpu.VMEM((seq, LANES), F32),
            pltpu.VMEM((seq, LANES), F32),
            pltpu.VMEM((seq, LANES), F32),
            pltpu.VMEM((nb, 2 * p, LANES), F32),
            pltpu.VMEM((2 * p, LANES), BF16),
        ],
        compiler_params=_params(("parallel", "parallel")),
        name="hyena",
    )(proj, proj, proj, cw, cb, skip, hfilt, fwd_b, inv_b)


def _rope_lanes(x, cos_t, sin_t, half):
    lane = lax.broadcasted_iota(jnp.int32, x.shape, 1)
    lower = (lane % (2 * half)) < half
    partner = jnp.where(lower, pltpu.roll(x, LANES - half, 1), pltpu.roll(x, half, 1))
    return x * cos_t + partner * sin_t


def _rms_rows(x, g, width):
    ss = jnp.sum(x * x, axis=-1, keepdims=True) * (1.0 / width)
    return x * lax.rsqrt(ss + EPS) * g


def _gqa_prep_kernel(q_ref, k_ref, cos_ref, sin_ref, qg_ref, kg_ref, qo_ref, ko_ref):
    cos_t = cos_ref[...]
    sin_t = sin_ref[...]
    scale = LOG2E / math.sqrt(C_HD)
    for h in range(C_HEADS):
        cols = slice(h * C_HD, (h + 1) * C_HD)
        q = _rope_lanes(_rms_rows(q_ref[:, cols].astype(F32), qg_ref[...], C_HD), cos_t, sin_t, C_HD // 4)
        qo_ref[:, cols] = (q * scale).astype(BF16)
    for h in range(C_KV_HEADS):
        cols = slice(h * C_HD, (h + 1) * C_HD)
        k = _rope_lanes(_rms_rows(k_ref[:, cols].astype(F32), kg_ref[...], C_HD), cos_t, sin_t, C_HD // 4)
        ko_ref[:, cols] = k.astype(BF16)


def _gqa_prep(proj, cos_t, sin_t, qg, kg, layer, seq):
    t = proj.shape[0]
    tm = 512
    qw = C_HEADS * C_HD
    kw = C_KV_HEADS * C_HD
    c0 = IN_A + IN_B
    g2 = lambda i: (layer, 0, 0)
    return pl.pallas_call(
        _gqa_prep_kernel,
        grid=(t // tm,),
        in_specs=[
            pl.BlockSpec((tm, qw), lambda i: (i, c0 // qw)),
            pl.BlockSpec((tm, kw), lambda i: (i, (c0 + qw) // kw)),
            pl.BlockSpec((tm, LANES), lambda i: (i % (seq // tm), 0)),
            pl.BlockSpec((tm, LANES), lambda i: (i % (seq // tm), 0)),
            pl.BlockSpec((None, 1, C_HD), g2),
            pl.BlockSpec((None, 1, C_HD), g2),
        ],
        out_specs=[
            pl.BlockSpec((tm, qw), lambda i: (i, 0)),
            pl.BlockSpec((tm, kw), lambda i: (i, 0)),
        ],
        out_shape=[
            jax.ShapeDtypeStruct((t, qw), BF16),
            jax.ShapeDtypeStruct((t, kw), BF16),
        ],
        compiler_params=_params(("parallel",)),
        name="gqa_prep",
    )(proj, proj, cos_t, sin_t, qg, kg)


def _mla_prep_kernel(qa_ref, ckv_ref, kr_ref, cos_ref, sin_ref, qag_ref, kvg_ref, wq_ref, wkv_ref,
                     qn_ref, kn_ref, qo_ref, ko_ref, vo_ref):
    cos_t = cos_ref[...]
    sin_t = sin_ref[...]
    width = NOPE + ROPE_D
    scale = LOG2E / math.sqrt(width)
    qa = _rms_rows(qa_ref[...].astype(F32), qag_ref[...], Q_LORA).astype(BF16)
    q = jnp.dot(qa, wq_ref[...], preferred_element_type=F32)
    ckv = _rms_rows(ckv_ref[...].astype(F32), kvg_ref[...], KV_LORA).astype(BF16)
    kv = jnp.dot(ckv, wkv_ref[...], preferred_element_type=F32)
    kr = kr_ref[...].astype(F32)
    kr_ss = jnp.sum(kr * kr, axis=-1, keepdims=True)
    qn = qn_ref[...]
    kn = kn_ref[...]
    for h in range(D_HEADS):
        base = h * MLA_SLOT
        qh = _rms_rows(q[:, base:base + MLA_SLOT], qn, width)
        qo_ref[:, base:base + NOPE] = (qh[:, :NOPE] * scale).astype(BF16)
        qr = _rope_lanes(qh[:, NOPE:], cos_t, sin_t, ROPE_D // 4)
        qo_ref[:, base + NOPE:base + MLA_SLOT] = (qr * scale).astype(BF16)
        k_nope = kv[:, base:base + NOPE]
        ss = (jnp.sum(k_nope * k_nope, axis=-1, keepdims=True) + kr_ss) * (1.0 / width)
        inv = lax.rsqrt(ss + EPS)
        ko_ref[:, base:base + NOPE] = (k_nope * inv * kn[:, :NOPE]).astype(BF16)
        krn = _rope_lanes(kr * inv * kn[:, NOPE:], cos_t, sin_t, ROPE_D // 4)
        ko_ref[:, base + NOPE:base + MLA_SLOT] = krn.astype(BF16)
        vo_ref[:, h * V_HD:(h + 1) * V_HD] = kv[:, base + NOPE:base + NOPE + V_HD].astype(BF16)


def _mla_prep(proj, cos_t, sin_t, qag, kvg, wq_b, wkv_b, qn_p, kn_p, layer, seq):
    t = proj.shape[0]
    tm = 512
    c0 = IN_A + IN_B + IN_C
    g2 = lambda i: (layer, 0, 0)
    hw = D_HEADS * MLA_SLOT
    return pl.pallas_call(
        _mla_prep_kernel,
        grid=(t // tm,),
        in_specs=[
            pl.BlockSpec((tm, Q_LORA), lambda i: (i, c0 // Q_LORA)),
            pl.BlockSpec((tm, KV_LORA), lambda i: (i, (c0 + Q_LORA) // KV_LORA)),
            pl.BlockSpec((tm, LANES), lambda i: (i, (c0 + Q_LORA + KV_LORA) // LANES)),
            pl.BlockSpec((tm, LANES), lambda i: (i % (seq // tm), 0)),
            pl.BlockSpec((tm, LANES), lambda i: (i % (seq // tm), 0)),
            pl.BlockSpec((None, 1, Q_LORA), g2),
            pl.BlockSpec((None, 1, KV_LORA), g2),
            pl.BlockSpec((None, Q_LORA, hw), g2),
            pl.BlockSpec((None, KV_LORA, hw), g2),
            pl.BlockSpec((None, 1, MLA_SLOT), g2),
            pl.BlockSpec((None, 1, MLA_SLOT), g2),
        ],
        out_specs=[
            pl.BlockSpec((tm, hw), lambda i: (i, 0)),
            pl.BlockSpec((tm, hw), lambda i: (i, 0)),
            pl.BlockSpec((tm, D_HEADS * V_HD), lambda i: (i, 0)),
        ],
        out_shape=[
            jax.ShapeDtypeStruct((t, hw), BF16),
            jax.ShapeDtypeStruct((t, hw), BF16),
            jax.ShapeDtypeStruct((t, D_HEADS * V_HD), BF16),
        ],
        compiler_params=_params(("parallel",)),
        name="mla_prep",
    )(proj, proj, proj, cos_t, sin_t, qag, kvg, wq_b, wkv_b, qn_p, kn_p)


def _attn_kernel(q_ref, k_ref, v_ref, o_ref, vext_ref, *, groups, dk, dv, sub):
    @pl.when(pl.program_id(2) == 0)
    def _():
        vext_ref[:, :dv] = v_ref[...]
        vext_ref[:, dv:] = jnp.ones((v_ref.shape[0], dv), BF16)

    k = k_ref[...]
    v = vext_ref[...]
    tq = q_ref.shape[0]
    chains = [(g, slice(r * sub, (r + 1) * sub)) for g in range(groups) for r in range(tq // sub)]

    def scores(c):
        g, rows = chains[c]
        q = q_ref[rows, g * dk:(g + 1) * dk]
        return lax.dot_general(q, k, (((1,), (1,)), ((), ())), preferred_element_type=F32)

    s = scores(0)
    for c, (g, rows) in enumerate(chains):
        s_next = scores(c + 1) if c + 1 < len(chains) else None
        m = jnp.max(s, axis=-1, keepdims=True)
        pexp = jnp.exp2(s - m).astype(BF16)
        o = jnp.dot(pexp, v, preferred_element_type=F32)
        o_ref[rows, g * dv:(g + 1) * dv] = (o[:, :dv] / o[:, dv:]).astype(o_ref.dtype)
        s = s_next


def _attention(q, k, v, v_col0, seq, kv_heads, groups, dk, dv, tq, sub, name):
    t = q.shape[0]
    nq = seq // tq
    return pl.pallas_call(
        functools.partial(_attn_kernel, groups=groups, dk=dk, dv=dv, sub=sub),
        grid=(t // seq, kv_heads, nq),
        in_specs=[
            pl.BlockSpec((tq, groups * dk), lambda b, h, i: (b * nq + i, h)),
            pl.BlockSpec((seq, dk), lambda b, h, i: (b, h)),
            pl.BlockSpec((seq, dv), lambda b, h, i: (b, v_col0 + h)),
        ],
        out_specs=pl.BlockSpec((tq, groups * dv), lambda b, h, i: (b * nq + i, h)),
        out_shape=jax.ShapeDtypeStruct((t, kv_heads * groups * dv), BF16),
        scratch_shapes=[pltpu.VMEM((seq, 2 * dv), BF16)],
        compiler_params=_params(("parallel", "parallel", "arbitrary")),
        name=name,
    )(q, k, v)


def _outproj_kernel(ya_ref, yb_ref, yc_ref, yd_ref, x_ref, mod_ref, gg_ref, w_ref, o_ref):
    acc = None
    for gi, y_ref in enumerate((ya_ref, yb_ref, yc_ref, yd_ref)):
        rows = slice(gi * W_GROUP, (gi + 1) * W_GROUP)
        yn = _rms_rows(y_ref[...].astype(F32), gg_ref[:, rows], W_GROUP).astype(BF16)
        part = jnp.dot(yn, w_ref[rows, :], preferred_element_type=F32)
        acc = part if acc is None else acc + part
    o_ref[...] = x_ref[...] + mod_ref[0, 2:3, :] * acc


def _out_proj(ya, yb, yc, yd, x, mod_l, b0, seq, gg, w_out_b, layer):
    t = x.shape[0]
    tm = 512
    yspec = pl.BlockSpec((tm, W_GROUP), lambda i: (i, 0))
    return pl.pallas_call(
        _outproj_kernel,
        grid=(t // tm,),
        in_specs=[
            yspec, yspec, yspec, yspec,
            pl.BlockSpec((tm, D_MODEL), lambda i: (i, 0)),
            pl.BlockSpec((1, 6, D_MODEL), lambda i: (b0 + (i * tm) // seq, 0, 0)),
            pl.BlockSpec((None, 1, D_MODEL), lambda i: (layer, 0, 0)),
            _single((None, D_MODEL, D_MODEL), lambda i: (layer, 0, 0)),
        ],
        out_specs=pl.BlockSpec((tm, D_MODEL), lambda i: (i, 0)),
        out_shape=jax.ShapeDtypeStruct((t, D_MODEL), F32),
        compiler_params=_params(("parallel",)),
        name="out_proj",
    )(ya, yb, yc, yd, x, mod_l, gg, w_out_b)


def _ffn_kernel(x_ref, xp_ref, xn_ref, mod_ref, g_ref, wg_ref, wu_ref, cwg_ref, cwu_ref, cbg_ref, cbu_ref,
                wd_ref, o_ref, h_ref, *, tm, seq):
    i = pl.program_id(0)
    j = pl.program_id(1)
    nj = pl.num_programs(1)
    rc = 128
    halo = FFN_HALO

    @pl.when(j == 0)
    def _():
        g = g_ref[...]
        shift = mod_ref[0, 3:4, :]
        scale = mod_ref[0, 4:5, :]
        has_prev = ((i * tm) % seq) != 0
        has_next = (((i + 1) * tm) % seq) != 0
        hp = _mod_norm_rows(xp_ref[...], g, shift, scale)
        hn = _mod_norm_rows(xn_ref[...], g, shift, scale)
        h_ref[pl.ds(0, halo), :] = jnp.where(has_prev, hp, 0.0).astype(BF16)
        h_ref[pl.ds(halo + tm, halo), :] = jnp.where(has_next, hn, 0.0).astype(BF16)

        def body(c, carry):
            r0 = pl.multiple_of(c * rc, rc)
            h_ref[pl.ds(halo + r0, rc), :] = _mod_norm_rows(x_ref[pl.ds(r0, rc), :], g, shift, scale).astype(BF16)
            return carry

        lax.fori_loop(0, tm // rc, body, 0)
        o_ref[...] = jnp.zeros(o_ref.shape, F32)

    hfull = h_ref[...]
    n = tm + 2 * halo

    def conv(u, cw_ref, cb_ref):
        dn = pltpu.roll(u, 1, 0)[halo:halo + tm]
        up = pltpu.roll(u, n - 1, 0)[halo:halo + tm]
        return dn * cw_ref[0:1, :] + u[halo:halo + tm] * cw_ref[1:2, :] + up * cw_ref[2:3, :] + cb_ref[...]

    gate = conv(jnp.dot(hfull, wg_ref[...], preferred_element_type=F32), cwg_ref, cbg_ref)
    upv = conv(jnp.dot(hfull, wu_ref[...], preferred_element_type=F32), cwu_ref, cbu_ref)
    act = (jax.nn.silu(gate) * upv).astype(BF16)
    o_ref[...] += jnp.dot(act, wd_ref[...], preferred_element_type=F32)

    @pl.when(j == nj - 1)
    def _():
        o_ref[...] = x_ref[...] + mod_ref[0, 5:6, :] * o_ref[...]


def _ffn(x, mod_l, b0, seq, g, w_up_b, cw, cb, w_down_b, layer):
    t = x.shape[0]
    tm, tf = 512, 512
    nf = D_FF // tf
    hb = tm // FFN_HALO
    last = t // FFN_HALO - 1
    return pl.pallas_call(
        functools.partial(_ffn_kernel, tm=tm, seq=seq),
        grid=(t // tm, nf),
        in_specs=[
            pl.BlockSpec((tm, D_MODEL), lambda i, j: (i, 0)),
            pl.BlockSpec((FFN_HALO, D_MODEL), lambda i, j: (jnp.maximum(i * hb - 1, 0), 0)),
            pl.BlockSpec((FFN_HALO, D_MODEL), lambda i, j: (jnp.minimum((i + 1) * hb, last), 0)),
            pl.BlockSpec((1, 6, D_MODEL), lambda i, j: (b0 + (i * tm) // seq, 0, 0)),
            pl.BlockSpec((None, 1, D_MODEL), lambda i, j: (layer, 0, 0)),
            pl.BlockSpec((None, D_MODEL, tf), lambda i, j: (layer, 0, j)),
            pl.BlockSpec((None, D_MODEL, tf), lambda i, j: (layer, 0, nf + j)),
            pl.BlockSpec((None, 3, tf), lambda i, j: (layer, 0, j)),
            pl.BlockSpec((None, 3, tf), lambda i, j: (layer, 0, nf + j)),
            pl.BlockSpec((None, 1, tf), lambda i, j: (layer, 0, j)),
            pl.BlockSpec((None, 1, tf), lambda i, j: (layer, 0, nf + j)),
            pl.BlockSpec((None, tf, D_MODEL), lambda i, j: (layer, j, 0)),
        ],
        out_specs=pl.BlockSpec((tm, D_MODEL), lambda i, j: (i, 0)),
        out_shape=jax.ShapeDtypeStruct((t, D_MODEL), F32),
        scratch_shapes=[pltpu.VMEM((tm + 2 * FFN_HALO, D_MODEL), BF16)],
        compiler_params=_params(("parallel", "arbitrary")),
        name="ffn",
    )(x, x, x, mod_l, g, w_up_b, w_up_b, cw, cw, cb, cb, w_down_b)


def _axial_tables(seq, sec, lanes_used):
    pos = jnp.arange(seq, dtype=jnp.int32)
    row = (pos // GRID_W).astype(F32)
    col = (pos % GRID_W).astype(F32)
    inv = ROPE_THETA ** (-jnp.arange(0, sec, 2, dtype=F32) / sec)
    half = sec // 2
    lane = jnp.arange(LANES)
    in_use = lane < lanes_used
    which = (lane // sec) % 2
    freq = inv[lane % half]
    ang = jnp.where(which[None, :] == 0, row[:, None], col[:, None]) * freq[None, :]
    cos_t = jnp.where(in_use[None, :], jnp.cos(ang), 1.0)
    sign = jnp.where((lane % sec) < half, -1.0, 1.0)
    sin_t = jnp.where(in_use[None, :], jnp.sin(ang) * sign[None, :], 0.0)
    return cos_t.astype(F32), sin_t.astype(F32)


def _hyena_feats(seq):
    n = jnp.arange(seq, dtype=jnp.int32)
    posi = jnp.concatenate([seq - n, n])
    valid = (posi < seq).astype(F32)
    posi = jnp.minimum(posi, seq - 1)
    tlin = jnp.linspace(0.0, 1.0, seq, dtype=F32)[posi]
    bands = jnp.linspace(1e-4, POS_BANDS - 1, POS_BANDS, dtype=F32)
    ang = (2.0 * math.pi / seq) * posi.astype(F32)[:, None] * bands[None, :]
    feats = jnp.concatenate([tlin[:, None], jnp.cos(ang), -jnp.sin(ang)], axis=-1)
    pad = jnp.zeros((2 * seq, LANES - POS_EMB - 1), F32)
    return jnp.concatenate([feats, pad, valid[:, None]], axis=-1)


def _dft_mats(p):
    r = jnp.arange(p, dtype=jnp.int32)
    q = (r[:, None] * r[None, :]) % (2 * p)
    ang = q.astype(F32) * (math.pi / p)
    cosm = jnp.cos(ang)
    sinm = jnp.sin(ang)
    alt = jnp.where(r % 2 == 0, 1.0, -1.0).astype(F32)
    im_rows = jnp.where((r == 0)[:, None], alt[None, :], -sinm)
    fwd = jnp.concatenate([cosm, im_rows], axis=0)
    wre = jnp.where((r == 0)[None, :], 0.5, 1.0) / p
    inv_re = cosm * wre
    inv_im = jnp.where((r == 0)[None, :], alt[:, None] * (0.5 / p), -sinm / p)
    inv = jnp.concatenate([inv_re, inv_im], axis=1)
    return fwd.astype(BF16), inv.astype(BF16)


def kernel(x_prompt, x_sample, c_prompt, c_sample, ada_w, ada_b, norm1_g, w_in, gm_vnorm_g, gm_spatial_w, gm_spatial_b, hy_conv_w, hy_conv_b, hy_w1, hy_b1, hy_f1, hy_w2, hy_b2, hy_f2, hy_w3, hy_decay, hy_skip, gqa_qn_g, gqa_kn_g, mla_q_a_g, mla_w_q_b, mla_kv_a_g, mla_w_kv_b, mla_qn_g, mla_kn_g, group_norm_g, w_out, norm2_g, ffn_w_up, ffn_conv_w, ffn_conv_b, ffn_w_down):
    nl = DEPTH
    w_in_b = jnp.pad(w_in, ((0, 0), (0, 0), (0, IN_PAD - IN_COLS))).astype(BF16)
    w_out_b = w_out.astype(BF16)
    w_up_b = ffn_w_up.astype(BF16)
    w_down_b = ffn_w_down.astype(BF16)
    ws_b = gm_spatial_w.astype(BF16)
    gm_bias = jnp.broadcast_to(jnp.swapaxes(gm_spatial_b, 1, 2)[:, :, :, None],
                               (nl, CHUNK, A_HEADS, LANES)).reshape(nl, CHUNK, W_GROUP)
    row3 = lambda a: a.reshape(nl, 1, a.shape[-1])
    norm1 = row3(norm1_g)
    norm2 = row3(norm2_g)
    gm_g = row3(gm_vnorm_g)
    gg = row3(group_norm_g)
    hy_cw = hy_conv_w.reshape(nl, 3, 3, W_GROUP).transpose(0, 2, 1, 3)
    hy_cb = hy_conv_b.reshape(nl, 3, W_GROUP)
    padl = lambda a, rows, cols: jnp.pad(a, ((0, 0), (0, rows - a.shape[1]), (0, cols - a.shape[2])))
    hy_w1p = padl(hy_w1, LANES, LANES)
    hy_w2p = padl(hy_w2, LANES, LANES)
    hy_b1p = padl(row3(hy_b1), 1, LANES)
    hy_f1p = padl(row3(hy_f1), 1, LANES)
    hy_b2p = padl(row3(hy_b2), 1, LANES)
    hy_f2p = padl(row3(hy_f2), 1, LANES)
    hy_w3d = jnp.pad(hy_w3.reshape(nl, FILT_H, 2, 2, W_GROUP).transpose(0, 3, 1, 2, 4)
                     .reshape(nl, 2, FILT_H, 2 * W_GROUP), ((0, 0), (0, 0), (0, LANES - FILT_H), (0, 0)))
    hy_decd = hy_decay.reshape(nl, 2, 2, W_GROUP).transpose(0, 2, 1, 3).reshape(nl, 2, 1, 2 * W_GROUP)
    gqa_qg = row3(gqa_qn_g)
    gqa_kg = row3(gqa_kn_g)
    mla_qag = row3(mla_q_a_g)
    mla_kvg = row3(mla_kv_a_g)
    slot_pad = MLA_SLOT - NOPE - ROPE_D
    wq_b = jnp.pad(mla_w_q_b.reshape(nl, Q_LORA, D_HEADS, NOPE + ROPE_D),
                   ((0, 0), (0, 0), (0, 0), (0, slot_pad))).reshape(nl, Q_LORA, D_HEADS * MLA_SLOT).astype(BF16)
    wkv_b = mla_w_kv_b.astype(BF16)
    mla_qn = jnp.pad(row3(mla_qn_g), ((0, 0), (0, 0), (0, slot_pad)))
    mla_kn = jnp.pad(row3(mla_kn_g), ((0, 0), (0, 0), (0, slot_pad)))
    ffn_cb = row3(ffn_conv_b)

    nbp = x_prompt.shape[0]
    c_all = jnp.concatenate([c_prompt, c_sample], axis=0)
    mod = _ada_mod(c_all, ada_w, ada_b).reshape(nl, c_all.shape[0], 6, D_MODEL)

    fwd_b, inv_b = _dft_mats(HY_P)

    def trunk(x3, b0):
        bsz, seq, _ = x3.shape
        x = x3.reshape(bsz * seq, D_MODEL)
        gcos, gsin = _axial_tables(seq, C_HD // 2, C_HD)
        mcos, msin = _axial_tables(seq, ROPE_D // 2, ROPE_D)
        feats = _hyena_feats(seq)
        sub = ATTN_SCORE_ELEMS // seq
        for l in range(nl):
            mod_l = mod[l]
            proj = _in_proj(x, mod_l, b0, seq, norm1, w_in_b, l)
            ya = _gmlp(proj, gm_g, ws_b, gm_bias, l)
            kseq = _hy_filters(feats, hy_w1p, hy_b1p, hy_f1p, hy_w2p, hy_b2p, hy_f2p, hy_w3d, hy_decd, l, seq)
            hfilt = _hy_filter_dft(kseq, fwd_b, seq)
            yb = _hyena(proj, hy_cw, hy_cb, hy_skip, hfilt, fwd_b, inv_b, l, seq)
            gq, gk = _gqa_prep(proj, gcos, gsin, gqa_qg, gqa_kg, l, seq)
            yc = _attention(gq, gk, proj, GQA_V_COL // C_HD, seq, C_KV_HEADS, C_HEADS // C_KV_HEADS, C_HD, C_HD,
                            2 * sub, sub, "gqa_attn")
            mq, mk, mv = _mla_prep(proj, mcos, msin, mla_qag, mla_kvg, wq_b, wkv_b, mla_qn, mla_kn, l, seq)
            yd = _attention(mq, mk, mv, 0, seq, D_HEADS, 1, MLA_SLOT, V_HD, 4 * sub, sub, "mla_attn")
            x = _out_proj(ya, yb, yc, yd, x, mod_l, b0, seq, gg, w_out_b, l)
            x = _ffn(x, mod_l, b0, seq, norm2, w_up_b, ffn_conv_w, ffn_cb, w_down_b, l)
        return x.reshape(bsz, seq, D_MODEL)

    return trunk(x_prompt, 0), trunk(x_sample, nbp)
```

```python
import functools
import math

import jax
import jax.numpy as jnp
from jax import lax
from jax.experimental import pallas as pl
from jax.experimental.pallas import tpu as pltpu

F32 = jnp.float32
BF16 = jnp.bfloat16

D_MODEL = 2048
DEPTH = 4
GRID_W = 64
CHUNK = 128
W_GROUP = 512
A_HEADS = 4
POS_BANDS = 16
POS_EMB = 2 * POS_BANDS + 1
FILT_H = 64
C_HEADS = 4
C_KV_HEADS = 2
C_HD = 128
D_HEADS = 4
Q_LORA = 512
KV_LORA = 256
NOPE = 128
ROPE_D = 64
V_HD = 128
ROPE_THETA = 10000.0
D_FF = 5632
EPS = 1e-6
LOG2E = 1.4426950408889634
IN_A = 1024
IN_B = 1536
IN_C = 1024
IN_D = 832
IN_COLS = IN_A + IN_B + IN_C + IN_D
GQA_V_COL = IN_A + IN_B + (C_HEADS + C_KV_HEADS) * C_HD
IN_PAD = 4608

LANES = 128
SUBLANES = 8
BF16_ROWS = 16
VMEM_LIMIT = 56 * 1024 * 1024

ATTN_SCORE_ELEMS = 256 * 4096
HY_P = 512
FFN_HALO = BF16_ROWS
MLA_SLOT = 256


def _params(sem, vmem=VMEM_LIMIT):
    return pltpu.CompilerParams(dimension_semantics=sem, vmem_limit_bytes=vmem)


def _single(block_shape, index_map):
    return pl.BlockSpec(block_shape, index_map, pipeline_mode=pl.Buffered(1))


def _ada_kernel(c_ref, w_ref, b_ref, o_ref):
    s = jax.nn.silu(c_ref[...]).astype(BF16)
    o_ref[0] = jnp.dot(s, w_ref[0].astype(BF16), preferred_element_type=F32) + b_ref[0]


def _ada_mod(c_all, ada_w, ada_b):
    nb = c_all.shape[0]
    tn = 1024
    return pl.pallas_call(
        _ada_kernel,
        grid=(DEPTH, 6 * D_MODEL // tn),
        in_specs=[
            pl.BlockSpec((nb, D_MODEL), lambda l, j: (0, 0)),
            pl.BlockSpec((1, D_MODEL, tn), lambda l, j: (l, 0, j)),
            pl.BlockSpec((1, 1, tn), lambda l, j: (l, 0, j)),
        ],
        out_specs=pl.BlockSpec((1, nb, tn), lambda l, j: (l, 0, j)),
        out_shape=jax.ShapeDtypeStruct((DEPTH, nb, 6 * D_MODEL), F32),
        compiler_params=_params(("parallel", "parallel")),
        name="ada_mod",
    )(c_all, ada_w, ada_b.reshape(DEPTH, 1, 6 * D_MODEL))


def _mod_norm_rows(x, g, shift, scale):
    ms = jnp.mean(x * x, axis=-1, keepdims=True)
    return (x * lax.rsqrt(ms + EPS) * g) * (1.0 + scale) + shift


def _inproj_kernel(xn_ref, x0_ref, modn_ref, mod0_ref, g_ref, w_ref, o_ref, ha_ref, hb_ref, *, tm):
    i = pl.program_id(0)
    rc = 128
    g = g_ref[...]

    def fill(x_ref, mod_ref, dst_ref):
        shift = mod_ref[0, 0:1, :]
        scale = mod_ref[0, 1:2, :]
        for c in range(tm // rc):
            rows = slice(c * rc, (c + 1) * rc)
            dst_ref[rows, :] = _mod_norm_rows(x_ref[rows, :], g, shift, scale).astype(BF16)

    @pl.when(i == 0)
    def _():
        fill(x0_ref, mod0_ref, ha_ref)

    def step(cur_ref, nxt_ref):
        fill(xn_ref, modn_ref, nxt_ref)
        o_ref[...] = jnp.dot(cur_ref[...], w_ref[...], preferred_element_type=F32).astype(o_ref.dtype)

    @pl.when(i % 2 == 0)
    def _():
        step(ha_ref, hb_ref)

    @pl.when(i % 2 == 1)
    def _():
        step(hb_ref, ha_ref)


def _in_proj(x, mod_l, b0, seq, g, w_in_b, layer):
    t = x.shape[0]
    tm = 512
    n = t // tm
    nxt = lambda i: jnp.minimum(i + 1, n - 1)
    return pl.pallas_call(
        functools.partial(_inproj_kernel, tm=tm),
        grid=(n,),
        in_specs=[
            pl.BlockSpec((tm, D_MODEL), lambda i: (nxt(i), 0)),
            _single((tm, D_MODEL), lambda i: (0, 0)),
            pl.BlockSpec((1, 6, D_MODEL), lambda i: (b0 + (nxt(i) * tm) // seq, 0, 0)),
            pl.BlockSpec((1, 6, D_MODEL), lambda i: (b0, 0, 0)),
            pl.BlockSpec((None, 1, D_MODEL), lambda i: (layer, 0, 0)),
            _single((None, D_MODEL, IN_PAD), lambda i: (layer, 0, 0)),
        ],
        out_specs=pl.BlockSpec((tm, IN_PAD), lambda i: (i, 0)),
        out_shape=jax.ShapeDtypeStruct((t, IN_PAD), BF16),
        scratch_shapes=[pltpu.VMEM((tm, D_MODEL), BF16), pltpu.VMEM((tm, D_MODEL), BF16)],
        compiler_params=_params(("arbitrary",)),
        name="in_proj",
    )(x, x, mod_l, mod_l, g, w_in_b)


def _gelu(x):
    return 0.5 * x * (1.0 + lax.erf(x * (1.0 / math.sqrt(2.0))))


def _gmlp_kernel(u_ref, v_ref, g_ref, ws_ref, bias_ref, o_ref, *, tm):
    g = g_ref[...]
    for n in range(tm // CHUNK):
        rows = slice(n * CHUNK, (n + 1) * CHUNK)
        u = _gelu(u_ref[rows, :].astype(F32))
        v = _gelu(v_ref[rows, :].astype(F32))
        vc = v - jnp.mean(v, axis=-1, keepdims=True)
        vn = (vc * lax.rsqrt(jnp.mean(vc * vc, axis=-1, keepdims=True) + EPS) * g).astype(BF16)
        for h in range(A_HEADS):
            cols = slice(h * LANES, (h + 1) * LANES)
            mixed = jnp.dot(ws_ref[h], vn[:, cols], preferred_element_type=F32) + bias_ref[:, cols]
            o_ref[rows, cols] = (u[:, cols] * mixed).astype(o_ref.dtype)


def _gmlp(proj, g, ws_b, bias_full, layer):
    t = proj.shape[0]
    tm = 512
    return pl.pallas_call(
        functools.partial(_gmlp_kernel, tm=tm),
        grid=(t // tm,),
        in_specs=[
            pl.BlockSpec((tm, W_GROUP), lambda i: (i, 0)),
            pl.BlockSpec((tm, W_GROUP), lambda i: (i, 1)),
            pl.BlockSpec((None, 1, W_GROUP), lambda i: (layer, 0, 0)),
            pl.BlockSpec((None, A_HEADS, CHUNK, CHUNK), lambda i: (layer, 0, 0, 0)),
            pl.BlockSpec((None, CHUNK, W_GROUP), lambda i: (layer, 0, 0)),
        ],
        out_specs=pl.BlockSpec((tm, W_GROUP), lambda i: (i, 0)),
        out_shape=jax.ShapeDtypeStruct((t, W_GROUP), BF16),
        compiler_params=_params(("parallel",)),
        name="gmlp",
    )(proj, proj, g, ws_b, bias_full)


def _hy_filter_kernel(f_ref, w1_ref, b1_ref, f1_ref, w2_ref, b2_ref, f2_ref, w3_ref, dec_ref, o_ref):
    hp = lax.Precision.HIGHEST
    feats = f_ref[...]
    h = jnp.sin(f1_ref[...] * (jnp.dot(feats, w1_ref[...], precision=hp, preferred_element_type=F32) + b1_ref[...]))
    h = jnp.sin(f2_ref[...] * (jnp.dot(h, w2_ref[...], precision=hp, preferred_element_type=F32) + b2_ref[...]))
    h = jnp.dot(h, w3_ref[0], precision=hp, preferred_element_type=F32)
    tcol = feats[:, 0:1]
    valid = feats[:, LANES - 1:LANES]
    k = h * jnp.exp(-tcol * jnp.abs(dec_ref[0])) * valid
    o_ref[0] = k[:, :W_GROUP]
    o_ref[1] = k[:, W_GROUP:]


def _hy_filters(feats, w1p, b1p, f1p, w2p, b2p, f2p, w3d, decd, layer, seq):
    rt = 512
    half = seq // rt
    c2 = lambda i: (layer, 0, 0)
    return pl.pallas_call(
        _hy_filter_kernel,
        grid=(2 * seq // rt,),
        in_specs=[
            pl.BlockSpec((rt, LANES), lambda i: (i, 0)),
            pl.BlockSpec((None, LANES, LANES), c2),
            pl.BlockSpec((None, 1, LANES), c2),
            pl.BlockSpec((None, 1, LANES), c2),
            pl.BlockSpec((None, LANES, LANES), c2),
            pl.BlockSpec((None, 1, LANES), c2),
            pl.BlockSpec((None, 1, LANES), c2),
            pl.BlockSpec((None, 1, LANES, 2 * W_GROUP), lambda i: (layer, jnp.where(i < half, 1, 0), 0, 0)),
            pl.BlockSpec((None, 1, 1, 2 * W_GROUP), lambda i: (layer, jnp.where(i < half, 1, 0), 0, 0)),
        ],
        out_specs=pl.BlockSpec((2, rt, W_GROUP), lambda i: (0, i, 0)),
        out_shape=jax.ShapeDtypeStruct((2, 2 * seq, W_GROUP), F32),
        compiler_params=_params(("parallel",)),
        name="hy_filter",
    )(feats, w1p, b1p, f1p, w2p, b2p, f2p, w3d, decd)


def _hy_fdft_kernel(k_ref, fwd_ref, o_ref, prev_ref):
    e = pl.program_id(1)
    cur = jnp.dot(fwd_ref[...], k_ref[0].astype(BF16), preferred_element_type=F32)

    @pl.when(e > 0)
    def _():
        prev = prev_ref[...]
        odd = (lax.broadcasted_iota(jnp.int32, prev.shape, 0) & 1) == 1
        o_ref[0, 0] = cur + jnp.where(odd, -prev, prev)

    prev_ref[...] = cur


def _hy_filter_dft(kseq, fwd_b, seq):
    p = HY_P
    nb = seq // p
    return pl.pallas_call(
        _hy_fdft_kernel,
        grid=(2, 2 * nb),
        in_specs=[
            pl.BlockSpec((1, p, W_GROUP), lambda o, e: (o, e, 0)),
            pl.BlockSpec((2 * p, p), lambda o, e: (0, 0)),
        ],
        out_specs=pl.BlockSpec((1, 1, 2 * p, W_GROUP), lambda o, e: (o, jnp.maximum(e - 1, 0), 0, 0)),
        out_shape=jax.ShapeDtypeStruct((2, 2 * nb - 1, 2 * p, W_GROUP), F32),
        scratch_shapes=[pltpu.VMEM((2 * p, W_GROUP), F32)],
        compiler_params=_params(("arbitrary", "arbitrary")),
        name="hy_filter_dft",
    )(kseq, fwd_b)


def _dwconv_rows(src_ref, r0, rc, total, w, b):
    halo = BF16_ROWS
    xc = src_ref[pl.ds(r0, rc), :].astype(F32)
    prev = src_ref[pl.ds(pl.multiple_of(jnp.maximum(r0 - halo, 0), halo), halo), :].astype(F32)
    nxt = src_ref[pl.ds(pl.multiple_of(jnp.minimum(r0 + rc, total - halo), halo), halo), :].astype(F32)
    prev = jnp.where(r0 > 0, prev, 0.0)
    nxt = jnp.where(r0 + rc < total, nxt, 0.0)
    cat = jnp.concatenate([prev, xc, nxt], axis=0)
    n = rc + 2 * halo
    dn = pltpu.roll(cat, 1, 0)[halo:halo + rc]
    up = pltpu.roll(cat, n - 1, 0)[halo:halo + rc]
    return dn * w[0:1] + xc * w[1:2] + up * w[2:3] + b


def _hyena_kernel(p1_ref, p2_ref, pv_ref, cw_ref, cb_ref, skip_ref, h_ref, fwd_ref, inv_ref, o_ref,
                  vz, x1z, x2z, vhat, yhat, *, seq, p):
    nb = seq // p
    rc = 256
    rm = 32

    def conv_body(c, carry):
        r0 = pl.multiple_of(c * rc, rc)
        rows = pl.ds(r0, rc)
        x1z[rows, :] = _dwconv_rows(p1_ref, r0, rc, seq, cw_ref[0], cb_ref[0:1, :])
        x2z[rows, :] = _dwconv_rows(p2_ref, r0, rc, seq, cw_ref[1], cb_ref[1:2, :])
        vz[rows, :] = _dwconv_rows(pv_ref, r0, rc, seq, cw_ref[2], cb_ref[2:3, :])
        return carry

    lax.fori_loop(0, seq // rc, conv_body, 0)

    for order in range(2):
        gate = x1z if order == 0 else x2z
        skip = skip_ref[order:order + 1, :]

        def fwd_body(i, carry):
            rows = pl.ds(pl.multiple_of(i * p, p), p)
            vhat[i] = jnp.dot(fwd_ref[...], vz[rows, :].astype(BF16), preferred_element_type=F32)
            return carry

        lax.fori_loop(0, nb, fwd_body, 0)

        def out_body(i, carry):
            def mac(rows_re, rows_im, packed_dc):
                acc_r = None
                acc_i = None
                for ip in range(nb):
                    d = i - ip + (nb - 1)
                    hr = h_ref[order, d, rows_re, :]
                    hi = h_ref[order, d, rows_im, :]
                    ur = vhat[ip, rows_re, :]
                    ui = vhat[ip, rows_im, :]
                    if packed_dc:
                        first = lax.broadcasted_iota(jnp.int32, hr.shape, 0) == 0
                        tr = hr * ur - jnp.where(first, 0.0, hi * ui)
                        ti = jnp.where(first, hi * ui, hr * ui + hi * ur)
                    else:
                        tr = hr * ur - hi * ui
                        ti = hr * ui + hi * ur
                    acc_r = tr if acc_r is None else acc_r + tr
                    acc_i = ti if acc_i is None else acc_i + ti
                return acc_r, acc_i

            def mac_body(r, c2):
                rows_re = pl.ds(pl.multiple_of(r * rm, rm), rm)
                rows_im = pl.ds(pl.multiple_of(p + r * rm, rm), rm)
                acc_r, acc_i = mac(rows_re, rows_im, False)
                yhat[rows_re, :] = acc_r.astype(BF16)
                yhat[rows_im, :] = acc_i.astype(BF16)
                return c2

            lax.fori_loop(1, p // rm, mac_body, 0)
            acc_r, acc_i = mac(pl.ds(0, rm), pl.ds(p, rm), True)
            yhat[pl.ds(0, rm), :] = acc_r.astype(BF16)
            yhat[pl.ds(p, rm), :] = acc_i.astype(BF16)

            y = jnp.dot(inv_ref[...], yhat[...], preferred_element_type=F32)
            rows = pl.ds(pl.multiple_of(i * p, p), p)
            res = gate[rows, :] * (y + skip * vz[rows, :])
            if order == 0:
                vz[rows, :] = res
            else:
                o_ref[rows, :] = res.astype(o_ref.dtype)
            return carry

        lax.fori_loop(0, nb, out_body, 0)


def _hyena(proj, cw, cb, skip, hfilt, fwd_b, inv_b, layer, seq):
    t = proj.shape[0]
    p = HY_P
    nb = seq // p
    nct = W_GROUP // LANES
    base = IN_A // LANES
    return pl.pallas_call(
        functools.partial(_hyena_kernel, seq=seq, p=p),
        grid=(nct, t // seq),
        in_specs=[
            pl.BlockSpec((seq, LANES), lambda c, b: (b, base + c)),
            pl.BlockSpec((seq, LANES), lambda c, b: (b, base + nct + c)),
            pl.BlockSpec((seq, LANES), lambda c, b: (b, base + 2 * nct + c)),
            pl.BlockSpec((None, 3, 3, LANES), lambda c, b: (layer, 0, 0, c)),
            pl.BlockSpec((None, 3, LANES), lambda c, b: (layer, 0, c)),
            pl.BlockSpec((None, 2, LANES), lambda c, b: (layer, 0, c)),
            _single((2, 2 * nb - 1, 2 * p, LANES), lambda c, b: (0, 0, 0, c)),
            _single((2 * p, p), lambda c, b: (0, 0)),
            _single((p, 2 * p), lambda c, b: (0, 0)),
        ],
        out_specs=pl.BlockSpec((seq, LANES), lambda c, b: (b, c)),
        out_shape=jax.ShapeDtypeStruct((t, W_GROUP), BF16),
        scratch_shapes=[
            pltpu.VMEM((seq, LANES), F32),
            pltpu.VMEM((seq, LANES), F32),
            pltpu.VMEM((seq, LANES), F32),
            pltpu.VMEM((nb, 2 * p, LANES), F32),
            pltpu.VMEM((2 * p, LANES), BF16),
        ],
        compiler_params=_params(("parallel", "parallel")),
        name="hyena",
    )(proj, proj, proj, cw, cb, skip, hfilt, fwd_b, inv_b)


def _rope_lanes(x, cos_t, sin_t, half):
    lane = lax.broadcasted_iota(jnp.int32, x.shape, 1)
    lower = (lane % (2 * half)) < half
    partner = jnp.where(lower, pltpu.roll(x, LANES - half, 1), pltpu.roll(x, half, 1))
    return x * cos_t + partner * sin_t


def _rms_rows(x, g, width):
    ss = jnp.sum(x * x, axis=-1, keepdims=True) * (1.0 / width)
    return x * lax.rsqrt(ss + EPS) * g


def _gqa_prep_kernel(q_ref, k_ref, cos_ref, sin_ref, qg_ref, kg_ref, qo_ref, ko_ref):
    cos_t = cos_ref[...]
    sin_t = sin_ref[...]
    scale = LOG2E / math.sqrt(C_HD)
    for h in range(C_HEADS):
        cols = slice(h * C_HD, (h + 1) * C_HD)
        q = _rope_lanes(_rms_rows(q_ref[:, cols].astype(F32), qg_ref[...], C_HD), cos_t, sin_t, C_HD // 4)
        qo_ref[:, cols] = (q * scale).astype(BF16)
    for h in range(C_KV_HEADS):
        cols = slice(h * C_HD, (h + 1) * C_HD)
        k = _rope_lanes(_rms_rows(k_ref[:, cols].astype(F32), kg_ref[...], C_HD), cos_t, sin_t, C_HD // 4)
        ko_ref[:, cols] = k.astype(BF16)


def _gqa_prep(proj, cos_t, sin_t, qg, kg, layer, seq):
    t = proj.shape[0]
    tm = 512
    qw = C_HEADS * C_HD
    kw = C_KV_HEADS * C_HD
    c0 = IN_A + IN_B
    g2 = lambda i: (layer, 0, 0)
    return pl.pallas_call(
        _gqa_prep_kernel,
        grid=(t // tm,),
        in_specs=[
            pl.BlockSpec((tm, qw), lambda i: (i, c0 // qw)),
            pl.BlockSpec((tm, kw), lambda i: (i, (c0 + qw) // kw)),
            pl.BlockSpec((tm, LANES), lambda i: (i % (seq // tm), 0)),
            pl.BlockSpec((tm, LANES), lambda i: (i % (seq // tm), 0)),
            pl.BlockSpec((None, 1, C_HD), g2),
            pl.BlockSpec((None, 1, C_HD), g2),
        ],
        out_specs=[
            pl.BlockSpec((tm, qw), lambda i: (i, 0)),
            pl.BlockSpec((tm, kw), lambda i: (i, 0)),
        ],
        out_shape=[
            jax.ShapeDtypeStruct((t, qw), BF16),
            jax.ShapeDtypeStruct((t, kw), BF16),
        ],
        compiler_params=_params(("parallel",)),
        name="gqa_prep",
    )(proj, proj, cos_t, sin_t, qg, kg)


def _mla_prep_kernel(qa_ref, ckv_ref, kr_ref, cos_ref, sin_ref, qag_ref, kvg_ref, wq_ref, wkv_ref,
                     qn_ref, kn_ref, qo_ref, ko_ref, vo_ref):
    cos_t = cos_ref[...]
    sin_t = sin_ref[...]
    width = NOPE + ROPE_D
    scale = LOG2E / math.sqrt(width)
    qa = _rms_rows(qa_ref[...].astype(F32), qag_ref[...], Q_LORA).astype(BF16)
    q = jnp.dot(qa, wq_ref[...], preferred_element_type=F32)
    ckv = _rms_rows(ckv_ref[...].astype(F32), kvg_ref[...], KV_LORA).astype(BF16)
    kv = jnp.dot(ckv, wkv_ref[...], preferred_element_type=F32)
    kr = kr_ref[...].astype(F32)
    kr_ss = jnp.sum(kr * kr, axis=-1, keepdims=True)
    qn = qn_ref[...]
    kn = kn_ref[...]
    for h in range(D_HEADS):
        base = h * MLA_SLOT
        qh = _rms_rows(q[:, base:base + MLA_SLOT], qn, width)
        qo_ref[:, base:base + NOPE] = (qh[:, :NOPE] * scale).astype(BF16)
        qr = _rope_lanes(qh[:, NOPE:], cos_t, sin_t, ROPE_D // 4)
        qo_ref[:, base + NOPE:base + MLA_SLOT] = (qr * scale).astype(BF16)
        k_nope = kv[:, base:base + NOPE]
        ss = (jnp.sum(k_nope * k_nope, axis=-1, keepdims=True) + kr_ss) * (1.0 / width)
        inv = lax.rsqrt(ss + EPS)
        ko_ref[:, base:base + NOPE] = (k_nope * inv * kn[:, :NOPE]).astype(BF16)
        krn = _rope_lanes(kr * inv * kn[:, NOPE:], cos_t, sin_t, ROPE_D // 4)
        ko_ref[:, base + NOPE:base + MLA_SLOT] = krn.astype(BF16)
        vo_ref[:, h * V_HD:(h + 1) * V_HD] = kv[:, base + NOPE:base + NOPE + V_HD].astype(BF16)


def _mla_prep(proj, cos_t, sin_t, qag, kvg, wq_b, wkv_b, qn_p, kn_p, layer, seq):
    t = proj.shape[0]
    tm = 512
    c0 = IN_A + IN_B + IN_C
    g2 = lambda i: (layer, 0, 0)
    hw = D_HEADS * MLA_SLOT
    return pl.pallas_call(
        _mla_prep_kernel,
        grid=(t // tm,),
        in_specs=[
            pl.BlockSpec((tm, Q_LORA), lambda i: (i, c0 // Q_LORA)),
            pl.BlockSpec((tm, KV_LORA), lambda i: (i, (c0 + Q_LORA) // KV_LORA)),
            pl.BlockSpec((tm, LANES), lambda i: (i, (c0 + Q_LORA + KV_LORA) // LANES)),
            pl.BlockSpec((tm, LANES), lambda i: (i % (seq // tm), 0)),
            pl.BlockSpec((tm, LANES), lambda i: (i % (seq // tm), 0)),
            pl.BlockSpec((None, 1, Q_LORA), g2),
            pl.BlockSpec((None, 1, KV_LORA), g2),
            pl.BlockSpec((None, Q_LORA, hw), g2),
            pl.BlockSpec((None, KV_LORA, hw), g2),
            pl.BlockSpec((None, 1, MLA_SLOT), g2),
            pl.BlockSpec((None, 1, MLA_SLOT), g2),
        ],
        out_specs=[
            pl.BlockSpec((tm, hw), lambda i: (i, 0)),
            pl.BlockSpec((tm, hw), lambda i: (i, 0)),
            pl.BlockSpec((tm, D_HEADS * V_HD), lambda i: (i, 0)),
        ],
        out_shape=[
            jax.ShapeDtypeStruct((t, hw), BF16),
            jax.ShapeDtypeStruct((t, hw), BF16),
            jax.ShapeDtypeStruct((t, D_HEADS * V_HD), BF16),
        ],
        compiler_params=_params(("parallel",)),
        name="mla_prep",
    )(proj, proj, proj, cos_t, sin_t, qag, kvg, wq_b, wkv_b, qn_p, kn_p)


def _attn_kernel(q_ref, k_ref, v_ref, o_ref, vext_ref, *, groups, dk, dv, sub):
    @pl.when(pl.program_id(2) == 0)
    def _():
        vext_ref[:, :dv] = v_ref[...]
        vext_ref[:, dv:] = jnp.ones((v_ref.shape[0], dv), BF16)

    k = k_ref[...]
    v = vext_ref[...]
    tq = q_ref.shape[0]
    chains = [(g, slice(r * sub, (r + 1) * sub)) for g in range(groups) for r in range(tq // sub)]

    def scores(c):
        g, rows = chains[c]
        q = q_ref[rows, g * dk:(g + 1) * dk]
        return lax.dot_general(q, k, (((1,), (1,)), ((), ())), preferred_element_type=F32)

    s = scores(0)
    for c, (g, rows) in enumerate(chains):
        s_next = scores(c + 1) if c + 1 < len(chains) else None
        m = jnp.max(s, axis=-1, keepdims=True)
        pexp = jnp.exp2(s - m).astype(BF16)
        o = jnp.dot(pexp, v, preferred_element_type=F32)
        o_ref[rows, g * dv:(g + 1) * dv] = (o[:, :dv] / o[:, dv:]).astype(o_ref.dtype)
        s = s_next


def _attention(q, k, v, v_col0, seq, kv_heads, groups, dk, dv, tq, sub, name):
    t = q.shape[0]
    nq = seq // tq
    return pl.pallas_call(
        functools.partial(_attn_kernel, groups=groups, dk=dk, dv=dv, sub=sub),
        grid=(t // seq, kv_heads, nq),
        in_specs=[
            pl.BlockSpec((tq, groups * dk), lambda b, h, i: (b * nq + i, h)),
            pl.BlockSpec((seq, dk), lambda b, h, i: (b, h)),
            pl.BlockSpec((seq, dv), lambda b, h, i: (b, v_col0 + h)),
        ],
        out_specs=pl.BlockSpec((tq, groups * dv), lambda b, h, i: (b * nq + i, h)),
        out_shape=jax.ShapeDtypeStruct((t, kv_heads * groups * dv), BF16),
        scratch_shapes=[pltpu.VMEM((seq, 2 * dv), BF16)],
        compiler_params=_params(("parallel", "parallel", "arbitrary")),
        name=name,
    )(q, k, v)


def _outproj_kernel(ya_ref, yb_ref, yc_ref, yd_ref, x_ref, mod_ref, gg_ref, w_ref, o_ref):
    acc = None
    for gi, y_ref in enumerate((ya_ref, yb_ref, yc_ref, yd_ref)):
        rows = slice(gi * W_GROUP, (gi + 1) * W_GROUP)
        yn = _rms_rows(y_ref[...].astype(F32), gg_ref[:, rows], W_GROUP).astype(BF16)
        part = jnp.dot(yn, w_ref[rows, :], preferred_element_type=F32)
        acc = part if acc is None else acc + part
    o_ref[...] = x_ref[...] + mod_ref[0, 2:3, :] * acc


def _out_proj(ya, yb, yc, yd, x, mod_l, b0, seq, gg, w_out_b, layer):
    t = x.shape[0]
    tm = 512
    yspec = pl.BlockSpec((tm, W_GROUP), lambda i: (i, 0))
    return pl.pallas_call(
        _outproj_kernel,
        grid=(t // tm,),
        in_specs=[
            yspec, yspec, yspec, yspec,
            pl.BlockSpec((tm, D_MODEL), lambda i: (i, 0)),
            pl.BlockSpec((1, 6, D_MODEL), lambda i: (b0 + (i * tm) // seq, 0, 0)),
            pl.BlockSpec((None, 1, D_MODEL), lambda i: (layer, 0, 0)),
            _single((None, D_MODEL, D_MODEL), lambda i: (layer, 0, 0)),
        ],
        out_specs=pl.BlockSpec((tm, D_MODEL), lambda i: (i, 0)),
        out_shape=jax.ShapeDtypeStruct((t, D_MODEL), F32),
        compiler_params=_params(("parallel",)),
        name="out_proj",
    )(ya, yb, yc, yd, x, mod_l, gg, w_out_b)


def _ffn_kernel(x_ref, xp_ref, xn_ref, mod_ref, g_ref, wg_ref, wu_ref, cwg_ref, cwu_ref, cbg_ref, cbu_ref,
                wd_ref, o_ref, h_ref, *, tm, seq):
    i = pl.program_id(0)
    j = pl.program_id(1)
    nj = pl.num_programs(1)
    rc = 128
    halo = FFN_HALO

    @pl.when(j == 0)
    def _():
        g = g_ref[...]
        shift = mod_ref[0, 3:4, :]
        scale = mod_ref[0, 4:5, :]
        has_prev = ((i * tm) % seq) != 0
        has_next = (((i + 1) * tm) % seq) != 0
        hp = _mod_norm_rows(xp_ref[...], g, shift, scale)
        hn = _mod_norm_rows(xn_ref[...], g, shift, scale)
        h_ref[pl.ds(0, halo), :] = jnp.where(has_prev, hp, 0.0).astype(BF16)
        h_ref[pl.ds(halo + tm, halo), :] = jnp.where(has_next, hn, 0.0).astype(BF16)

        def body(c, carry):
            r0 = pl.multiple_of(c * rc, rc)
            h_ref[pl.ds(halo + r0, rc), :] = _mod_norm_rows(x_ref[pl.ds(r0, rc), :], g, shift, scale).astype(BF16)
            return carry

        lax.fori_loop(0, tm // rc, body, 0)
        o_ref[...] = jnp.zeros(o_ref.shape, F32)

    hfull = h_ref[...]
    n = tm + 2 * halo

    def conv(u, cw_ref, cb_ref):
        dn = pltpu.roll(u, 1, 0)[halo:halo + tm]
        up = pltpu.roll(u, n - 1, 0)[halo:halo + tm]
        return dn * cw_ref[0:1, :] + u[halo:halo + tm] * cw_ref[1:2, :] + up * cw_ref[2:3, :] + cb_ref[...]

    gate = conv(jnp.dot(hfull, wg_ref[...], preferred_element_type=F32), cwg_ref, cbg_ref)
    upv = conv(jnp.dot(hfull, wu_ref[...], preferred_element_type=F32), cwu_ref, cbu_ref)
    act = (jax.nn.silu(gate) * upv).astype(BF16)
    o_ref[...] += jnp.dot(act, wd_ref[...], preferred_element_type=F32)

    @pl.when(j == nj - 1)
    def _():
        o_ref[...] = x_ref[...] + mod_ref[0, 5:6, :] * o_ref[...]


def _ffn(x, mod_l, b0, seq, g, w_up_b, cw, cb, w_down_b, layer):
    t = x.shape[0]
    tm, tf = 1024, 512
    nf = D_FF // tf
    hb = tm // FFN_HALO
    last = t // FFN_HALO - 1
    return pl.pallas_call(
        functools.partial(_ffn_kernel, tm=tm, seq=seq),
        grid=(t // tm, nf),
        in_specs=[
            _single((tm, D_MODEL), lambda i, j: (i, 0)),
            pl.BlockSpec((FFN_HALO, D_MODEL), lambda i, j: (jnp.maximum(i * hb - 1, 0), 0)),
            pl.BlockSpec((FFN_HALO, D_MODEL), lambda i, j: (jnp.minimum((i + 1) * hb, last), 0)),
            pl.BlockSpec((1, 6, D_MODEL), lambda i, j: (b0 + (i * tm) // seq, 0, 0)),
            pl.BlockSpec((None, 1, D_MODEL), lambda i, j: (layer, 0, 0)),
            pl.BlockSpec((None, D_MODEL, tf), lambda i, j: (layer, 0, j)),
            pl.BlockSpec((None, D_MODEL, tf), lambda i, j: (layer, 0, nf + j)),
            pl.BlockSpec((None, 3, tf), lambda i, j: (layer, 0, j)),
            pl.BlockSpec((None, 3, tf), lambda i, j: (layer, 0, nf + j)),
            pl.BlockSpec((None, 1, tf), lambda i, j: (layer, 0, j)),
            pl.BlockSpec((None, 1, tf), lambda i, j: (layer, 0, nf + j)),
            pl.BlockSpec((None, tf, D_MODEL), lambda i, j: (layer, j, 0)),
        ],
        out_specs=pl.BlockSpec((tm, D_MODEL), lambda i, j: (i, 0)),
        out_shape=jax.ShapeDtypeStruct((t, D_MODEL), F32),
        scratch_shapes=[pltpu.VMEM((tm + 2 * FFN_HALO, D_MODEL), BF16)],
        compiler_params=_params(("parallel", "arbitrary")),
        name="ffn",
    )(x, x, x, mod_l, g, w_up_b, w_up_b, cw, cw, cb, cb, w_down_b)


def _axial_tables(seq, sec, lanes_used):
    pos = jnp.arange(seq, dtype=jnp.int32)
    row = (pos // GRID_W).astype(F32)
    col = (pos % GRID_W).astype(F32)
    inv = ROPE_THETA ** (-jnp.arange(0, sec, 2, dtype=F32) / sec)
    half = sec // 2
    lane = jnp.arange(LANES)
    in_use = lane < lanes_used
    which = (lane // sec) % 2
    freq = inv[lane % half]
    ang = jnp.where(which[None, :] == 0, row[:, None], col[:, None]) * freq[None, :]
    cos_t = jnp.where(in_use[None, :], jnp.cos(ang), 1.0)
    sign = jnp.where((lane % sec) < half, -1.0, 1.0)
    sin_t = jnp.where(in_use[None, :], jnp.sin(ang) * sign[None, :], 0.0)
    return cos_t.astype(F32), sin_t.astype(F32)


def _hyena_feats(seq):
    n = jnp.arange(seq, dtype=jnp.int32)
    posi = jnp.concatenate([seq - n, n])
    valid = (posi < seq).astype(F32)
    posi = jnp.minimum(posi, seq - 1)
    tlin = jnp.linspace(0.0, 1.0, seq, dtype=F32)[posi]
    bands = jnp.linspace(1e-4, POS_BANDS - 1, POS_BANDS, dtype=F32)
    ang = (2.0 * math.pi / seq) * posi.astype(F32)[:, None] * bands[None, :]
    feats = jnp.concatenate([tlin[:, None], jnp.cos(ang), -jnp.sin(ang)], axis=-1)
    pad = jnp.zeros((2 * seq, LANES - POS_EMB - 1), F32)
    return jnp.concatenate([feats, pad, valid[:, None]], axis=-1)


def _dft_mats(p):
    r = jnp.arange(p, dtype=jnp.int32)
    q = (r[:, None] * r[None, :]) % (2 * p)
    ang = q.astype(F32) * (math.pi / p)
    cosm = jnp.cos(ang)
    sinm = jnp.sin(ang)
    alt = jnp.where(r % 2 == 0, 1.0, -1.0).astype(F32)
    im_rows = jnp.where((r == 0)[:, None], alt[None, :], -sinm)
    fwd = jnp.concatenate([cosm, im_rows], axis=0)
    wre = jnp.where((r == 0)[None, :], 0.5, 1.0) / p
    inv_re = cosm * wre
    inv_im = jnp.where((r == 0)[None, :], alt[:, None] * (0.5 / p), -sinm / p)
    inv = jnp.concatenate([inv_re, inv_im], axis=1)
    return fwd.astype(BF16), inv.astype(BF16)


def kernel(x_prompt, x_sample, c_prompt, c_sample, ada_w, ada_b, norm1_g, w_in, gm_vnorm_g, gm_spatial_w, gm_spatial_b, hy_conv_w, hy_conv_b, hy_w1, hy_b1, hy_f1, hy_w2, hy_b2, hy_f2, hy_w3, hy_decay, hy_skip, gqa_qn_g, gqa_kn_g, mla_q_a_g, mla_w_q_b, mla_kv_a_g, mla_w_kv_b, mla_qn_g, mla_kn_g, group_norm_g, w_out, norm2_g, ffn_w_up, ffn_conv_w, ffn_conv_b, ffn_w_down):
    nl = DEPTH
    w_in_b = jnp.pad(w_in, ((0, 0), (0, 0), (0, IN_PAD - IN_COLS))).astype(BF16)
    w_out_b = w_out.astype(BF16)
    w_up_b = ffn_w_up.astype(BF16)
    w_down_b = ffn_w_down.astype(BF16)
    ws_b = gm_spatial_w.astype(BF16)
    gm_bias = jnp.broadcast_to(jnp.swapaxes(gm_spatial_b, 1, 2)[:, :, :, None],
                               (nl, CHUNK, A_HEADS, LANES)).reshape(nl, CHUNK, W_GROUP)
    row3 = lambda a: a.reshape(nl, 1, a.shape[-1])
    norm1 = row3(norm1_g)
    norm2 = row3(norm2_g)
    gm_g = row3(gm_vnorm_g)
    gg = row3(group_norm_g)
    hy_cw = hy_conv_w.reshape(nl, 3, 3, W_GROUP).transpose(0, 2, 1, 3)
    hy_cb = hy_conv_b.reshape(nl, 3, W_GROUP)
    padl = lambda a, rows, cols: jnp.pad(a, ((0, 0), (0, rows - a.shape[1]), (0, cols - a.shape[2])))
    hy_w1p = padl(hy_w1, LANES, LANES)
    hy_w2p = padl(hy_w2, LANES, LANES)
    hy_b1p = padl(row3(hy_b1), 1, LANES)
    hy_f1p = padl(row3(hy_f1), 1, LANES)
    hy_b2p = padl(row3(hy_b2), 1, LANES)
    hy_f2p = padl(row3(hy_f2), 1, LANES)
    hy_w3d = jnp.pad(hy_w3.reshape(nl, FILT_H, 2, 2, W_GROUP).transpose(0, 3, 1, 2, 4)
                     .reshape(nl, 2, FILT_H, 2 * W_GROUP), ((0, 0), (0, 0), (0, LANES - FILT_H), (0, 0)))
    hy_decd = hy_decay.reshape(nl, 2, 2, W_GROUP).transpose(0, 2, 1, 3).reshape(nl, 2, 1, 2 * W_GROUP)
    gqa_qg = row3(gqa_qn_g)
    gqa_kg = row3(gqa_kn_g)
    mla_qag = row3(mla_q_a_g)
    mla_kvg = row3(mla_kv_a_g)
    slot_pad = MLA_SLOT - NOPE - ROPE_D
    wq_b = jnp.pad(mla_w_q_b.reshape(nl, Q_LORA, D_HEADS, NOPE + ROPE_D),
                   ((0, 0), (0, 0), (0, 0), (0, slot_pad))).reshape(nl, Q_LORA, D_HEADS * MLA_SLOT).astype(BF16)
    wkv_b = mla_w_kv_b.astype(BF16)
    mla_qn = jnp.pad(row3(mla_qn_g), ((0, 0), (0, 0), (0, slot_pad)))
    mla_kn = jnp.pad(row3(mla_kn_g), ((0, 0), (0, 0), (0, slot_pad)))
    ffn_cb = row3(ffn_conv_b)

    nbp = x_prompt.shape[0]
    c_all = jnp.concatenate([c_prompt, c_sample], axis=0)
    mod = _ada_mod(c_all, ada_w, ada_b).reshape(nl, c_all.shape[0], 6, D_MODEL)

    fwd_b, inv_b = _dft_mats(HY_P)

    def trunk(x3, b0):
        bsz, seq, _ = x3.shape
        x = x3.reshape(bsz * seq, D_MODEL)
        gcos, gsin = _axial_tables(seq, C_HD // 2, C_HD)
        mcos, msin = _axial_tables(seq, ROPE_D // 2, ROPE_D)
        feats = _hyena_feats(seq)
        sub = ATTN_SCORE_ELEMS // seq
        for l in range(nl):
            mod_l = mod[l]
            proj = _in_proj(x, mod_l, b0, seq, norm1, w_in_b, l)
            ya = _gmlp(proj, gm_g, ws_b, gm_bias, l)
            kseq = _hy_filters(feats, hy_w1p, hy_b1p, hy_f1p, hy_w2p, hy_b2p, hy_f2p, hy_w3d, hy_decd, l, seq)
            hfilt = _hy_filter_dft(kseq, fwd_b, seq)
            yb = _hyena(proj, hy_cw, hy_cb, hy_skip, hfilt, fwd_b, inv_b, l, seq)
            gq, gk = _gqa_prep(proj, gcos, gsin, gqa_qg, gqa_kg, l, seq)
            yc = _attention(gq, gk, proj, GQA_V_COL // C_HD, seq, C_KV_HEADS, C_HEADS // C_KV_HEADS, C_HD, C_HD,
                            2 * sub, sub, "gqa_attn")
            mq, mk, mv = _mla_prep(proj, mcos, msin, mla_qag, mla_kvg, wq_b, wkv_b, mla_qn, mla_kn, l, seq)
            yd = _attention(mq, mk, mv, 0, seq, D_HEADS, 1, MLA_SLOT, V_HD, 4 * sub, sub, "mla_attn")
            x = _out_proj(ya, yb, yc, yd, x, mod_l, b0, seq, gg, w_out_b, l)
            x = _ffn(x, mod_l, b0, seq, norm2, w_up_b, ffn_conv_w, ffn_cb, w_down_b, l)
        return x.reshape(bsz, seq, D_MODEL)

    return trunk(x_prompt, 0), trunk(x_sample, nbp)
```

```python
import functools
import math

import jax
import jax.numpy as jnp
from jax import lax
from jax.experimental import pallas as pl
from jax.experimental.pallas import tpu as pltpu

F32 = jnp.float32
BF16 = jnp.bfloat16

D_MODEL = 2048
DEPTH = 4
GRID_W = 64
CHUNK = 128
W_GROUP = 512
A_HEADS = 4
POS_BANDS = 16
POS_EMB = 2 * POS_BANDS + 1
FILT_H = 64
C_HEADS = 4
C_KV_HEADS = 2
C_HD = 128
D_HEADS = 4
Q_LORA = 512
KV_LORA = 256
NOPE = 128
ROPE_D = 64
V_HD = 128
ROPE_THETA = 10000.0
D_FF = 5632
EPS = 1e-6
LOG2E = 1.4426950408889634
IN_A = 1024
IN_B = 1536
IN_C = 1024
IN_D = 832
IN_COLS = IN_A + IN_B + IN_C + IN_D
GQA_V_COL = IN_A + IN_B + (C_HEADS + C_KV_HEADS) * C_HD
IN_PAD = 4608

LANES = 128
SUBLANES = 8
BF16_ROWS = 16
VMEM_LIMIT = 56 * 1024 * 1024

ATTN_SCORE_ELEMS = 256 * 4096
HY_P = 512
FFN_HALO = BF16_ROWS
MLA_SLOT = 256


def _params(sem, vmem=VMEM_LIMIT):
    return pltpu.CompilerParams(dimension_semantics=sem, vmem_limit_bytes=vmem)


def _single(block_shape, index_map):
    return pl.BlockSpec(block_shape, index_map, pipeline_mode=pl.Buffered(1))


def _ada_kernel(c_ref, w_ref, b_ref, o_ref):
    s = jax.nn.silu(c_ref[...]).astype(BF16)
    o_ref[0] = jnp.dot(s, w_ref[0].astype(BF16), preferred_element_type=F32) + b_ref[0]


def _ada_mod(c_all, ada_w, ada_b):
    nb = c_all.shape[0]
    tn = 1024
    return pl.pallas_call(
        _ada_kernel,
        grid=(DEPTH, 6 * D_MODEL // tn),
        in_specs=[
            pl.BlockSpec((nb, D_MODEL), lambda l, j: (0, 0)),
            pl.BlockSpec((1, D_MODEL, tn), lambda l, j: (l, 0, j)),
            pl.BlockSpec((1, 1, tn), lambda l, j: (l, 0, j)),
        ],
        out_specs=pl.BlockSpec((1, nb, tn), lambda l, j: (l, 0, j)),
        out_shape=jax.ShapeDtypeStruct((DEPTH, nb, 6 * D_MODEL), F32),
        compiler_params=_params(("parallel", "parallel")),
        name="ada_mod",
    )(c_all, ada_w, ada_b.reshape(DEPTH, 1, 6 * D_MODEL))


def _mod_norm_rows(x, g, shift, scale):
    ms = jnp.mean(x * x, axis=-1, keepdims=True)
    return (x * lax.rsqrt(ms + EPS) * g) * (1.0 + scale) + shift


def _inproj_kernel(xn_ref, x0_ref, modn_ref, mod0_ref, g_ref, w_ref, o_ref, ha_ref, hb_ref, *, tm):
    i = pl.program_id(0)
    rc = 128
    g = g_ref[...]

    def fill(x_ref, mod_ref, dst_ref):
        shift = mod_ref[0, 0:1, :]
        scale = mod_ref[0, 1:2, :]
        for c in range(tm // rc):
            rows = slice(c * rc, (c + 1) * rc)
            dst_ref[rows, :] = _mod_norm_rows(x_ref[rows, :], g, shift, scale).astype(BF16)

    @pl.when(i == 0)
    def _():
        fill(x0_ref, mod0_ref, ha_ref)

    def step(cur_ref, nxt_ref):
        fill(xn_ref, modn_ref, nxt_ref)
        o_ref[...] = jnp.dot(cur_ref[...], w_ref[...], preferred_element_type=F32).astype(o_ref.dtype)

    @pl.when(i % 2 == 0)
    def _():
        step(ha_ref, hb_ref)

    @pl.when(i % 2 == 1)
    def _():
        step(hb_ref, ha_ref)


def _in_proj(x, mod_l, b0, seq, g, w_in_b, layer):
    t = x.shape[0]
    tm = 512
    n = t // tm
    nxt = lambda i: jnp.minimum(i + 1, n - 1)
    return pl.pallas_call(
        functools.partial(_inproj_kernel, tm=tm),
        grid=(n,),
        in_specs=[
            pl.BlockSpec((tm, D_MODEL), lambda i: (nxt(i), 0)),
            _single((tm, D_MODEL), lambda i: (0, 0)),
            pl.BlockSpec((1, 6, D_MODEL), lambda i: (b0 + (nxt(i) * tm) // seq, 0, 0)),
            pl.BlockSpec((1, 6, D_MODEL), lambda i: (b0, 0, 0)),
            pl.BlockSpec((None, 1, D_MODEL), lambda i: (layer, 0, 0)),
            _single((None, D_MODEL, IN_PAD), lambda i: (layer, 0, 0)),
        ],
        out_specs=pl.BlockSpec((tm, IN_PAD), lambda i: (i, 0)),
        out_shape=jax.ShapeDtypeStruct((t, IN_PAD), BF16),
        scratch_shapes=[pltpu.VMEM((tm, D_MODEL), BF16), pltpu.VMEM((tm, D_MODEL), BF16)],
        compiler_params=_params(("arbitrary",)),
        name="in_proj",
    )(x, x, mod_l, mod_l, g, w_in_b)


def _gelu(x):
    return 0.5 * x * (1.0 + lax.erf(x * (1.0 / math.sqrt(2.0))))


def _gmlp_kernel(u_ref, v_ref, g_ref, ws_ref, bias_ref, o_ref, *, tm):
    g = g_ref[...]
    for n in range(tm // CHUNK):
        rows = slice(n * CHUNK, (n + 1) * CHUNK)
        u = _gelu(u_ref[rows, :].astype(F32))
        v = _gelu(v_ref[rows, :].astype(F32))
        vc = v - jnp.mean(v, axis=-1, keepdims=True)
        vn = (vc * lax.rsqrt(jnp.mean(vc * vc, axis=-1, keepdims=True) + EPS) * g).astype(BF16)
        for h in range(A_HEADS):
            cols = slice(h * LANES, (h + 1) * LANES)
            mixed = jnp.dot(ws_ref[h], vn[:, cols], preferred_element_type=F32) + bias_ref[:, cols]
            o_ref[rows, cols] = (u[:, cols] * mixed).astype(o_ref.dtype)


def _gmlp(proj, g, ws_b, bias_full, layer):
    t = proj.shape[0]
    tm = 512
    return pl.pallas_call(
        functools.partial(_gmlp_kernel, tm=tm),
        grid=(t // tm,),
        in_specs=[
            pl.BlockSpec((tm, W_GROUP), lambda i: (i, 0)),
            pl.BlockSpec((tm, W_GROUP), lambda i: (i, 1)),
            pl.BlockSpec((None, 1, W_GROUP), lambda i: (layer, 0, 0)),
            pl.BlockSpec((None, A_HEADS, CHUNK, CHUNK), lambda i: (layer, 0, 0, 0)),
            pl.BlockSpec((None, CHUNK, W_GROUP), lambda i: (layer, 0, 0)),
        ],
        out_specs=pl.BlockSpec((tm, W_GROUP), lambda i: (i, 0)),
        out_shape=jax.ShapeDtypeStruct((t, W_GROUP), BF16),
        compiler_params=_params(("parallel",)),
        name="gmlp",
    )(proj, proj, g, ws_b, bias_full)


def _hy_filter_kernel(f_ref, w1_ref, b1_ref, f1_ref, w2_ref, b2_ref, f2_ref, w3_ref, dec_ref, o_ref):
    hp = lax.Precision.HIGHEST
    feats = f_ref[...]
    h = jnp.sin(f1_ref[...] * (jnp.dot(feats, w1_ref[...], precision=hp, preferred_element_type=F32) + b1_ref[...]))
    h = jnp.sin(f2_ref[...] * (jnp.dot(h, w2_ref[...], precision=hp, preferred_element_type=F32) + b2_ref[...]))
    h = jnp.dot(h, w3_ref[0], precision=hp, preferred_element_type=F32)
    tcol = feats[:, 0:1]
    valid = feats[:, LANES - 1:LANES]
    k = h * jnp.exp(-tcol * jnp.abs(dec_ref[0])) * valid
    o_ref[0] = k[:, :W_GROUP]
    o_ref[1] = k[:, W_GROUP:]


def _hy_filters(feats, w1p, b1p, f1p, w2p, b2p, f2p, w3d, decd, layer, seq):
    rt = 512
    half = seq // rt
    c2 = lambda i: (layer, 0, 0)
    return pl.pallas_call(
        _hy_filter_kernel,
        grid=(2 * seq // rt,),
        in_specs=[
            pl.BlockSpec((rt, LANES), lambda i: (i, 0)),
            pl.BlockSpec((None, LANES, LANES), c2),
            pl.BlockSpec((None, 1, LANES), c2),
            pl.BlockSpec((None, 1, LANES), c2),
            pl.BlockSpec((None, LANES, LANES), c2),
            pl.BlockSpec((None, 1, LANES), c2),
            pl.BlockSpec((None, 1, LANES), c2),
            pl.BlockSpec((None, 1, LANES, 2 * W_GROUP), lambda i: (layer, jnp.where(i < half, 1, 0), 0, 0)),
            pl.BlockSpec((None, 1, 1, 2 * W_GROUP), lambda i: (layer, jnp.where(i < half, 1, 0), 0, 0)),
        ],
        out_specs=pl.BlockSpec((2, rt, W_GROUP), lambda i: (0, i, 0)),
        out_shape=jax.ShapeDtypeStruct((2, 2 * seq, W_GROUP), F32),
        compiler_params=_params(("parallel",)),
        name="hy_filter",
    )(feats, w1p, b1p, f1p, w2p, b2p, f2p, w3d, decd)


def _hy_fdft_kernel(k_ref, fwd_ref, o_ref, prev_ref):
    e = pl.program_id(1)
    cur = jnp.dot(fwd_ref[...], k_ref[0].astype(BF16), preferred_element_type=F32)

    @pl.when(e > 0)
    def _():
        prev = prev_ref[...]
        odd = (lax.broadcasted_iota(jnp.int32, prev.shape, 0) & 1) == 1
        o_ref[0, 0] = cur + jnp.where(odd, -prev, prev)

    prev_ref[...] = cur


def _hy_filter_dft(kseq, fwd_b, seq):
    p = HY_P
    nb = seq // p
    return pl.pallas_call(
        _hy_fdft_kernel,
        grid=(2, 2 * nb),
        in_specs=[
            pl.BlockSpec((1, p, W_GROUP), lambda o, e: (o, e, 0)),
            pl.BlockSpec((2 * p, p), lambda o, e: (0, 0)),
        ],
        out_specs=pl.BlockSpec((1, 1, 2 * p, W_GROUP), lambda o, e: (o, jnp.maximum(e - 1, 0), 0, 0)),
        out_shape=jax.ShapeDtypeStruct((2, 2 * nb - 1, 2 * p, W_GROUP), F32),
        scratch_shapes=[pltpu.VMEM((2 * p, W_GROUP), F32)],
        compiler_params=_params(("arbitrary", "arbitrary")),
        name="hy_filter_dft",
    )(kseq, fwd_b)


def _dwconv_rows(src_ref, r0, rc, total, w, b):
    halo = BF16_ROWS
    xc = src_ref[pl.ds(r0, rc), :].astype(F32)
    prev = src_ref[pl.ds(pl.multiple_of(jnp.maximum(r0 - halo, 0), halo), halo), :].astype(F32)
    nxt = src_ref[pl.ds(pl.multiple_of(jnp.minimum(r0 + rc, total - halo), halo), halo), :].astype(F32)
    prev = jnp.where(r0 > 0, prev, 0.0)
    nxt = jnp.where(r0 + rc < total, nxt, 0.0)
    cat = jnp.concatenate([prev, xc, nxt], axis=0)
    n = rc + 2 * halo
    dn = pltpu.roll(cat, 1, 0)[halo:halo + rc]
    up = pltpu.roll(cat, n - 1, 0)[halo:halo + rc]
    return dn * w[0:1] + xc * w[1:2] + up * w[2:3] + b


def _hyena_kernel(p1_ref, p2_ref, pv_ref, cw_ref, cb_ref, skip_ref, h_ref, fwd_ref, inv_ref, o_ref,
                  vz, x1z, x2z, vhat0, vhat1, yhat_a, yhat_b, *, seq, p):
    nb = seq // p
    rc = 256
    rm = 32
    yhat = (yhat_a, yhat_b)

    def conv_body(c, carry):
        r0 = pl.multiple_of(c * rc, rc)
        rows = pl.ds(r0, rc)
        x1z[rows, :] = _dwconv_rows(p1_ref, r0, rc, seq, cw_ref[0], cb_ref[0:1, :])
        x2z[rows, :] = _dwconv_rows(p2_ref, r0, rc, seq, cw_ref[1], cb_ref[1:2, :])
        vz[rows, :] = _dwconv_rows(pv_ref, r0, rc, seq, cw_ref[2], cb_ref[2:3, :])
        return carry

    lax.fori_loop(0, seq // rc, conv_body, 0)

    def fwd_body(i, carry):
        rows = pl.ds(pl.multiple_of(i * p, p), p)
        vhat0[i] = jnp.dot(fwd_ref[...], vz[rows, :].astype(BF16), preferred_element_type=F32)
        return carry

    lax.fori_loop(0, nb, fwd_body, 0)

    def block_spectrum(order, i, vhat, dst):
        for r in range(p // rm):
            rows_re = slice(r * rm, (r + 1) * rm)
            rows_im = slice(p + r * rm, p + (r + 1) * rm)
            acc_r = None
            acc_i = None
            for ip in range(nb):
                d = i - ip + (nb - 1)
                hr = h_ref[order, d, rows_re, :]
                hi = h_ref[order, d, rows_im, :]
                ur = vhat[ip, rows_re, :]
                ui = vhat[ip, rows_im, :]
                if r == 0:
                    first = lax.broadcasted_iota(jnp.int32, hr.shape, 0) == 0
                    tr = hr * ur - jnp.where(first, 0.0, hi * ui)
                    ti = jnp.where(first, hi * ui, hr * ui + hi * ur)
                else:
                    tr = hr * ur - hi * ui
                    ti = hr * ui + hi * ur
                acc_r = tr if acc_r is None else acc_r + tr
                acc_i = ti if acc_i is None else acc_i + ti
            dst[rows_re, :] = acc_r.astype(BF16)
            dst[rows_im, :] = acc_i.astype(BF16)

    for order in range(2):
        gate = x1z if order == 0 else x2z
        vhat = vhat0 if order == 0 else vhat1
        skip = skip_ref[order:order + 1, :]
        for i in range(nb + 1):
            if i >= 1:
                y = jnp.dot(inv_ref[...], yhat[(i - 1) % 2][...], preferred_element_type=F32)
            if i < nb:
                block_spectrum(order, i, vhat, yhat[i % 2])
            if i >= 1:
                rows = slice((i - 1) * p, i * p)
                res = gate[rows, :] * (y + skip * vz[rows, :])
                if order == 0:
                    vz[rows, :] = res
                    vhat1[i - 1] = jnp.dot(fwd_ref[...], res.astype(BF16), preferred_element_type=F32)
                else:
                    o_ref[rows, :] = res.astype(o_ref.dtype)


def _hyena(proj, cw, cb, skip, hfilt, fwd_b, inv_b, layer, seq):
    t = proj.shape[0]
    p = HY_P
    nb = seq // p
    nct = W_GROUP // LANES
    base = IN_A // LANES
    return pl.pallas_call(
        functools.partial(_hyena_kernel, seq=seq, p=p),
        grid=(nct, t // seq),
        in_specs=[
            pl.BlockSpec((seq, LANES), lambda c, b: (b, base + c)),
            pl.BlockSpec((seq, LANES), lambda c, b: (b, base + nct + c)),
            pl.BlockSpec((seq, LANES), lambda c, b: (b, base + 2 * nct + c)),
            pl.BlockSpec((None, 3, 3, LANES), lambda c, b: (layer, 0, 0, c)),
            pl.BlockSpec((None, 3, LANES), lambda c, b: (layer, 0, c)),
            pl.BlockSpec((None, 2, LANES), lambda c, b: (layer, 0, c)),
            _single((2, 2 * nb - 1, 2 * p, LANES), lambda c, b: (0, 0, 0, c)),
            _single((2 * p, p), lambda c, b: (0, 0)),
            _single((p, 2 * p), lambda c, b: (0, 0)),
        ],
        out_specs=pl.BlockSpec((seq, LANES), lambda c, b: (b, c)),
        out_shape=jax.ShapeDtypeStruct((t, W_GROUP), BF16),
        scratch_shapes=[
            pltpu.VMEM((seq, LANES), F32),
            pltpu.VMEM((seq, LANES), F32),
            pltpu.VMEM((seq, LANES), F32),
            pltpu.VMEM((nb, 2 * p, LANES), F32),
            pltpu.VMEM((nb, 2 * p, LANES), F32),
            pltpu.VMEM((2 * p, LANES), BF16),
            pltpu.VMEM((2 * p, LANES), BF16),
        ],
        compiler_params=_params(("parallel", "parallel")),
        name="hyena",
    )(proj, proj, proj, cw, cb, skip, hfilt, fwd_b, inv_b)


def _rope_lanes(x, cos_t, sin_t, half):
    lane = lax.broadcasted_iota(jnp.int32, x.shape, 1)
    lower = (lane % (2 * half)) < half
    partner = jnp.where(lower, pltpu.roll(x, LANES - half, 1), pltpu.roll(x, half, 1))
    return x * cos_t + partner * sin_t


def _rms_rows(x, g, width):
    ss = jnp.sum(x * x, axis=-1, keepdims=True) * (1.0 / width)
    return x * lax.rsqrt(ss + EPS) * g


def _gqa_prep_kernel(q_ref, k_ref, cos_ref, sin_ref, qg_ref, kg_ref, qo_ref, ko_ref):
    cos_t = cos_ref[...]
    sin_t = sin_ref[...]
    scale = LOG2E / math.sqrt(C_HD)
    for h in range(C_HEADS):
        cols = slice(h * C_HD, (h + 1) * C_HD)
        q = _rope_lanes(_rms_rows(q_ref[:, cols].astype(F32), qg_ref[...], C_HD), cos_t, sin_t, C_HD // 4)
        qo_ref[:, cols] = (q * scale).astype(BF16)
    for h in range(C_KV_HEADS):
        cols = slice(h * C_HD, (h + 1) * C_HD)
        k = _rope_lanes(_rms_rows(k_ref[:, cols].astype(F32), kg_ref[...], C_HD), cos_t, sin_t, C_HD // 4)
        ko_ref[:, cols] = k.astype(BF16)


def _gqa_prep(proj, cos_t, sin_t, qg, kg, layer, seq):
    t = proj.shape[0]
    tm = 512
    qw = C_HEADS * C_HD
    kw = C_KV_HEADS * C_HD
    c0 = IN_A + IN_B
    g2 = lambda i: (layer, 0, 0)
    return pl.pallas_call(
        _gqa_prep_kernel,
        grid=(t // tm,),
        in_specs=[
            pl.BlockSpec((tm, qw), lambda i: (i, c0 // qw)),
            pl.BlockSpec((tm, kw), lambda i: (i, (c0 + qw) // kw)),
            pl.BlockSpec((tm, LANES), lambda i: (i % (seq // tm), 0)),
            pl.BlockSpec((tm, LANES), lambda i: (i % (seq // tm), 0)),
            pl.BlockSpec((None, 1, C_HD), g2),
            pl.BlockSpec((None, 1, C_HD), g2),
        ],
        out_specs=[
            pl.BlockSpec((tm, qw), lambda i: (i, 0)),
            pl.BlockSpec((tm, kw), lambda i: (i, 0)),
        ],
        out_shape=[
            jax.ShapeDtypeStruct((t, qw), BF16),
            jax.ShapeDtypeStruct((t, kw), BF16),
        ],
        compiler_params=_params(("parallel",)),
        name="gqa_prep",
    )(proj, proj, cos_t, sin_t, qg, kg)


def _mla_prep_kernel(qa_ref, ckv_ref, kr_ref, cos_ref, sin_ref, qag_ref, kvg_ref, wq_ref, wkv_ref,
                     qn_ref, kn_ref, qo_ref, ko_ref, vo_ref):
    cos_t = cos_ref[...]
    sin_t = sin_ref[...]
    width = NOPE + ROPE_D
    scale = LOG2E / math.sqrt(width)
    qa = _rms_rows(qa_ref[...].astype(F32), qag_ref[...], Q_LORA).astype(BF16)
    q = jnp.dot(qa, wq_ref[...], preferred_element_type=F32)
    ckv = _rms_rows(ckv_ref[...].astype(F32), kvg_ref[...], KV_LORA).astype(BF16)
    kv = jnp.dot(ckv, wkv_ref[...], preferred_element_type=F32)
    kr = kr_ref[...].astype(F32)
    kr_ss = jnp.sum(kr * kr, axis=-1, keepdims=True)
    qn = qn_ref[...]
    kn = kn_ref[...]
    for h in range(D_HEADS):
        base = h * MLA_SLOT
        qh = _rms_rows(q[:, base:base + MLA_SLOT], qn, width)
        qo_ref[:, base:base + NOPE] = (qh[:, :NOPE] * scale).astype(BF16)
        qr = _rope_lanes(qh[:, NOPE:], cos_t, sin_t, ROPE_D // 4)
        qo_ref[:, base + NOPE:base + MLA_SLOT] = (qr * scale).astype(BF16)
        k_nope = kv[:, base:base + NOPE]
        ss = (jnp.sum(k_nope * k_nope, axis=-1, keepdims=True) + kr_ss) * (1.0 / width)
        inv = lax.rsqrt(ss + EPS)
        ko_ref[:, base:base + NOPE] = (k_nope * inv * kn[:, :NOPE]).astype(BF16)
        krn = _rope_lanes(kr * inv * kn[:, NOPE:], cos_t, sin_t, ROPE_D // 4)
        ko_ref[:, base + NOPE:base + MLA_SLOT] = krn.astype(BF16)
        vo_ref[:, h * V_HD:(h + 1) * V_HD] = kv[:, base + NOPE:base + NOPE + V_HD].astype(BF16)


def _mla_prep(proj, cos_t, sin_t, qag, kvg, wq_b, wkv_b, qn_p, kn_p, layer, seq):
    t = proj.shape[0]
    tm = 512
    c0 = IN_A + IN_B + IN_C
    g2 = lambda i: (layer, 0, 0)
    hw = D_HEADS * MLA_SLOT
    return pl.pallas_call(
        _mla_prep_kernel,
        grid=(t // tm,),
        in_specs=[
            pl.BlockSpec((tm, Q_LORA), lambda i: (i, c0 // Q_LORA)),
            pl.BlockSpec((tm, KV_LORA), lambda i: (i, (c0 + Q_LORA) // KV_LORA)),
            pl.BlockSpec((tm, LANES), lambda i: (i, (c0 + Q_LORA + KV_LORA) // LANES)),
            pl.BlockSpec((tm, LANES), lambda i: (i % (seq // tm), 0)),
            pl.BlockSpec((tm, LANES), lambda i: (i % (seq // tm), 0)),
            pl.BlockSpec((None, 1, Q_LORA), g2),
            pl.BlockSpec((None, 1, KV_LORA), g2),
            pl.BlockSpec((None, Q_LORA, hw), g2),
            pl.BlockSpec((None, KV_LORA, hw), g2),
            pl.BlockSpec((None, 1, MLA_SLOT), g2),
            pl.BlockSpec((None, 1, MLA_SLOT), g2),
        ],
        out_specs=[
            pl.BlockSpec((tm, hw), lambda i: (i, 0)),
            pl.BlockSpec((tm, hw), lambda i: (i, 0)),
            pl.BlockSpec((tm, D_HEADS * V_HD), lambda i: (i, 0)),
        ],
        out_shape=[
            jax.ShapeDtypeStruct((t, hw), BF16),
            jax.ShapeDtypeStruct((t, hw), BF16),
            jax.ShapeDtypeStruct((t, D_HEADS * V_HD), BF16),
        ],
        compiler_params=_params(("parallel",)),
        name="mla_prep",
    )(proj, proj, proj, cos_t, sin_t, qag, kvg, wq_b, wkv_b, qn_p, kn_p)


def _attn_kernel(q_ref, k_ref, v_ref, o_ref, vext_ref, *, groups, dk, dv, sub):
    @pl.when(pl.program_id(2) == 0)
    def _():
        vext_ref[:, :dv] = v_ref[...]
        vext_ref[:, dv:] = jnp.ones((v_ref.shape[0], dv), BF16)

    k = k_ref[...]
    v = vext_ref[...]
    tq = q_ref.shape[0]
    chains = [(g, slice(r * sub, (r + 1) * sub)) for g in range(groups) for r in range(tq // sub)]

    def scores(c):
        g, rows = chains[c]
        q = q_ref[rows, g * dk:(g + 1) * dk]
        return lax.dot_general(q, k, (((1,), (1,)), ((), ())), preferred_element_type=F32)

    s = scores(0)
    for c, (g, rows) in enumerate(chains):
        s_next = scores(c + 1) if c + 1 < len(chains) else None
        m = jnp.max(s, axis=-1, keepdims=True)
        pexp = jnp.exp2(s - m).astype(BF16)
        o = jnp.dot(pexp, v, preferred_element_type=F32)
        o_ref[rows, g * dv:(g + 1) * dv] = (o[:, :dv] / o[:, dv:]).astype(o_ref.dtype)
        s = s_next


def _attention(q, k, v, v_col0, seq, kv_heads, groups, dk, dv, tq, sub, name):
    t = q.shape[0]
    nq = seq // tq
    return pl.pallas_call(
        functools.partial(_attn_kernel, groups=groups, dk=dk, dv=dv, sub=sub),
        grid=(t // seq, kv_heads, nq),
        in_specs=[
            pl.BlockSpec((tq, groups * dk), lambda b, h, i: (b * nq + i, h)),
            pl.BlockSpec((seq, dk), lambda b, h, i: (b, h)),
            pl.BlockSpec((seq, dv), lambda b, h, i: (b, v_col0 + h)),
        ],
        out_specs=pl.BlockSpec((tq, groups * dv), lambda b, h, i: (b * nq + i, h)),
        out_shape=jax.ShapeDtypeStruct((t, kv_heads * groups * dv), BF16),
        scratch_shapes=[pltpu.VMEM((seq, 2 * dv), BF16)],
        compiler_params=_params(("parallel", "parallel", "arbitrary")),
        name=name,
    )(q, k, v)


def _outproj_kernel(ya_ref, yb_ref, yc_ref, yd_ref, x_ref, mod_ref, gg_ref, w_ref, o_ref):
    acc = None
    for gi, y_ref in enumerate((ya_ref, yb_ref, yc_ref, yd_ref)):
        rows = slice(gi * W_GROUP, (gi + 1) * W_GROUP)
        yn = _rms_rows(y_ref[...].astype(F32), gg_ref[:, rows], W_GROUP).astype(BF16)
        part = jnp.dot(yn, w_ref[rows, :], preferred_element_type=F32)
        acc = part if acc is None else acc + part
    o_ref[...] = x_ref[...] + mod_ref[0, 2:3, :] * acc


def _out_proj(ya, yb, yc, yd, x, mod_l, b0, seq, gg, w_out_b, layer):
    t = x.shape[0]
    tm = 512
    yspec = pl.BlockSpec((tm, W_GROUP), lambda i: (i, 0))
    return pl.pallas_call(
        _outproj_kernel,
        grid=(t // tm,),
        in_specs=[
            yspec, yspec, yspec, yspec,
            pl.BlockSpec((tm, D_MODEL), lambda i: (i, 0)),
            pl.BlockSpec((1, 6, D_MODEL), lambda i: (b0 + (i * tm) // seq, 0, 0)),
            pl.BlockSpec((None, 1, D_MODEL), lambda i: (layer, 0, 0)),
            _single((None, D_MODEL, D_MODEL), lambda i: (layer, 0, 0)),
        ],
        out_specs=pl.BlockSpec((tm, D_MODEL), lambda i: (i, 0)),
        out_shape=jax.ShapeDtypeStruct((t, D_MODEL), F32),
        compiler_params=_params(("parallel",)),
        name="out_proj",
    )(ya, yb, yc, yd, x, mod_l, gg, w_out_b)


def _ffn_kernel(x_ref, xp_ref, xn_ref, mod_ref, g_ref, wg_ref, wu_ref, cwg_ref, cwu_ref, cbg_ref, cbu_ref,
                wd_ref, o_ref, h_ref, *, tm, seq):
    i = pl.program_id(0)
    j = pl.program_id(1)
    nj = pl.num_programs(1)
    rc = 128
    halo = FFN_HALO

    @pl.when(j == 0)
    def _():
        g = g_ref[...]
        shift = mod_ref[0, 3:4, :]
        scale = mod_ref[0, 4:5, :]
        has_prev = ((i * tm) % seq) != 0
        has_next = (((i + 1) * tm) % seq) != 0
        hp = _mod_norm_rows(xp_ref[...], g, shift, scale)
        hn = _mod_norm_rows(xn_ref[...], g, shift, scale)
        h_ref[pl.ds(0, halo), :] = jnp.where(has_prev, hp, 0.0).astype(BF16)
        h_ref[pl.ds(halo + tm, halo), :] = jnp.where(has_next, hn, 0.0).astype(BF16)

        def body(c, carry):
            r0 = pl.multiple_of(c * rc, rc)
            h_ref[pl.ds(halo + r0, rc), :] = _mod_norm_rows(x_ref[pl.ds(r0, rc), :], g, shift, scale).astype(BF16)
            return carry

        lax.fori_loop(0, tm // rc, body, 0)
        o_ref[...] = jnp.zeros(o_ref.shape, F32)

    hfull = h_ref[...]
    n = tm + 2 * halo

    def conv(u, cw_ref, cb_ref):
        dn = pltpu.roll(u, 1, 0)[halo:halo + tm]
        up = pltpu.roll(u, n - 1, 0)[halo:halo + tm]
        return dn * cw_ref[0:1, :] + u[halo:halo + tm] * cw_ref[1:2, :] + up * cw_ref[2:3, :] + cb_ref[...]

    gate = conv(jnp.dot(hfull, wg_ref[...], preferred_element_type=F32), cwg_ref, cbg_ref)
    upv = conv(jnp.dot(hfull, wu_ref[...], preferred_element_type=F32), cwu_ref, cbu_ref)
    act = (jax.nn.silu(gate) * upv).astype(BF16)
    o_ref[...] += jnp.dot(act, wd_ref[...], preferred_element_type=F32)

    @pl.when(j == nj - 1)
    def _():
        o_ref[...] = x_ref[...] + mod_ref[0, 5:6, :] * o_ref[...]


def _ffn(x, mod_l, b0, seq, g, w_up_b, cw, cb, w_down_b, layer):
    t = x.shape[0]
    tm, tf = 1024, 512
    nf = D_FF // tf
    hb = tm // FFN_HALO
    last = t // FFN_HALO - 1
    return pl.pallas_call(
        functools.partial(_ffn_kernel, tm=tm, seq=seq),
        grid=(t // tm, nf),
        in_specs=[
            _single((tm, D_MODEL), lambda i, j: (i, 0)),
            pl.BlockSpec((FFN_HALO, D_MODEL), lambda i, j: (jnp.maximum(i * hb - 1, 0), 0)),
            pl.BlockSpec((FFN_HALO, D_MODEL), lambda i, j: (jnp.minimum((i + 1) * hb, last), 0)),
            pl.BlockSpec((1, 6, D_MODEL), lambda i, j: (b0 + (i * tm) // seq, 0, 0)),
            pl.BlockSpec((None, 1, D_MODEL), lambda i, j: (layer, 0, 0)),
            pl.BlockSpec((None, D_MODEL, tf), lambda i, j: (layer, 0, j)),
            pl.BlockSpec((None, D_MODEL, tf), lambda i, j: (layer, 0, nf + j)),
            pl.BlockSpec((None, 3, tf), lambda i, j: (layer, 0, j)),
            pl.BlockSpec((None, 3, tf), lambda i, j: (layer, 0, nf + j)),
            pl.BlockSpec((None, 1, tf), lambda i, j: (layer, 0, j)),
            pl.BlockSpec((None, 1, tf), lambda i, j: (layer, 0, nf + j)),
            pl.BlockSpec((None, tf, D_MODEL), lambda i, j: (layer, j, 0)),
        ],
        out_specs=pl.BlockSpec((tm, D_MODEL), lambda i, j: (i, 0)),
        out_shape=jax.ShapeDtypeStruct((t, D_MODEL), F32),
        scratch_shapes=[pltpu.VMEM((tm + 2 * FFN_HALO, D_MODEL), BF16)],
        compiler_params=_params(("parallel", "arbitrary")),
        name="ffn",
    )(x, x, x, mod_l, g, w_up_b, w_up_b, cw, cw, cb, cb, w_down_b)


def _axial_tables(seq, sec, lanes_used):
    pos = jnp.arange(seq, dtype=jnp.int32)
    row = (pos // GRID_W).astype(F32)
    col = (pos % GRID_W).astype(F32)
    inv = ROPE_THETA ** (-jnp.arange(0, sec, 2, dtype=F32) / sec)
    half = sec // 2
    lane = jnp.arange(LANES)
    in_use = lane < lanes_used
    which = (lane // sec) % 2
    freq = inv[lane % half]
    ang = jnp.where(which[None, :] == 0, row[:, None], col[:, None]) * freq[None, :]
    cos_t = jnp.where(in_use[None, :], jnp.cos(ang), 1.0)
    sign = jnp.where((lane % sec) < half, -1.0, 1.0)
    sin_t = jnp.where(in_use[None, :], jnp.sin(ang) * sign[None, :], 0.0)
    return cos_t.astype(F32), sin_t.astype(F32)


def _hyena_feats(seq):
    n = jnp.arange(seq, dtype=jnp.int32)
    posi = jnp.concatenate([seq - n, n])
    valid = (posi < seq).astype(F32)
    posi = jnp.minimum(posi, seq - 1)
    tlin = jnp.linspace(0.0, 1.0, seq, dtype=F32)[posi]
    bands = jnp.linspace(1e-4, POS_BANDS - 1, POS_BANDS, dtype=F32)
    ang = (2.0 * math.pi / seq) * posi.astype(F32)[:, None] * bands[None, :]
    feats = jnp.concatenate([tlin[:, None], jnp.cos(ang), -jnp.sin(ang)], axis=-1)
    pad = jnp.zeros((2 * seq, LANES - POS_EMB - 1), F32)
    return jnp.concatenate([feats, pad, valid[:, None]], axis=-1)


def _dft_mats(p):
    r = jnp.arange(p, dtype=jnp.int32)
    q = (r[:, None] * r[None, :]) % (2 * p)
    ang = q.astype(F32) * (math.pi / p)
    cosm = jnp.cos(ang)
    sinm = jnp.sin(ang)
    alt = jnp.where(r % 2 == 0, 1.0, -1.0).astype(F32)
    im_rows = jnp.where((r == 0)[:, None], alt[None, :], -sinm)
    fwd = jnp.concatenate([cosm, im_rows], axis=0)
    wre = jnp.where((r == 0)[None, :], 0.5, 1.0) / p
    inv_re = cosm * wre
    inv_im = jnp.where((r == 0)[None, :], alt[:, None] * (0.5 / p), -sinm / p)
    inv = jnp.concatenate([inv_re, inv_im], axis=1)
    return fwd.astype(BF16), inv.astype(BF16)


def kernel(x_prompt, x_sample, c_prompt, c_sample, ada_w, ada_b, norm1_g, w_in, gm_vnorm_g, gm_spatial_w, gm_spatial_b, hy_conv_w, hy_conv_b, hy_w1, hy_b1, hy_f1, hy_w2, hy_b2, hy_f2, hy_w3, hy_decay, hy_skip, gqa_qn_g, gqa_kn_g, mla_q_a_g, mla_w_q_b, mla_kv_a_g, mla_w_kv_b, mla_qn_g, mla_kn_g, group_norm_g, w_out, norm2_g, ffn_w_up, ffn_conv_w, ffn_conv_b, ffn_w_down):
    nl = DEPTH
    w_in_b = jnp.pad(w_in, ((0, 0), (0, 0), (0, IN_PAD - IN_COLS))).astype(BF16)
    w_out_b = w_out.astype(BF16)
    w_up_b = ffn_w_up.astype(BF16)
    w_down_b = ffn_w_down.astype(BF16)
    ws_b = gm_spatial_w.astype(BF16)
    gm_bias = jnp.broadcast_to(jnp.swapaxes(gm_spatial_b, 1, 2)[:, :, :, None],
                               (nl, CHUNK, A_HEADS, LANES)).reshape(nl, CHUNK, W_GROUP)
    row3 = lambda a: a.reshape(nl, 1, a.shape[-1])
    norm1 = row3(norm1_g)
    norm2 = row3(norm2_g)
    gm_g = row3(gm_vnorm_g)
    gg = row3(group_norm_g)
    hy_cw = hy_conv_w.reshape(nl, 3, 3, W_GROUP).transpose(0, 2, 1, 3)
    hy_cb = hy_conv_b.reshape(nl, 3, W_GROUP)
    padl = lambda a, rows, cols: jnp.pad(a, ((0, 0), (0, rows - a.shape[1]), (0, cols - a.shape[2])))
    hy_w1p = padl(hy_w1, LANES, LANES)
    hy_w2p = padl(hy_w2, LANES, LANES)
    hy_b1p = padl(row3(hy_b1), 1, LANES)
    hy_f1p = padl(row3(hy_f1), 1, LANES)
    hy_b2p = padl(row3(hy_b2), 1, LANES)
    hy_f2p = padl(row3(hy_f2), 1, LANES)
    hy_w3d = jnp.pad(hy_w3.reshape(nl, FILT_H, 2, 2, W_GROUP).transpose(0, 3, 1, 2, 4)
                     .reshape(nl, 2, FILT_H, 2 * W_GROUP), ((0, 0), (0, 0), (0, LANES - FILT_H), (0, 0)))
    hy_decd = hy_decay.reshape(nl, 2, 2, W_GROUP).transpose(0, 2, 1, 3).reshape(nl, 2, 1, 2 * W_GROUP)
    gqa_qg = row3(gqa_qn_g)
    gqa_kg = row3(gqa_kn_g)
    mla_qag = row3(mla_q_a_g)
    mla_kvg = row3(mla_kv_a_g)
    slot_pad = MLA_SLOT - NOPE - ROPE_D
    wq_b = jnp.pad(mla_w_q_b.reshape(nl, Q_LORA, D_HEADS, NOPE + ROPE_D),
                   ((0, 0), (0, 0), (0, 0), (0, slot_pad))).reshape(nl, Q_LORA, D_HEADS * MLA_SLOT).astype(BF16)
    wkv_b = mla_w_kv_b.astype(BF16)
    mla_qn = jnp.pad(row3(mla_qn_g), ((0, 0), (0, 0), (0, slot_pad)))
    mla_kn = jnp.pad(row3(mla_kn_g), ((0, 0), (0, 0), (0, slot_pad)))
    ffn_cb = row3(ffn_conv_b)

    nbp = x_prompt.shape[0]
    c_all = jnp.concatenate([c_prompt, c_sample], axis=0)
    mod = _ada_mod(c_all, ada_w, ada_b).reshape(nl, c_all.shape[0], 6, D_MODEL)

    fwd_b, inv_b = _dft_mats(HY_P)

    def trunk(x3, b0):
        bsz, seq, _ = x3.shape
        x = x3.reshape(bsz * seq, D_MODEL)
        gcos, gsin = _axial_tables(seq, C_HD // 2, C_HD)
        mcos, msin = _axial_tables(seq, ROPE_D // 2, ROPE_D)
        feats = _hyena_feats(seq)
        sub = ATTN_SCORE_ELEMS // seq
        for l in range(nl):
            mod_l = mod[l]
            proj = _in_proj(x, mod_l, b0, seq, norm1, w_in_b, l)
            ya = _gmlp(proj, gm_g, ws_b, gm_bias, l)
            kseq = _hy_filters(feats, hy_w1p, hy_b1p, hy_f1p, hy_w2p, hy_b2p, hy_f2p, hy_w3d, hy_decd, l, seq)
            hfilt = _hy_filter_dft(kseq, fwd_b, seq)
            yb = _hyena(proj, hy_cw, hy_cb, hy_skip, hfilt, fwd_b, inv_b, l, seq)
            gq, gk = _gqa_prep(proj, gcos, gsin, gqa_qg, gqa_kg, l, seq)
            yc = _attention(gq, gk, proj, GQA_V_COL // C_HD, seq, C_KV_HEADS, C_HEADS // C_KV_HEADS, C_HD, C_HD,
                            2 * sub, sub, "gqa_attn")
            mq, mk, mv = _mla_prep(proj, mcos, msin, mla_qag, mla_kvg, wq_b, wkv_b, mla_qn, mla_kn, l, seq)
            yd = _attention(mq, mk, mv, 0, seq, D_HEADS, 1, MLA_SLOT, V_HD, 4 * sub, sub, "mla_attn")
            x = _out_proj(ya, yb, yc, yd, x, mod_l, b0, seq, gg, w_out_b, l)
            x = _ffn(x, mod_l, b0, seq, norm2, w_up_b, ffn_conv_w, ffn_cb, w_down_b, l)
        return x.reshape(bsz, seq, D_MODEL)

    return trunk(x_prompt, 0), trunk(x_sample, nbp)
```

```python
import functools
import math

import jax
import jax.numpy as jnp
from jax import lax
from jax.experimental import pallas as pl
from jax.experimental.pallas import tpu as pltpu

F32 = jnp.float32
BF16 = jnp.bfloat16

D_MODEL = 2048
DEPTH = 4
GRID_W = 64
CHUNK = 128
W_GROUP = 512
A_HEADS = 4
POS_BANDS = 16
POS_EMB = 2 * POS_BANDS + 1
FILT_H = 64
C_HEADS = 4
C_KV_HEADS = 2
C_HD = 128
D_HEADS = 4
Q_LORA = 512
KV_LORA = 256
NOPE = 128
ROPE_D = 64
V_HD = 128
ROPE_THETA = 10000.0
D_FF = 5632
EPS = 1e-6
LOG2E = 1.4426950408889634
IN_A = 1024
IN_B = 1536
IN_C = 1024
IN_D = 832
IN_COLS = IN_A + IN_B + IN_C + IN_D
GQA_V_COL = IN_A + IN_B + (C_HEADS + C_KV_HEADS) * C_HD
IN_PAD = 4608

LANES = 128
SUBLANES = 8
BF16_ROWS = 16
VMEM_LIMIT = 56 * 1024 * 1024
FFN_VMEM_LIMIT = 60 * 1024 * 1024

ATTN_SCORE_ELEMS = 256 * 4096
HY_P = 512
FFN_HALO = BF16_ROWS
MLA_SLOT = 256


def _params(sem, vmem=VMEM_LIMIT):
    return pltpu.CompilerParams(dimension_semantics=sem, vmem_limit_bytes=vmem)


def _single(block_shape, index_map):
    return pl.BlockSpec(block_shape, index_map, pipeline_mode=pl.Buffered(1))


def _ada_kernel(c_ref, w_ref, b_ref, o_ref):
    s = jax.nn.silu(c_ref[...]).astype(BF16)
    o_ref[0] = jnp.dot(s, w_ref[0].astype(BF16), preferred_element_type=F32) + b_ref[0]


def _ada_mod(c_all, ada_w, ada_b):
    nb = c_all.shape[0]
    tn = 1024
    return pl.pallas_call(
        _ada_kernel,
        grid=(DEPTH, 6 * D_MODEL // tn),
        in_specs=[
            pl.BlockSpec((nb, D_MODEL), lambda l, j: (0, 0)),
            pl.BlockSpec((1, D_MODEL, tn), lambda l, j: (l, 0, j)),
            pl.BlockSpec((1, 1, tn), lambda l, j: (l, 0, j)),
        ],
        out_specs=pl.BlockSpec((1, nb, tn), lambda l, j: (l, 0, j)),
        out_shape=jax.ShapeDtypeStruct((DEPTH, nb, 6 * D_MODEL), F32),
        compiler_params=_params(("parallel", "parallel")),
        name="ada_mod",
    )(c_all, ada_w, ada_b.reshape(DEPTH, 1, 6 * D_MODEL))


def _mod_norm_rows(x, g, shift, scale):
    ms = jnp.mean(x * x, axis=-1, keepdims=True)
    return (x * lax.rsqrt(ms + EPS) * g) * (1.0 + scale) + shift


def _inproj_kernel(xn_ref, x0_ref, modn_ref, mod0_ref, g_ref, w_ref, o_ref, ha_ref, hb_ref, *, tm):
    i = pl.program_id(0)
    rc = 128
    g = g_ref[...]

    def fill(x_ref, mod_ref, dst_ref):
        shift = mod_ref[0, 0:1, :]
        scale = mod_ref[0, 1:2, :]
        for c in range(tm // rc):
            rows = slice(c * rc, (c + 1) * rc)
            dst_ref[rows, :] = _mod_norm_rows(x_ref[rows, :], g, shift, scale).astype(BF16)

    @pl.when(i == 0)
    def _():
        fill(x0_ref, mod0_ref, ha_ref)

    def step(cur_ref, nxt_ref):
        fill(xn_ref, modn_ref, nxt_ref)
        o_ref[...] = jnp.dot(cur_ref[...], w_ref[...], preferred_element_type=F32).astype(o_ref.dtype)

    @pl.when(i % 2 == 0)
    def _():
        step(ha_ref, hb_ref)

    @pl.when(i % 2 == 1)
    def _():
        step(hb_ref, ha_ref)


def _in_proj(x, mod_l, b0, seq, g, w_in_b, layer):
    t = x.shape[0]
    tm = 512
    n = t // tm
    nxt = lambda i: jnp.minimum(i + 1, n - 1)
    return pl.pallas_call(
        functools.partial(_inproj_kernel, tm=tm),
        grid=(n,),
        in_specs=[
            pl.BlockSpec((tm, D_MODEL), lambda i: (nxt(i), 0)),
            _single((tm, D_MODEL), lambda i: (0, 0)),
            pl.BlockSpec((1, 6, D_MODEL), lambda i: (b0 + (nxt(i) * tm) // seq, 0, 0)),
            pl.BlockSpec((1, 6, D_MODEL), lambda i: (b0, 0, 0)),
            pl.BlockSpec((None, 1, D_MODEL), lambda i: (layer, 0, 0)),
            _single((None, D_MODEL, IN_PAD), lambda i: (layer, 0, 0)),
        ],
        out_specs=pl.BlockSpec((tm, IN_PAD), lambda i: (i, 0)),
        out_shape=jax.ShapeDtypeStruct((t, IN_PAD), BF16),
        scratch_shapes=[pltpu.VMEM((tm, D_MODEL), BF16), pltpu.VMEM((tm, D_MODEL), BF16)],
        compiler_params=_params(("arbitrary",)),
        name="in_proj",
    )(x, x, mod_l, mod_l, g, w_in_b)


def _gelu(x):
    return 0.5 * x * (1.0 + lax.erf(x * (1.0 / math.sqrt(2.0))))


def _gmlp_kernel(u_ref, v_ref, g_ref, ws_ref, bias_ref, o_ref, *, tm):
    g = g_ref[...]
    for n in range(tm // CHUNK):
        rows = slice(n * CHUNK, (n + 1) * CHUNK)
        u = _gelu(u_ref[rows, :].astype(F32))
        v = _gelu(v_ref[rows, :].astype(F32))
        vc = v - jnp.mean(v, axis=-1, keepdims=True)
        vn = (vc * lax.rsqrt(jnp.mean(vc * vc, axis=-1, keepdims=True) + EPS) * g).astype(BF16)
        for h in range(A_HEADS):
            cols = slice(h * LANES, (h + 1) * LANES)
            mixed = jnp.dot(ws_ref[h], vn[:, cols], preferred_element_type=F32) + bias_ref[:, cols]
            o_ref[rows, cols] = (u[:, cols] * mixed).astype(o_ref.dtype)


def _gmlp(proj, g, ws_b, bias_full, layer):
    t = proj.shape[0]
    tm = 512
    return pl.pallas_call(
        functools.partial(_gmlp_kernel, tm=tm),
        grid=(t // tm,),
        in_specs=[
            pl.BlockSpec((tm, W_GROUP), lambda i: (i, 0)),
            pl.BlockSpec((tm, W_GROUP), lambda i: (i, 1)),
            pl.BlockSpec((None, 1, W_GROUP), lambda i: (layer, 0, 0)),
            pl.BlockSpec((None, A_HEADS, CHUNK, CHUNK), lambda i: (layer, 0, 0, 0)),
            pl.BlockSpec((None, CHUNK, W_GROUP), lambda i: (layer, 0, 0)),
        ],
        out_specs=pl.BlockSpec((tm, W_GROUP), lambda i: (i, 0)),
        out_shape=jax.ShapeDtypeStruct((t, W_GROUP), BF16),
        compiler_params=_params(("parallel",)),
        name="gmlp",
    )(proj, proj, g, ws_b, bias_full)


def _hy_filter_kernel(f_ref, w1_ref, b1_ref, f1_ref, w2_ref, b2_ref, f2_ref, w3_ref, dec_ref, o_ref):
    hp = lax.Precision.HIGHEST
    feats = f_ref[...]
    h = jnp.sin(f1_ref[...] * (jnp.dot(feats, w1_ref[...], precision=hp, preferred_element_type=F32) + b1_ref[...]))
    h = jnp.sin(f2_ref[...] * (jnp.dot(h, w2_ref[...], precision=hp, preferred_element_type=F32) + b2_ref[...]))
    h = jnp.dot(h, w3_ref[0], precision=hp, preferred_element_type=F32)
    tcol = feats[:, 0:1]
    valid = feats[:, LANES - 1:LANES]
    k = h * jnp.exp(-tcol * jnp.abs(dec_ref[0])) * valid
    o_ref[0] = k[:, :W_GROUP]
    o_ref[1] = k[:, W_GROUP:]


def _hy_filters(feats, w1p, b1p, f1p, w2p, b2p, f2p, w3d, decd, layer, seq):
    rt = 512
    half = seq // rt
    c2 = lambda i: (layer, 0, 0)
    return pl.pallas_call(
        _hy_filter_kernel,
        grid=(2 * seq // rt,),
        in_specs=[
            pl.BlockSpec((rt, LANES), lambda i: (i, 0)),
            pl.BlockSpec((None, LANES, LANES), c2),
            pl.BlockSpec((None, 1, LANES), c2),
            pl.BlockSpec((None, 1, LANES), c2),
            pl.BlockSpec((None, LANES, LANES), c2),
            pl.BlockSpec((None, 1, LANES), c2),
            pl.BlockSpec((None, 1, LANES), c2),
            pl.BlockSpec((None, 1, LANES, 2 * W_GROUP), lambda i: (layer, jnp.where(i < half, 1, 0), 0, 0)),
            pl.BlockSpec((None, 1, 1, 2 * W_GROUP), lambda i: (layer, jnp.where(i < half, 1, 0), 0, 0)),
        ],
        out_specs=pl.BlockSpec((2, rt, W_GROUP), lambda i: (0, i, 0)),
        out_shape=jax.ShapeDtypeStruct((2, 2 * seq, W_GROUP), F32),
        compiler_params=_params(("parallel",)),
        name="hy_filter",
    )(feats, w1p, b1p, f1p, w2p, b2p, f2p, w3d, decd)


def _hy_fdft_kernel(k_ref, fwd_ref, o_ref, prev_ref):
    e = pl.program_id(1)
    cur = jnp.dot(fwd_ref[...], k_ref[0].astype(BF16), preferred_element_type=F32)

    @pl.when(e > 0)
    def _():
        prev = prev_ref[...]
        odd = (lax.broadcasted_iota(jnp.int32, prev.shape, 0) & 1) == 1
        o_ref[0, 0] = cur + jnp.where(odd, -prev, prev)

    prev_ref[...] = cur


def _hy_filter_dft(kseq, fwd_b, seq):
    p = HY_P
    nb = seq // p
    return pl.pallas_call(
        _hy_fdft_kernel,
        grid=(2, 2 * nb),
        in_specs=[
            pl.BlockSpec((1, p, W_GROUP), lambda o, e: (o, e, 0)),
            pl.BlockSpec((2 * p, p), lambda o, e: (0, 0)),
        ],
        out_specs=pl.BlockSpec((1, 1, 2 * p, W_GROUP), lambda o, e: (o, jnp.maximum(e - 1, 0), 0, 0)),
        out_shape=jax.ShapeDtypeStruct((2, 2 * nb - 1, 2 * p, W_GROUP), F32),
        scratch_shapes=[pltpu.VMEM((2 * p, W_GROUP), F32)],
        compiler_params=_params(("arbitrary", "arbitrary")),
        name="hy_filter_dft",
    )(kseq, fwd_b)


def _dwconv_rows(src_ref, r0, rc, total, w, b):
    halo = BF16_ROWS
    xc = src_ref[pl.ds(r0, rc), :].astype(F32)
    prev = src_ref[pl.ds(pl.multiple_of(jnp.maximum(r0 - halo, 0), halo), halo), :].astype(F32)
    nxt = src_ref[pl.ds(pl.multiple_of(jnp.minimum(r0 + rc, total - halo), halo), halo), :].astype(F32)
    prev = jnp.where(r0 > 0, prev, 0.0)
    nxt = jnp.where(r0 + rc < total, nxt, 0.0)
    cat = jnp.concatenate([prev, xc, nxt], axis=0)
    n = rc + 2 * halo
    dn = pltpu.roll(cat, 1, 0)[halo:halo + rc]
    up = pltpu.roll(cat, n - 1, 0)[halo:halo + rc]
    return dn * w[0:1] + xc * w[1:2] + up * w[2:3] + b


def _hyena_kernel(p1_ref, p2_ref, pv_ref, cw_ref, cb_ref, skip_ref, h_ref, fwd_ref, inv_ref, o_ref,
                  vz, x1z, x2z, vhat0, vhat1, yhat_a, yhat_b, *, seq, p):
    nb = seq // p
    rc = 256
    rm = 32
    yhat = (yhat_a, yhat_b)

    def conv_body(c, carry):
        r0 = pl.multiple_of(c * rc, rc)
        rows = pl.ds(r0, rc)
        x1z[rows, :] = _dwconv_rows(p1_ref, r0, rc, seq, cw_ref[0], cb_ref[0:1, :])
        x2z[rows, :] = _dwconv_rows(p2_ref, r0, rc, seq, cw_ref[1], cb_ref[1:2, :])
        vz[rows, :] = _dwconv_rows(pv_ref, r0, rc, seq, cw_ref[2], cb_ref[2:3, :])
        return carry

    lax.fori_loop(0, seq // rc, conv_body, 0)

    def fwd_body(i, carry):
        rows = pl.ds(pl.multiple_of(i * p, p), p)
        vhat0[i] = jnp.dot(fwd_ref[...], vz[rows, :].astype(BF16), preferred_element_type=F32)
        return carry

    lax.fori_loop(0, nb, fwd_body, 0)

    def block_spectrum(order, i, vhat, dst):
        for r in range(p // rm):
            rows_re = slice(r * rm, (r + 1) * rm)
            rows_im = slice(p + r * rm, p + (r + 1) * rm)
            acc_r = None
            acc_i = None
            for ip in range(nb):
                d = i - ip + (nb - 1)
                hr = h_ref[order, d, rows_re, :]
                hi = h_ref[order, d, rows_im, :]
                ur = vhat[ip, rows_re, :]
                ui = vhat[ip, rows_im, :]
                if r == 0:
                    first = lax.broadcasted_iota(jnp.int32, hr.shape, 0) == 0
                    tr = hr * ur - jnp.where(first, 0.0, hi * ui)
                    ti = jnp.where(first, hi * ui, hr * ui + hi * ur)
                else:
                    tr = hr * ur - hi * ui
                    ti = hr * ui + hi * ur
                acc_r = tr if acc_r is None else acc_r + tr
                acc_i = ti if acc_i is None else acc_i + ti
            dst[rows_re, :] = acc_r.astype(BF16)
            dst[rows_im, :] = acc_i.astype(BF16)

    for order in range(2):
        gate = x1z if order == 0 else x2z
        vhat = vhat0 if order == 0 else vhat1
        skip = skip_ref[order:order + 1, :]
        for i in range(nb + 1):
            if i >= 1:
                y = jnp.dot(inv_ref[...], yhat[(i - 1) % 2][...], preferred_element_type=F32)
            if i < nb:
                block_spectrum(order, i, vhat, yhat[i % 2])
            if i >= 1:
                rows = slice((i - 1) * p, i * p)
                res = gate[rows, :] * (y + skip * vz[rows, :])
                if order == 0:
                    vz[rows, :] = res
                    vhat1[i - 1] = jnp.dot(fwd_ref[...], res.astype(BF16), preferred_element_type=F32)
                else:
                    o_ref[rows, :] = res.astype(o_ref.dtype)


def _hyena(proj, cw, cb, skip, hfilt, fwd_b, inv_b, layer, seq):
    t = proj.shape[0]
    p = HY_P
    nb = seq // p
    nct = W_GROUP // LANES
    base = IN_A // LANES
    return pl.pallas_call(
        functools.partial(_hyena_kernel, seq=seq, p=p),
        grid=(nct, t // seq),
        in_specs=[
            pl.BlockSpec((seq, LANES), lambda c, b: (b, base + c)),
            pl.BlockSpec((seq, LANES), lambda c, b: (b, base + nct + c)),
            pl.BlockSpec((seq, LANES), lambda c, b: (b, base + 2 * nct + c)),
            pl.BlockSpec((None, 3, 3, LANES), lambda c, b: (layer, 0, 0, c)),
            pl.BlockSpec((None, 3, LANES), lambda c, b: (layer, 0, c)),
            pl.BlockSpec((None, 2, LANES), lambda c, b: (layer, 0, c)),
            _single((2, 2 * nb - 1, 2 * p, LANES), lambda c, b: (0, 0, 0, c)),
            _single((2 * p, p), lambda c, b: (0, 0)),
            _single((p, 2 * p), lambda c, b: (0, 0)),
        ],
        out_specs=pl.BlockSpec((seq, LANES), lambda c, b: (b, c)),
        out_shape=jax.ShapeDtypeStruct((t, W_GROUP), BF16),
        scratch_shapes=[
            pltpu.VMEM((seq, LANES), F32),
            pltpu.VMEM((seq, LANES), F32),
            pltpu.VMEM((seq, LANES), F32),
            pltpu.VMEM((nb, 2 * p, LANES), F32),
            pltpu.VMEM((nb, 2 * p, LANES), F32),
            pltpu.VMEM((2 * p, LANES), BF16),
            pltpu.VMEM((2 * p, LANES), BF16),
        ],
        compiler_params=_params(("parallel", "parallel")),
        name="hyena",
    )(proj, proj, proj, cw, cb, skip, hfilt, fwd_b, inv_b)


def _rope_lanes(x, cos_t, sin_t, half):
    lane = lax.broadcasted_iota(jnp.int32, x.shape, 1)
    lower = (lane % (2 * half)) < half
    partner = jnp.where(lower, pltpu.roll(x, LANES - half, 1), pltpu.roll(x, half, 1))
    return x * cos_t + partner * sin_t


def _rms_rows(x, g, width):
    ss = jnp.sum(x * x, axis=-1, keepdims=True) * (1.0 / width)
    return x * lax.rsqrt(ss + EPS) * g


def _gqa_prep_kernel(q_ref, k_ref, cos_ref, sin_ref, qg_ref, kg_ref, qo_ref, ko_ref):
    cos_t = cos_ref[...]
    sin_t = sin_ref[...]
    scale = LOG2E / math.sqrt(C_HD)
    for h in range(C_HEADS):
        cols = slice(h * C_HD, (h + 1) * C_HD)
        q = _rope_lanes(_rms_rows(q_ref[:, cols].astype(F32), qg_ref[...], C_HD), cos_t, sin_t, C_HD // 4)
        qo_ref[:, cols] = (q * scale).astype(BF16)
    for h in range(C_KV_HEADS):
        cols = slice(h * C_HD, (h + 1) * C_HD)
        k = _rope_lanes(_rms_rows(k_ref[:, cols].astype(F32), kg_ref[...], C_HD), cos_t, sin_t, C_HD // 4)
        ko_ref[:, cols] = k.astype(BF16)


def _gqa_prep(proj, cos_t, sin_t, qg, kg, layer, seq):
    t = proj.shape[0]
    tm = 512
    qw = C_HEADS * C_HD
    kw = C_KV_HEADS * C_HD
    c0 = IN_A + IN_B
    g2 = lambda i: (layer, 0, 0)
    return pl.pallas_call(
        _gqa_prep_kernel,
        grid=(t // tm,),
        in_specs=[
            pl.BlockSpec((tm, qw), lambda i: (i, c0 // qw)),
            pl.BlockSpec((tm, kw), lambda i: (i, (c0 + qw) // kw)),
            pl.BlockSpec((tm, LANES), lambda i: (i % (seq // tm), 0)),
            pl.BlockSpec((tm, LANES), lambda i: (i % (seq // tm), 0)),
            pl.BlockSpec((None, 1, C_HD), g2),
            pl.BlockSpec((None, 1, C_HD), g2),
        ],
        out_specs=[
            pl.BlockSpec((tm, qw), lambda i: (i, 0)),
            pl.BlockSpec((tm, kw), lambda i: (i, 0)),
        ],
        out_shape=[
            jax.ShapeDtypeStruct((t, qw), BF16),
            jax.ShapeDtypeStruct((t, kw), BF16),
        ],
        compiler_params=_params(("parallel",)),
        name="gqa_prep",
    )(proj, proj, cos_t, sin_t, qg, kg)


def _mla_prep_kernel(qa_ref, ckv_ref, kr_ref, cos_ref, sin_ref, qag_ref, kvg_ref, wq_ref, wkv_ref,
                     qn_ref, kn_ref, qo_ref, ko_ref, vo_ref):
    cos_t = cos_ref[...]
    sin_t = sin_ref[...]
    width = NOPE + ROPE_D
    scale = LOG2E / math.sqrt(width)
    qa = _rms_rows(qa_ref[...].astype(F32), qag_ref[...], Q_LORA).astype(BF16)
    q = jnp.dot(qa, wq_ref[...], preferred_element_type=F32)
    ckv = _rms_rows(ckv_ref[...].astype(F32), kvg_ref[...], KV_LORA).astype(BF16)
    kv = jnp.dot(ckv, wkv_ref[...], preferred_element_type=F32)
    kr = kr_ref[...].astype(F32)
    kr_ss = jnp.sum(kr * kr, axis=-1, keepdims=True)
    qn = qn_ref[...]
    kn = kn_ref[...]
    for h in range(D_HEADS):
        base = h * MLA_SLOT
        qh = _rms_rows(q[:, base:base + MLA_SLOT], qn, width)
        qo_ref[:, base:base + NOPE] = (qh[:, :NOPE] * scale).astype(BF16)
        qr = _rope_lanes(qh[:, NOPE:], cos_t, sin_t, ROPE_D // 4)
        qo_ref[:, base + NOPE:base + MLA_SLOT] = (qr * scale).astype(BF16)
        k_nope = kv[:, base:base + NOPE]
        ss = (jnp.sum(k_nope * k_nope, axis=-1, keepdims=True) + kr_ss) * (1.0 / width)
        inv = lax.rsqrt(ss + EPS)
        ko_ref[:, base:base + NOPE] = (k_nope * inv * kn[:, :NOPE]).astype(BF16)
        krn = _rope_lanes(kr * inv * kn[:, NOPE:], cos_t, sin_t, ROPE_D // 4)
        ko_ref[:, base + NOPE:base + MLA_SLOT] = krn.astype(BF16)
        vo_ref[:, h * V_HD:(h + 1) * V_HD] = kv[:, base + NOPE:base + NOPE + V_HD].astype(BF16)


def _mla_prep(proj, cos_t, sin_t, qag, kvg, wq_b, wkv_b, qn_p, kn_p, layer, seq):
    t = proj.shape[0]
    tm = 512
    c0 = IN_A + IN_B + IN_C
    g2 = lambda i: (layer, 0, 0)
    hw = D_HEADS * MLA_SLOT
    return pl.pallas_call(
        _mla_prep_kernel,
        grid=(t // tm,),
        in_specs=[
            pl.BlockSpec((tm, Q_LORA), lambda i: (i, c0 // Q_LORA)),
            pl.BlockSpec((tm, KV_LORA), lambda i: (i, (c0 + Q_LORA) // KV_LORA)),
            pl.BlockSpec((tm, LANES), lambda i: (i, (c0 + Q_LORA + KV_LORA) // LANES)),
            pl.BlockSpec((tm, LANES), lambda i: (i % (seq // tm), 0)),
            pl.BlockSpec((tm, LANES), lambda i: (i % (seq // tm), 0)),
            pl.BlockSpec((None, 1, Q_LORA), g2),
            pl.BlockSpec((None, 1, KV_LORA), g2),
            pl.BlockSpec((None, Q_LORA, hw), g2),
            pl.BlockSpec((None, KV_LORA, hw), g2),
            pl.BlockSpec((None, 1, MLA_SLOT), g2),
            pl.BlockSpec((None, 1, MLA_SLOT), g2),
        ],
        out_specs=[
            pl.BlockSpec((tm, hw), lambda i: (i, 0)),
            pl.BlockSpec((tm, hw), lambda i: (i, 0)),
            pl.BlockSpec((tm, D_HEADS * V_HD), lambda i: (i, 0)),
        ],
        out_shape=[
            jax.ShapeDtypeStruct((t, hw), BF16),
            jax.ShapeDtypeStruct((t, hw), BF16),
            jax.ShapeDtypeStruct((t, D_HEADS * V_HD), BF16),
        ],
        compiler_params=_params(("parallel",)),
        name="mla_prep",
    )(proj, proj, proj, cos_t, sin_t, qag, kvg, wq_b, wkv_b, qn_p, kn_p)


def _attn_kernel(q_ref, k_ref, v_ref, o_ref, vext_ref, *, groups, dk, dv, sub):
    @pl.when(pl.program_id(2) == 0)
    def _():
        vext_ref[:, :dv] = v_ref[...]
        vext_ref[:, dv:] = jnp.ones((v_ref.shape[0], dv), BF16)

    k = k_ref[...]
    v = vext_ref[...]
    tq = q_ref.shape[0]
    chains = [(g, slice(r * sub, (r + 1) * sub)) for g in range(groups) for r in range(tq // sub)]

    def scores(c):
        g, rows = chains[c]
        q = q_ref[rows, g * dk:(g + 1) * dk]
        return lax.dot_general(q, k, (((1,), (1,)), ((), ())), preferred_element_type=F32)

    s = scores(0)
    for c, (g, rows) in enumerate(chains):
        s_next = scores(c + 1) if c + 1 < len(chains) else None
        m = jnp.max(s, axis=-1, keepdims=True)
        pexp = jnp.exp2(s - m).astype(BF16)
        o = jnp.dot(pexp, v, preferred_element_type=F32)
        o_ref[rows, g * dv:(g + 1) * dv] = (o[:, :dv] / o[:, dv:]).astype(o_ref.dtype)
        s = s_next


def _attention(q, k, v, v_col0, seq, kv_heads, groups, dk, dv, tq, sub, name):
    t = q.shape[0]
    nq = seq // tq
    return pl.pallas_call(
        functools.partial(_attn_kernel, groups=groups, dk=dk, dv=dv, sub=sub),
        grid=(t // seq, kv_heads, nq),
        in_specs=[
            pl.BlockSpec((tq, groups * dk), lambda b, h, i: (b * nq + i, h)),
            pl.BlockSpec((seq, dk), lambda b, h, i: (b, h)),
            pl.BlockSpec((seq, dv), lambda b, h, i: (b, v_col0 + h)),
        ],
        out_specs=pl.BlockSpec((tq, groups * dv), lambda b, h, i: (b * nq + i, h)),
        out_shape=jax.ShapeDtypeStruct((t, kv_heads * groups * dv), BF16),
        scratch_shapes=[pltpu.VMEM((seq, 2 * dv), BF16)],
        compiler_params=_params(("parallel", "parallel", "arbitrary")),
        name=name,
    )(q, k, v)


def _outproj_kernel(ya_ref, yb_ref, yc_ref, yd_ref, x_ref, mod_ref, gg_ref, w_ref, o_ref):
    acc = None
    for gi, y_ref in enumerate((ya_ref, yb_ref, yc_ref, yd_ref)):
        rows = slice(gi * W_GROUP, (gi + 1) * W_GROUP)
        yn = _rms_rows(y_ref[...].astype(F32), gg_ref[:, rows], W_GROUP).astype(BF16)
        part = jnp.dot(yn, w_ref[rows, :], preferred_element_type=F32)
        acc = part if acc is None else acc + part
    o_ref[...] = x_ref[...] + mod_ref[0, 2:3, :] * acc


def _out_proj(ya, yb, yc, yd, x, mod_l, b0, seq, gg, w_out_b, layer):
    t = x.shape[0]
    tm = 512
    yspec = pl.BlockSpec((tm, W_GROUP), lambda i: (i, 0))
    return pl.pallas_call(
        _outproj_kernel,
        grid=(t // tm,),
        in_specs=[
            yspec, yspec, yspec, yspec,
            pl.BlockSpec((tm, D_MODEL), lambda i: (i, 0)),
            pl.BlockSpec((1, 6, D_MODEL), lambda i: (b0 + (i * tm) // seq, 0, 0)),
            pl.BlockSpec((None, 1, D_MODEL), lambda i: (layer, 0, 0)),
            _single((None, D_MODEL, D_MODEL), lambda i: (layer, 0, 0)),
        ],
        out_specs=pl.BlockSpec((tm, D_MODEL), lambda i: (i, 0)),
        out_shape=jax.ShapeDtypeStruct((t, D_MODEL), F32),
        compiler_params=_params(("parallel",)),
        name="out_proj",
    )(ya, yb, yc, yd, x, mod_l, gg, w_out_b)


def _ffn_kernel(x_ref, xp_ref, xn_ref, mod_ref, g_ref, wg_ref, wu_ref, cwg_ref, cwu_ref, cbg_ref, cbu_ref,
                wd_ref, o_ref, h_ref, *, tm, seq):
    i = pl.program_id(0)
    j = pl.program_id(1)
    nj = pl.num_programs(1)
    rc = 128
    halo = FFN_HALO

    @pl.when(j == 0)
    def _():
        g = g_ref[...]
        shift = mod_ref[0, 3:4, :]
        scale = mod_ref[0, 4:5, :]
        has_prev = ((i * tm) % seq) != 0
        has_next = (((i + 1) * tm) % seq) != 0
        hp = _mod_norm_rows(xp_ref[...], g, shift, scale)
        hn = _mod_norm_rows(xn_ref[...], g, shift, scale)
        h_ref[pl.ds(0, halo), :] = jnp.where(has_prev, hp, 0.0).astype(BF16)
        h_ref[pl.ds(halo + tm, halo), :] = jnp.where(has_next, hn, 0.0).astype(BF16)

        def body(c, carry):
            r0 = pl.multiple_of(c * rc, rc)
            h_ref[pl.ds(halo + r0, rc), :] = _mod_norm_rows(x_ref[pl.ds(r0, rc), :], g, shift, scale).astype(BF16)
            return carry

        lax.fori_loop(0, tm // rc, body, 0)
        o_ref[...] = jnp.zeros(o_ref.shape, F32)

    hfull = h_ref[...]
    n = tm + 2 * halo

    def conv(u, cw_ref, cb_ref):
        dn = pltpu.roll(u, 1, 0)[halo:halo + tm]
        up = pltpu.roll(u, n - 1, 0)[halo:halo + tm]
        return dn * cw_ref[0:1, :] + u[halo:halo + tm] * cw_ref[1:2, :] + up * cw_ref[2:3, :] + cb_ref[...]

    gate = conv(jnp.dot(hfull, wg_ref[...], preferred_element_type=F32), cwg_ref, cbg_ref)
    upv = conv(jnp.dot(hfull, wu_ref[...], preferred_element_type=F32), cwu_ref, cbu_ref)
    act = (jax.nn.silu(gate) * upv).astype(BF16)
    o_ref[...] += jnp.dot(act, wd_ref[...], preferred_element_type=F32)

    @pl.when(j == nj - 1)
    def _():
        o_ref[...] = x_ref[...] + mod_ref[0, 5:6, :] * o_ref[...]


def _ffn(x, mod_l, b0, seq, g, w_up_b, cw, cb, w_down_b, layer):
    t = x.shape[0]
    tm, tf = 1024, 512
    nf = D_FF // tf
    hb = tm // FFN_HALO
    last = t // FFN_HALO - 1
    return pl.pallas_call(
        functools.partial(_ffn_kernel, tm=tm, seq=seq),
        grid=(t // tm, nf),
        in_specs=[
            pl.BlockSpec((tm, D_MODEL), lambda i, j: (i, 0)),
            pl.BlockSpec((FFN_HALO, D_MODEL), lambda i, j: (jnp.maximum(i * hb - 1, 0), 0)),
            pl.BlockSpec((FFN_HALO, D_MODEL), lambda i, j: (jnp.minimum((i + 1) * hb, last), 0)),
            pl.BlockSpec((1, 6, D_MODEL), lambda i, j: (b0 + (i * tm) // seq, 0, 0)),
            pl.BlockSpec((None, 1, D_MODEL), lambda i, j: (layer, 0, 0)),
            pl.BlockSpec((None, D_MODEL, tf), lambda i, j: (layer, 0, j)),
            pl.BlockSpec((None, D_MODEL, tf), lambda i, j: (layer, 0, nf + j)),
            pl.BlockSpec((None, 3, tf), lambda i, j: (layer, 0, j)),
            pl.BlockSpec((None, 3, tf), lambda i, j: (layer, 0, nf + j)),
            pl.BlockSpec((None, 1, tf), lambda i, j: (layer, 0, j)),
            pl.BlockSpec((None, 1, tf), lambda i, j: (layer, 0, nf + j)),
            pl.BlockSpec((None, tf, D_MODEL), lambda i, j: (layer, j, 0)),
        ],
        out_specs=pl.BlockSpec((tm, D_MODEL), lambda i, j: (i, 0)),
        out_shape=jax.ShapeDtypeStruct((t, D_MODEL), F32),
        scratch_shapes=[pltpu.VMEM((tm + 2 * FFN_HALO, D_MODEL), BF16)],
        compiler_params=_params(("parallel", "arbitrary"), FFN_VMEM_LIMIT),
        name="ffn",
    )(x, x, x, mod_l, g, w_up_b, w_up_b, cw, cw, cb, cb, w_down_b)


def _axial_tables(seq, sec, lanes_used):
    pos = jnp.arange(seq, dtype=jnp.int32)
    row = (pos // GRID_W).astype(F32)
    col = (pos % GRID_W).astype(F32)
    inv = ROPE_THETA ** (-jnp.arange(0, sec, 2, dtype=F32) / sec)
    half = sec // 2
    lane = jnp.arange(LANES)
    in_use = lane < lanes_used
    which = (lane // sec) % 2
    freq = inv[lane % half]
    ang = jnp.where(which[None, :] == 0, row[:, None], col[:, None]) * freq[None, :]
    cos_t = jnp.where(in_use[None, :], jnp.cos(ang), 1.0)
    sign = jnp.where((lane % sec) < half, -1.0, 1.0)
    sin_t = jnp.where(in_use[None, :], jnp.sin(ang) * sign[None, :], 0.0)
    return cos_t.astype(F32), sin_t.astype(F32)


def _hyena_feats(seq):
    n = jnp.arange(seq, dtype=jnp.int32)
    posi = jnp.concatenate([seq - n, n])
    valid = (posi < seq).astype(F32)
    posi = jnp.minimum(posi, seq - 1)
    tlin = jnp.linspace(0.0, 1.0, seq, dtype=F32)[posi]
    bands = jnp.linspace(1e-4, POS_BANDS - 1, POS_BANDS, dtype=F32)
    ang = (2.0 * math.pi / seq) * posi.astype(F32)[:, None] * bands[None, :]
    feats = jnp.concatenate([tlin[:, None], jnp.cos(ang), -jnp.sin(ang)], axis=-1)
    pad = jnp.zeros((2 * seq, LANES - POS_EMB - 1), F32)
    return jnp.concatenate([feats, pad, valid[:, None]], axis=-1)


def _dft_mats(p):
    r = jnp.arange(p, dtype=jnp.int32)
    q = (r[:, None] * r[None, :]) % (2 * p)
    ang = q.astype(F32) * (math.pi / p)
    cosm = jnp.cos(ang)
    sinm = jnp.sin(ang)
    alt = jnp.where(r % 2 == 0, 1.0, -1.0).astype(F32)
    im_rows = jnp.where((r == 0)[:, None], alt[None, :], -sinm)
    fwd = jnp.concatenate([cosm, im_rows], axis=0)
    wre = jnp.where((r == 0)[None, :], 0.5, 1.0) / p
    inv_re = cosm * wre
    inv_im = jnp.where((r == 0)[None, :], alt[:, None] * (0.5 / p), -sinm / p)
    inv = jnp.concatenate([inv_re, inv_im], axis=1)
    return fwd.astype(BF16), inv.astype(BF16)


def kernel(x_prompt, x_sample, c_prompt, c_sample, ada_w, ada_b, norm1_g, w_in, gm_vnorm_g, gm_spatial_w, gm_spatial_b, hy_conv_w, hy_conv_b, hy_w1, hy_b1, hy_f1, hy_w2, hy_b2, hy_f2, hy_w3, hy_decay, hy_skip, gqa_qn_g, gqa_kn_g, mla_q_a_g, mla_w_q_b, mla_kv_a_g, mla_w_kv_b, mla_qn_g, mla_kn_g, group_norm_g, w_out, norm2_g, ffn_w_up, ffn_conv_w, ffn_conv_b, ffn_w_down):
    nl = DEPTH
    w_in_b = jnp.pad(w_in, ((0, 0), (0, 0), (0, IN_PAD - IN_COLS))).astype(BF16)
    w_out_b = w_out.astype(BF16)
    w_up_b = ffn_w_up.astype(BF16)
    w_down_b = ffn_w_down.astype(BF16)
    ws_b = gm_spatial_w.astype(BF16)
    gm_bias = jnp.broadcast_to(jnp.swapaxes(gm_spatial_b, 1, 2)[:, :, :, None],
                               (nl, CHUNK, A_HEADS, LANES)).reshape(nl, CHUNK, W_GROUP)
    row3 = lambda a: a.reshape(nl, 1, a.shape[-1])
    norm1 = row3(norm1_g)
    norm2 = row3(norm2_g)
    gm_g = row3(gm_vnorm_g)
    gg = row3(group_norm_g)
    hy_cw = hy_conv_w.reshape(nl, 3, 3, W_GROUP).transpose(0, 2, 1, 3)
    hy_cb = hy_conv_b.reshape(nl, 3, W_GROUP)
    padl = lambda a, rows, cols: jnp.pad(a, ((0, 0), (0, rows - a.shape[1]), (0, cols - a.shape[2])))
    hy_w1p = padl(hy_w1, LANES, LANES)
    hy_w2p = padl(hy_w2, LANES, LANES)
    hy_b1p = padl(row3(hy_b1), 1, LANES)
    hy_f1p = padl(row3(hy_f1), 1, LANES)
    hy_b2p = padl(row3(hy_b2), 1, LANES)
    hy_f2p = padl(row3(hy_f2), 1, LANES)
    hy_w3d = jnp.pad(hy_w3.reshape(nl, FILT_H, 2, 2, W_GROUP).transpose(0, 3, 1, 2, 4)
                     .reshape(nl, 2, FILT_H, 2 * W_GROUP), ((0, 0), (0, 0), (0, LANES - FILT_H), (0, 0)))
    hy_decd = hy_decay.reshape(nl, 2, 2, W_GROUP).transpose(0, 2, 1, 3).reshape(nl, 2, 1, 2 * W_GROUP)
    gqa_qg = row3(gqa_qn_g)
    gqa_kg = row3(gqa_kn_g)
    mla_qag = row3(mla_q_a_g)
    mla_kvg = row3(mla_kv_a_g)
    slot_pad = MLA_SLOT - NOPE - ROPE_D
    wq_b = jnp.pad(mla_w_q_b.reshape(nl, Q_LORA, D_HEADS, NOPE + ROPE_D),
                   ((0, 0), (0, 0), (0, 0), (0, slot_pad))).reshape(nl, Q_LORA, D_HEADS * MLA_SLOT).astype(BF16)
    wkv_b = mla_w_kv_b.astype(BF16)
    mla_qn = jnp.pad(row3(mla_qn_g), ((0, 0), (0, 0), (0, slot_pad)))
    mla_kn = jnp.pad(row3(mla_kn_g), ((0, 0), (0, 0), (0, slot_pad)))
    ffn_cb = row3(ffn_conv_b)

    nbp = x_prompt.shape[0]
    c_all = jnp.concatenate([c_prompt, c_sample], axis=0)
    mod = _ada_mod(c_all, ada_w, ada_b).reshape(nl, c_all.shape[0], 6, D_MODEL)

    fwd_b, inv_b = _dft_mats(HY_P)

    def trunk(x3, b0):
        bsz, seq, _ = x3.shape
        x = x3.reshape(bsz * seq, D_MODEL)
        gcos, gsin = _axial_tables(seq, C_HD // 2, C_HD)
        mcos, msin = _axial_tables(seq, ROPE_D // 2, ROPE_D)
        feats = _hyena_feats(seq)
        sub = ATTN_SCORE_ELEMS // seq
        for l in range(nl):
            mod_l = mod[l]
            proj = _in_proj(x, mod_l, b0, seq, norm1, w_in_b, l)
            ya = _gmlp(proj, gm_g, ws_b, gm_bias, l)
            kseq = _hy_filters(feats, hy_w1p, hy_b1p, hy_f1p, hy_w2p, hy_b2p, hy_f2p, hy_w3d, hy_decd, l, seq)
            hfilt = _hy_filter_dft(kseq, fwd_b, seq)
            yb = _hyena(proj, hy_cw, hy_cb, hy_skip, hfilt, fwd_b, inv_b, l, seq)
            gq, gk = _gqa_prep(proj, gcos, gsin, gqa_qg, gqa_kg, l, seq)
            yc = _attention(gq, gk, proj, GQA_V_COL // C_HD, seq, C_KV_HEADS, C_HEADS // C_KV_HEADS, C_HD, C_HD,
                            2 * sub, sub, "gqa_attn")
            mq, mk, mv = _mla_prep(proj, mcos, msin, mla_qag, mla_kvg, wq_b, wkv_b, mla_qn, mla_kn, l, seq)
            yd = _attention(mq, mk, mv, 0, seq, D_HEADS, 1, MLA_SLOT, V_HD, 4 * sub, sub, "mla_attn")
            x = _out_proj(ya, yb, yc, yd, x, mod_l, b0, seq, gg, w_out_b, l)
            x = _ffn(x, mod_l, b0, seq, norm2, w_up_b, ffn_conv_w, ffn_cb, w_down_b, l)
        return x.reshape(bsz, seq, D_MODEL)

    return trunk(x_prompt, 0), trunk(x_sample, nbp)
```

```python
import functools
import math

import jax
import jax.numpy as jnp
from jax import lax
from jax.experimental import pallas as pl
from jax.experimental.pallas import tpu as pltpu

F32 = jnp.float32
BF16 = jnp.bfloat16

D_MODEL = 2048
DEPTH = 4
GRID_W = 64
CHUNK = 128
W_GROUP = 512
A_HEADS = 4
POS_BANDS = 16
POS_EMB = 2 * POS_BANDS + 1
FILT_H = 64
C_HEADS = 4
C_KV_HEADS = 2
C_HD = 128
D_HEADS = 4
Q_LORA = 512
KV_LORA = 256
NOPE = 128
ROPE_D = 64
V_HD = 128
ROPE_THETA = 10000.0
D_FF = 5632
EPS = 1e-6
LOG2E = 1.4426950408889634
IN_A = 1024
IN_B = 1536
IN_C = 1024
IN_D = 832
IN_COLS = IN_A + IN_B + IN_C + IN_D
GQA_V_COL = IN_A + IN_B + (C_HEADS + C_KV_HEADS) * C_HD
IN_PAD = 4608

LANES = 128
SUBLANES = 8
BF16_ROWS = 16
NORM_ROWS = BF16_ROWS
VMEM_LIMIT = 56 * 1024 * 1024
FFN_VMEM_LIMIT = 60 * 1024 * 1024

ATTN_SCORE_ELEMS = 256 * 4096
HY_FILTER_ROWS = 512
HY_P = 512
FFN_HALO = BF16_ROWS
MLA_SLOT = 256


def _params(sem, vmem=VMEM_LIMIT):
    return pltpu.CompilerParams(dimension_semantics=sem, vmem_limit_bytes=vmem)


def _single(block_shape, index_map):
    return pl.BlockSpec(block_shape, index_map, pipeline_mode=pl.Buffered(1))


def _ada_kernel(c_ref, w_ref, b_ref, o_ref):
    s = jax.nn.silu(c_ref[...]).astype(BF16)
    o_ref[0] = jnp.dot(s, w_ref[0].astype(BF16), preferred_element_type=F32) + b_ref[0]


def _ada_mod(c_all, ada_w, ada_b):
    nb = c_all.shape[0]
    tn = 1024
    return pl.pallas_call(
        _ada_kernel,
        grid=(DEPTH, 6 * D_MODEL // tn),
        in_specs=[
            pl.BlockSpec((nb, D_MODEL), lambda l, j: (0, 0)),
            pl.BlockSpec((1, D_MODEL, tn), lambda l, j: (l, 0, j)),
            pl.BlockSpec((1, 1, tn), lambda l, j: (l, 0, j)),
        ],
        out_specs=pl.BlockSpec((1, nb, tn), lambda l, j: (l, 0, j)),
        out_shape=jax.ShapeDtypeStruct((DEPTH, nb, 6 * D_MODEL), F32),
        compiler_params=_params(("parallel", "parallel")),
        name="ada_mod",
    )(c_all, ada_w, ada_b.reshape(DEPTH, 1, 6 * D_MODEL))


def _mod_norm_rows(x, gs, shift):
    ms = jnp.mean(x * x, axis=-1, keepdims=True)
    return (x * lax.rsqrt(ms + EPS)) * gs + shift


def _inproj_kernel(xn_ref, x0_ref, modn_ref, mod0_ref, g_ref, w_ref, o_ref, ha_ref, hb_ref, *, tm):
    i = pl.program_id(0)
    rc = NORM_ROWS
    g = g_ref[...]

    def fill(x_ref, mod_ref, dst_ref):
        shift = mod_ref[0, 0:1, :]
        gs = g * (1.0 + mod_ref[0, 1:2, :])
        for c in range(tm // rc):
            rows = slice(c * rc, (c + 1) * rc)
            dst_ref[rows, :] = _mod_norm_rows(x_ref[rows, :], gs, shift).astype(BF16)

    @pl.when(i == 0)
    def _():
        fill(x0_ref, mod0_ref, ha_ref)

    def step(cur_ref, nxt_ref):
        fill(xn_ref, modn_ref, nxt_ref)
        o_ref[...] = jnp.dot(cur_ref[...], w_ref[...], preferred_element_type=F32).astype(o_ref.dtype)

    @pl.when(i % 2 == 0)
    def _():
        step(ha_ref, hb_ref)

    @pl.when(i % 2 == 1)
    def _():
        step(hb_ref, ha_ref)


def _in_proj(x, mod_l, b0, seq, g, w_in_b, layer):
    t = x.shape[0]
    tm = 512
    n = t // tm
    nxt = lambda i: jnp.minimum(i + 1, n - 1)
    return pl.pallas_call(
        functools.partial(_inproj_kernel, tm=tm),
        grid=(n,),
        in_specs=[
            pl.BlockSpec((tm, D_MODEL), lambda i: (nxt(i), 0)),
            _single((tm, D_MODEL), lambda i: (0, 0)),
            pl.BlockSpec((1, 6, D_MODEL), lambda i: (b0 + (nxt(i) * tm) // seq, 0, 0)),
            pl.BlockSpec((1, 6, D_MODEL), lambda i: (b0, 0, 0)),
            pl.BlockSpec((None, 1, D_MODEL), lambda i: (layer, 0, 0)),
            _single((None, D_MODEL, IN_PAD), lambda i: (layer, 0, 0)),
        ],
        out_specs=pl.BlockSpec((tm, IN_PAD), lambda i: (i, 0)),
        out_shape=jax.ShapeDtypeStruct((t, IN_PAD), BF16),
        scratch_shapes=[pltpu.VMEM((tm, D_MODEL), BF16), pltpu.VMEM((tm, D_MODEL), BF16)],
        compiler_params=_params(("arbitrary",)),
        name="in_proj",
    )(x, x, mod_l, mod_l, g, w_in_b)


def _gelu(x):
    return 0.5 * x * (1.0 + lax.erf(x * (1.0 / math.sqrt(2.0))))


def _gmlp_kernel(u_ref, v_ref, g_ref, ws_ref, bias_ref, o_ref, *, tm):
    g = g_ref[...]
    for n in range(tm // CHUNK):
        rows = slice(n * CHUNK, (n + 1) * CHUNK)
        u = _gelu(u_ref[rows, :].astype(F32))
        v = _gelu(v_ref[rows, :].astype(F32))
        vc = v - jnp.mean(v, axis=-1, keepdims=True)
        vn = (vc * lax.rsqrt(jnp.mean(vc * vc, axis=-1, keepdims=True) + EPS) * g).astype(BF16)
        for h in range(A_HEADS):
            cols = slice(h * LANES, (h + 1) * LANES)
            mixed = jnp.dot(ws_ref[h], vn[:, cols], preferred_element_type=F32) + bias_ref[:, cols]
            o_ref[rows, cols] = (u[:, cols] * mixed).astype(o_ref.dtype)


def _gmlp(proj, g, ws_b, bias_full, layer):
    t = proj.shape[0]
    tm = 512
    return pl.pallas_call(
        functools.partial(_gmlp_kernel, tm=tm),
        grid=(t // tm,),
        in_specs=[
            pl.BlockSpec((tm, W_GROUP), lambda i: (i, 0)),
            pl.BlockSpec((tm, W_GROUP), lambda i: (i, 1)),
            pl.BlockSpec((None, 1, W_GROUP), lambda i: (layer, 0, 0)),
            pl.BlockSpec((None, A_HEADS, CHUNK, CHUNK), lambda i: (layer, 0, 0, 0)),
            pl.BlockSpec((None, CHUNK, W_GROUP), lambda i: (layer, 0, 0)),
        ],
        out_specs=pl.BlockSpec((tm, W_GROUP), lambda i: (i, 0)),
        out_shape=jax.ShapeDtypeStruct((t, W_GROUP), BF16),
        compiler_params=_params(("parallel",)),
        name="gmlp",
    )(proj, proj, g, ws_b, bias_full)


def _hy_filter_kernel(f_ref, w1_ref, b1_ref, f1_ref, w2_ref, b2_ref, f2_ref, w3_ref, dec_ref, o_ref):
    hp = lax.Precision.HIGHEST
    feats = f_ref[...]
    h = jnp.sin(f1_ref[...] * (jnp.dot(feats, w1_ref[...], precision=hp, preferred_element_type=F32) + b1_ref[...]))
    h = jnp.sin(f2_ref[...] * (jnp.dot(h, w2_ref[...], precision=hp, preferred_element_type=F32) + b2_ref[...]))
    half = feats.shape[0]
    decay = jnp.abs(dec_ref[0])
    for grp in range(2):
        k = jnp.dot(h, w3_ref[0, grp], precision=hp, preferred_element_type=F32)
        tcol = feats[:, grp * FILT_H:grp * FILT_H + 1]
        valid = feats[:, (grp + 1) * FILT_H - 1:(grp + 1) * FILT_H]
        k = k * jnp.exp(-tcol * decay) * valid
        o_ref[0, grp * half:(grp + 1) * half, :] = k[:, :W_GROUP]
        o_ref[1, grp * half:(grp + 1) * half, :] = k[:, W_GROUP:]


def _hy_filters(feats, w1p, b1p, f1p, w2p, b2p, f2p, w3d, decd, layer, seq):
    rt = HY_FILTER_ROWS
    half = seq // rt
    c2 = lambda i: (layer, 0, 0)
    return pl.pallas_call(
        _hy_filter_kernel,
        grid=(2 * seq // rt,),
        in_specs=[
            pl.BlockSpec((rt // 2, LANES), lambda i: (i, 0)),
            pl.BlockSpec((None, LANES, LANES), c2),
            pl.BlockSpec((None, 1, LANES), c2),
            pl.BlockSpec((None, 1, LANES), c2),
            pl.BlockSpec((None, LANES, LANES), c2),
            pl.BlockSpec((None, 1, LANES), c2),
            pl.BlockSpec((None, 1, LANES), c2),
            pl.BlockSpec((None, 1, 2, LANES, 2 * W_GROUP), lambda i: (layer, jnp.where(i < half, 1, 0), 0, 0, 0)),
            pl.BlockSpec((None, 1, 1, 2 * W_GROUP), lambda i: (layer, jnp.where(i < half, 1, 0), 0, 0)),
        ],
        out_specs=pl.BlockSpec((2, rt, W_GROUP), lambda i: (0, i, 0)),
        out_shape=jax.ShapeDtypeStruct((2, 2 * seq, W_GROUP), F32),
        compiler_params=_params(("parallel",)),
        name="hy_filter",
    )(feats, w1p, b1p, f1p, w2p, b2p, f2p, w3d, decd)


def _hy_fdft_kernel(k_ref, fwd_ref, o_ref, prev_ref):
    e = pl.program_id(1)
    cur = jnp.dot(fwd_ref[...], k_ref[0].astype(BF16), preferred_element_type=F32)

    @pl.when(e > 0)
    def _():
        prev = prev_ref[...]
        odd = (lax.broadcasted_iota(jnp.int32, prev.shape, 0) & 1) == 1
        o_ref[0, 0] = cur + jnp.where(odd, -prev, prev)

    prev_ref[...] = cur


def _hy_filter_dft(kseq, fwd_b, seq):
    p = HY_P
    nb = seq // p
    return pl.pallas_call(
        _hy_fdft_kernel,
        grid=(2, 2 * nb),
        in_specs=[
            pl.BlockSpec((1, p, W_GROUP), lambda o, e: (o, e, 0)),
            pl.BlockSpec((2 * p, p), lambda o, e: (0, 0)),
        ],
        out_specs=pl.BlockSpec((1, 1, 2 * p, W_GROUP), lambda o, e: (o, jnp.maximum(e - 1, 0), 0, 0)),
        out_shape=jax.ShapeDtypeStruct((2, 2 * nb - 1, 2 * p, W_GROUP), F32),
        scratch_shapes=[pltpu.VMEM((2 * p, W_GROUP), F32)],
        compiler_params=_params(("arbitrary", "arbitrary")),
        name="hy_filter_dft",
    )(kseq, fwd_b)


def _dwconv_rows(src_ref, r0, rc, total, w, b):
    halo = BF16_ROWS
    xc = src_ref[pl.ds(r0, rc), :].astype(F32)
    prev = src_ref[pl.ds(pl.multiple_of(jnp.maximum(r0 - halo, 0), halo), halo), :].astype(F32)
    nxt = src_ref[pl.ds(pl.multiple_of(jnp.minimum(r0 + rc, total - halo), halo), halo), :].astype(F32)
    prev = jnp.where(r0 > 0, prev, 0.0)
    nxt = jnp.where(r0 + rc < total, nxt, 0.0)
    cat = jnp.concatenate([prev, xc, nxt], axis=0)
    n = rc + 2 * halo
    dn = pltpu.roll(cat, 1, 0)[halo:halo + rc]
    up = pltpu.roll(cat, n - 1, 0)[halo:halo + rc]
    return dn * w[0:1] + xc * w[1:2] + up * w[2:3] + b


def _hyena_kernel(p1_ref, p2_ref, pv_ref, cw_ref, cb_ref, skip_ref, h_ref, fwd_ref, inv_ref, o_ref,
                  vz, x1z, x2z, vhat0, vhat1, yhat_a, yhat_b, *, seq, p):
    nb = seq // p
    rc = 256
    rm = 32
    yhat = (yhat_a, yhat_b)

    def conv_body(c, carry):
        r0 = pl.multiple_of(c * rc, rc)
        rows = pl.ds(r0, rc)
        x1z[rows, :] = _dwconv_rows(p1_ref, r0, rc, seq, cw_ref[0], cb_ref[0:1, :])
        x2z[rows, :] = _dwconv_rows(p2_ref, r0, rc, seq, cw_ref[1], cb_ref[1:2, :])
        vz[rows, :] = _dwconv_rows(pv_ref, r0, rc, seq, cw_ref[2], cb_ref[2:3, :])
        return carry

    lax.fori_loop(0, seq // rc, conv_body, 0)

    def fwd_body(i, carry):
        rows = pl.ds(pl.multiple_of(i * p, p), p)
        vhat0[i] = jnp.dot(fwd_ref[...], vz[rows, :].astype(BF16), preferred_element_type=F32)
        return carry

    lax.fori_loop(0, nb, fwd_body, 0)

    def block_spectrum(order, i, vhat, dst):
        for r in range(p // rm):
            rows_re = slice(r * rm, (r + 1) * rm)
            rows_im = slice(p + r * rm, p + (r + 1) * rm)
            acc_r = None
            acc_i = None
            for ip in range(nb):
                d = i - ip + (nb - 1)
                hr = h_ref[order, d, rows_re, :]
                hi = h_ref[order, d, rows_im, :]
                ur = vhat[ip, rows_re, :]
                ui = vhat[ip, rows_im, :]
                if r == 0:
                    first = lax.broadcasted_iota(jnp.int32, hr.shape, 0) == 0
                    tr = hr * ur - jnp.where(first, 0.0, hi * ui)
                    ti = jnp.where(first, hi * ui, hr * ui + hi * ur)
                else:
                    tr = hr * ur - hi * ui
                    ti = hr * ui + hi * ur
                acc_r = tr if acc_r is None else acc_r + tr
                acc_i = ti if acc_i is None else acc_i + ti
            dst[rows_re, :] = acc_r.astype(BF16)
            dst[rows_im, :] = acc_i.astype(BF16)

    for order in range(2):
        gate = x1z if order == 0 else x2z
        vhat = vhat0 if order == 0 else vhat1
        skip = skip_ref[order:order + 1, :]
        for i in range(nb + 1):
            if i >= 1:
                y = jnp.dot(inv_ref[...], yhat[(i - 1) % 2][...], preferred_element_type=F32)
            if i < nb:
                block_spectrum(order, i, vhat, yhat[i % 2])
            if i >= 1:
                rows = slice((i - 1) * p, i * p)
                res = gate[rows, :] * (y + skip * vz[rows, :])
                if order == 0:
                    vz[rows, :] = res
                    vhat1[i - 1] = jnp.dot(fwd_ref[...], res.astype(BF16), preferred_element_type=F32)
                else:
                    o_ref[rows, :] = res.astype(o_ref.dtype)


def _hyena(proj, cw, cb, skip, hfilt, fwd_b, inv_b, layer, seq):
    t = proj.shape[0]
    p = HY_P
    nb = seq // p
    nct = W_GROUP // LANES
    base = IN_A // LANES
    return pl.pallas_call(
        functools.partial(_hyena_kernel, seq=seq, p=p),
        grid=(nct, t // seq),
        in_specs=[
            pl.BlockSpec((seq, LANES), lambda c, b: (b, base + c)),
            pl.BlockSpec((seq, LANES), lambda c, b: (b, base + nct + c)),
            pl.BlockSpec((seq, LANES), lambda c, b: (b, base + 2 * nct + c)),
            pl.BlockSpec((None, 3, 3, LANES), lambda c, b: (layer, 0, 0, c)),
            pl.BlockSpec((None, 3, LANES), lambda c, b: (layer, 0, c)),
            pl.BlockSpec((None, 2, LANES), lambda c, b: (layer, 0, c)),
            _single((2, 2 * nb - 1, 2 * p, LANES), lambda c, b: (0, 0, 0, c)),
            _single((2 * p, p), lambda c, b: (0, 0)),
            _single((p, 2 * p), lambda c, b: (0, 0)),
        ],
        out_specs=pl.BlockSpec((seq, LANES), lambda c, b: (b, c)),
        out_shape=jax.ShapeDtypeStruct((t, W_GROUP), BF16),
        scratch_shapes=[
            pltpu.VMEM((seq, LANES), F32),
            pltpu.VMEM((seq, LANES), F32),
            pltpu.VMEM((seq, LANES), F32),
            pltpu.VMEM((nb, 2 * p, LANES), F32),
            pltpu.VMEM((nb, 2 * p, LANES), F32),
            pltpu.VMEM((2 * p, LANES), BF16),
            pltpu.VMEM((2 * p, LANES), BF16),
        ],
        compiler_params=_params(("parallel", "parallel")),
        name="hyena",
    )(proj, proj, proj, cw, cb, skip, hfilt, fwd_b, inv_b)


def _rope_lanes(x, cos_t, sin_t, half):
    lane = lax.broadcasted_iota(jnp.int32, x.shape, 1)
    lower = (lane % (2 * half)) < half
    partner = jnp.where(lower, pltpu.roll(x, LANES - half, 1), pltpu.roll(x, half, 1))
    return x * cos_t + partner * sin_t


def _rms_rows(x, g, width):
    ss = jnp.sum(x * x, axis=-1, keepdims=True) * (1.0 / width)
    return x * lax.rsqrt(ss + EPS) * g


def _gqa_prep_kernel(q_ref, k_ref, cos_ref, sin_ref, qg_ref, kg_ref, qo_ref, ko_ref):
    cos_t = cos_ref[...]
    sin_t = sin_ref[...]
    scale = LOG2E / math.sqrt(C_HD)
    for h in range(C_HEADS):
        cols = slice(h * C_HD, (h + 1) * C_HD)
        q = _rope_lanes(_rms_rows(q_ref[:, cols].astype(F32), qg_ref[...], C_HD), cos_t, sin_t, C_HD // 4)
        qo_ref[:, cols] = (q * scale).astype(BF16)
    for h in range(C_KV_HEADS):
        cols = slice(h * C_HD, (h + 1) * C_HD)
        k = _rope_lanes(_rms_rows(k_ref[:, cols].astype(F32), kg_ref[...], C_HD), cos_t, sin_t, C_HD // 4)
        ko_ref[:, cols] = k.astype(BF16)


def _gqa_prep(proj, cos_t, sin_t, qg, kg, layer, seq):
    t = proj.shape[0]
    tm = 512
    qw = C_HEADS * C_HD
    kw = C_KV_HEADS * C_HD
    c0 = IN_A + IN_B
    g2 = lambda i: (layer, 0, 0)
    return pl.pallas_call(
        _gqa_prep_kernel,
        grid=(t // tm,),
        in_specs=[
            pl.BlockSpec((tm, qw), lambda i: (i, c0 // qw)),
            pl.BlockSpec((tm, kw), lambda i: (i, (c0 + qw) // kw)),
            pl.BlockSpec((tm, LANES), lambda i: (i % (seq // tm), 0)),
            pl.BlockSpec((tm, LANES), lambda i: (i % (seq // tm), 0)),
            pl.BlockSpec((None, 1, C_HD), g2),
            pl.BlockSpec((None, 1, C_HD), g2),
        ],
        out_specs=[
            pl.BlockSpec((tm, qw), lambda i: (i, 0)),
            pl.BlockSpec((tm, kw), lambda i: (i, 0)),
        ],
        out_shape=[
            jax.ShapeDtypeStruct((t, qw), BF16),
            jax.ShapeDtypeStruct((t, kw), BF16),
        ],
        compiler_params=_params(("parallel",)),
        name="gqa_prep",
    )(proj, proj, cos_t, sin_t, qg, kg)


def _mla_prep_kernel(qa_ref, ckv_ref, kr_ref, cos_ref, sin_ref, qag_ref, kvg_ref, wq_ref, wkv_ref,
                     qn_ref, kn_ref, qo_ref, ko_ref, vo_ref):
    cos_t = cos_ref[...]
    sin_t = sin_ref[...]
    width = NOPE + ROPE_D
    scale = LOG2E / math.sqrt(width)
    qa = _rms_rows(qa_ref[...].astype(F32), qag_ref[...], Q_LORA).astype(BF16)
    q = jnp.dot(qa, wq_ref[...], preferred_element_type=F32)
    ckv = _rms_rows(ckv_ref[...].astype(F32), kvg_ref[...], KV_LORA).astype(BF16)
    kv = jnp.dot(ckv, wkv_ref[...], preferred_element_type=F32)
    kr = kr_ref[...].astype(F32)
    kr_ss = jnp.sum(kr * kr, axis=-1, keepdims=True)
    qn = qn_ref[...]
    kn = kn_ref[...]
    for h in range(D_HEADS):
        base = h * MLA_SLOT
        qh = _rms_rows(q[:, base:base + MLA_SLOT], qn, width)
        qo_ref[:, base:base + NOPE] = (qh[:, :NOPE] * scale).astype(BF16)
        qr = _rope_lanes(qh[:, NOPE:], cos_t, sin_t, ROPE_D // 4)
        qo_ref[:, base + NOPE:base + MLA_SLOT] = (qr * scale).astype(BF16)
        k_nope = kv[:, base:base + NOPE]
        ss = (jnp.sum(k_nope * k_nope, axis=-1, keepdims=True) + kr_ss) * (1.0 / width)
        inv = lax.rsqrt(ss + EPS)
        ko_ref[:, base:base + NOPE] = (k_nope * inv * kn[:, :NOPE]).astype(BF16)
        krn = _rope_lanes(kr * inv * kn[:, NOPE:], cos_t, sin_t, ROPE_D // 4)
        ko_ref[:, base + NOPE:base + MLA_SLOT] = krn.astype(BF16)
        vo_ref[:, h * V_HD:(h + 1) * V_HD] = kv[:, base + NOPE:base + NOPE + V_HD].astype(BF16)


def _mla_prep(proj, cos_t, sin_t, qag, kvg, wq_b, wkv_b, qn_p, kn_p, layer, seq):
    t = proj.shape[0]
    tm = 512
    c0 = IN_A + IN_B + IN_C
    g2 = lambda i: (layer, 0, 0)
    hw = D_HEADS * MLA_SLOT
    return pl.pallas_call(
        _mla_prep_kernel,
        grid=(t // tm,),
        in_specs=[
            pl.BlockSpec((tm, Q_LORA), lambda i: (i, c0 // Q_LORA)),
            pl.BlockSpec((tm, KV_LORA), lambda i: (i, (c0 + Q_LORA) // KV_LORA)),
            pl.BlockSpec((tm, LANES), lambda i: (i, (c0 + Q_LORA + KV_LORA) // LANES)),
            pl.BlockSpec((tm, LANES), lambda i: (i % (seq // tm), 0)),
            pl.BlockSpec((tm, LANES), lambda i: (i % (seq // tm), 0)),
            pl.BlockSpec((None, 1, Q_LORA), g2),
            pl.BlockSpec((None, 1, KV_LORA), g2),
            pl.BlockSpec((None, Q_LORA, hw), g2),
            pl.BlockSpec((None, KV_LORA, hw), g2),
            pl.BlockSpec((None, 1, MLA_SLOT), g2),
            pl.BlockSpec((None, 1, MLA_SLOT), g2),
        ],
        out_specs=[
            pl.BlockSpec((tm, hw), lambda i: (i, 0)),
            pl.BlockSpec((tm, hw), lambda i: (i, 0)),
            pl.BlockSpec((tm, D_HEADS * V_HD), lambda i: (i, 0)),
        ],
        out_shape=[
            jax.ShapeDtypeStruct((t, hw), BF16),
            jax.ShapeDtypeStruct((t, hw), BF16),
            jax.ShapeDtypeStruct((t, D_HEADS * V_HD), BF16),
        ],
        compiler_params=_params(("parallel",)),
        name="mla_prep",
    )(proj, proj, proj, cos_t, sin_t, qag, kvg, wq_b, wkv_b, qn_p, kn_p)


def _attn_kernel(q_ref, k_ref, v_ref, o_ref, vext_ref, *, groups, dk, dv, sub):
    @pl.when(pl.program_id(2) == 0)
    def _():
        vext_ref[:, :dv] = v_ref[...]
        vext_ref[:, dv:] = jnp.ones((v_ref.shape[0], dv), BF16)

    k = k_ref[...]
    v = vext_ref[...]
    tq = q_ref.shape[0]
    chains = [(g, slice(r * sub, (r + 1) * sub)) for g in range(groups) for r in range(tq // sub)]

    def scores(c):
        g, rows = chains[c]
        q = q_ref[rows, g * dk:(g + 1) * dk]
        return lax.dot_general(q, k, (((1,), (1,)), ((), ())), preferred_element_type=F32)

    s = scores(0)
    for c, (g, rows) in enumerate(chains):
        s_next = scores(c + 1) if c + 1 < len(chains) else None
        m = jnp.max(s, axis=-1, keepdims=True)
        pexp = jnp.exp2(s - m).astype(BF16)
        o = jnp.dot(pexp, v, preferred_element_type=F32)
        o_ref[rows, g * dv:(g + 1) * dv] = (o[:, :dv] / o[:, dv:]).astype(o_ref.dtype)
        s = s_next


def _attention(q, k, v, v_col0, seq, kv_heads, groups, dk, dv, tq, sub, name):
    t = q.shape[0]
    nq = seq // tq
    return pl.pallas_call(
        functools.partial(_attn_kernel, groups=groups, dk=dk, dv=dv, sub=sub),
        grid=(t // seq, kv_heads, nq),
        in_specs=[
            pl.BlockSpec((tq, groups * dk), lambda b, h, i: (b * nq + i, h)),
            pl.BlockSpec((seq, dk), lambda b, h, i: (b, h)),
            pl.BlockSpec((seq, dv), lambda b, h, i: (b, v_col0 + h)),
        ],
        out_specs=pl.BlockSpec((tq, groups * dv), lambda b, h, i: (b * nq + i, h)),
        out_shape=jax.ShapeDtypeStruct((t, kv_heads * groups * dv), BF16),
        scratch_shapes=[pltpu.VMEM((seq, 2 * dv), BF16)],
        compiler_params=_params(("parallel", "parallel", "arbitrary")),
        name=name,
    )(q, k, v)


def _outproj_kernel(ya_ref, yb_ref, yc_ref, yd_ref, x_ref, mod_ref, gg_ref, w_ref, o_ref):
    acc = None
    for gi, y_ref in enumerate((ya_ref, yb_ref, yc_ref, yd_ref)):
        rows = slice(gi * W_GROUP, (gi + 1) * W_GROUP)
        yn = _rms_rows(y_ref[...].astype(F32), gg_ref[:, rows], W_GROUP).astype(BF16)
        part = jnp.dot(yn, w_ref[rows, :], preferred_element_type=F32)
        acc = part if acc is None else acc + part
    o_ref[...] = x_ref[...] + mod_ref[0, 2:3, :] * acc


def _out_proj(ya, yb, yc, yd, x, mod_l, b0, seq, gg, w_out_b, layer):
    t = x.shape[0]
    tm = 512
    yspec = pl.BlockSpec((tm, W_GROUP), lambda i: (i, 0))
    return pl.pallas_call(
        _outproj_kernel,
        grid=(t // tm,),
        in_specs=[
            yspec, yspec, yspec, yspec,
            pl.BlockSpec((tm, D_MODEL), lambda i: (i, 0)),
            pl.BlockSpec((1, 6, D_MODEL), lambda i: (b0 + (i * tm) // seq, 0, 0)),
            pl.BlockSpec((None, 1, D_MODEL), lambda i: (layer, 0, 0)),
            _single((None, D_MODEL, D_MODEL), lambda i: (layer, 0, 0)),
        ],
        out_specs=pl.BlockSpec((tm, D_MODEL), lambda i: (i, 0)),
        out_shape=jax.ShapeDtypeStruct((t, D_MODEL), F32),
        compiler_params=_params(("parallel",)),
        name="out_proj",
    )(ya, yb, yc, yd, x, mod_l, gg, w_out_b)


def _ffn_kernel(x_ref, xp_ref, xn_ref, mod_ref, g_ref, wg_ref, wu_ref, cwg_ref, cwu_ref, cbg_ref, cbu_ref,
                wd_ref, o_ref, h_ref, *, tm, seq):
    i = pl.program_id(0)
    j = pl.program_id(1)
    nj = pl.num_programs(1)
    rc = NORM_ROWS
    halo = FFN_HALO

    @pl.when(j == 0)
    def _():
        shift = mod_ref[0, 3:4, :]
        gs = g_ref[...] * (1.0 + mod_ref[0, 4:5, :])
        has_prev = ((i * tm) % seq) != 0
        has_next = (((i + 1) * tm) % seq) != 0
        hp = _mod_norm_rows(xp_ref[...], gs, shift)
        hn = _mod_norm_rows(xn_ref[...], gs, shift)
        h_ref[pl.ds(0, halo), :] = jnp.where(has_prev, hp, 0.0).astype(BF16)
        h_ref[pl.ds(halo + tm, halo), :] = jnp.where(has_next, hn, 0.0).astype(BF16)

        chains = 8

        def body(c, carry):
            for k in range(chains):
                r0 = pl.multiple_of((c * chains + k) * rc, rc)
                h_ref[pl.ds(halo + r0, rc), :] = _mod_norm_rows(x_ref[pl.ds(r0, rc), :], gs, shift).astype(BF16)
            return carry

        lax.fori_loop(0, tm // (rc * chains), body, 0)
        o_ref[...] = jnp.zeros(o_ref.shape, F32)

    hfull = h_ref[...]
    n = tm + 2 * halo

    def conv(u, cw_ref, cb_ref):
        dn = pltpu.roll(u, 1, 0)[halo:halo + tm]
        up = pltpu.roll(u, n - 1, 0)[halo:halo + tm]
        return dn * cw_ref[0:1, :] + u[halo:halo + tm] * cw_ref[1:2, :] + up * cw_ref[2:3, :] + cb_ref[...]

    gate = conv(jnp.dot(hfull, wg_ref[...], preferred_element_type=F32), cwg_ref, cbg_ref)
    upv = conv(jnp.dot(hfull, wu_ref[...], preferred_element_type=F32), cwu_ref, cbu_ref)
    act = (jax.nn.silu(gate) * upv).astype(BF16)
    o_ref[...] += jnp.dot(act, wd_ref[...], preferred_element_type=F32)

    @pl.when(j == nj - 1)
    def _():
        o_ref[...] = x_ref[...] + mod_ref[0, 5:6, :] * o_ref[...]


def _ffn(x, mod_l, b0, seq, g, w_up_b, cw, cb, w_down_b, layer):
    t = x.shape[0]
    tm, tf = 1024, 512
    nf = D_FF // tf
    hb = tm // FFN_HALO
    last = t // FFN_HALO - 1
    return pl.pallas_call(
        functools.partial(_ffn_kernel, tm=tm, seq=seq),
        grid=(t // tm, nf),
        in_specs=[
            pl.BlockSpec((tm, D_MODEL), lambda i, j: (i, 0)),
            pl.BlockSpec((FFN_HALO, D_MODEL), lambda i, j: (jnp.maximum(i * hb - 1, 0), 0)),
            pl.BlockSpec((FFN_HALO, D_MODEL), lambda i, j: (jnp.minimum((i + 1) * hb, last), 0)),
            pl.BlockSpec((1, 6, D_MODEL), lambda i, j: (b0 + (i * tm) // seq, 0, 0)),
            pl.BlockSpec((None, 1, D_MODEL), lambda i, j: (layer, 0, 0)),
            pl.BlockSpec((None, D_MODEL, tf), lambda i, j: (layer, 0, j)),
            pl.BlockSpec((None, D_MODEL, tf), lambda i, j: (layer, 0, nf + j)),
            pl.BlockSpec((None, 3, tf), lambda i, j: (layer, 0, j)),
            pl.BlockSpec((None, 3, tf), lambda i, j: (layer, 0, nf + j)),
            pl.BlockSpec((None, 1, tf), lambda i, j: (layer, 0, j)),
            pl.BlockSpec((None, 1, tf), lambda i, j: (layer, 0, nf + j)),
            pl.BlockSpec((None, tf, D_MODEL), lambda i, j: (layer, j, 0)),
        ],
        out_specs=pl.BlockSpec((tm, D_MODEL), lambda i, j: (i, 0)),
        out_shape=jax.ShapeDtypeStruct((t, D_MODEL), F32),
        scratch_shapes=[pltpu.VMEM((tm + 2 * FFN_HALO, D_MODEL), BF16)],
        compiler_params=_params(("parallel", "arbitrary"), FFN_VMEM_LIMIT),
        name="ffn",
    )(x, x, x, mod_l, g, w_up_b, w_up_b, cw, cw, cb, cb, w_down_b)


def _axial_tables(seq, sec, lanes_used):
    pos = jnp.arange(seq, dtype=jnp.int32)
    row = (pos // GRID_W).astype(F32)
    col = (pos % GRID_W).astype(F32)
    inv = ROPE_THETA ** (-jnp.arange(0, sec, 2, dtype=F32) / sec)
    half = sec // 2
    lane = jnp.arange(LANES)
    in_use = lane < lanes_used
    which = (lane // sec) % 2
    freq = inv[lane % half]
    ang = jnp.where(which[None, :] == 0, row[:, None], col[:, None]) * freq[None, :]
    cos_t = jnp.where(in_use[None, :], jnp.cos(ang), 1.0)
    sign = jnp.where((lane % sec) < half, -1.0, 1.0)
    sin_t = jnp.where(in_use[None, :], jnp.sin(ang) * sign[None, :], 0.0)
    return cos_t.astype(F32), sin_t.astype(F32)


def _hy_filter_params(hy_w1, hy_b1, hy_f1, hy_w2, hy_b2, hy_f2, hy_w3, hy_decay):
    nl = hy_w1.shape[0]
    zero = jnp.zeros((nl, FILT_H, FILT_H), F32)
    bdiag = lambda w: jnp.concatenate([jnp.concatenate([w, zero], axis=2), jnp.concatenate([zero, w], axis=2)], axis=1)
    twice = lambda a: jnp.tile(a.reshape(nl, 1, FILT_H), (1, 1, 2))
    w1p = bdiag(jnp.pad(hy_w1, ((0, 0), (0, FILT_H - POS_EMB), (0, 0))))
    w2p = bdiag(hy_w2)
    w3 = hy_w3.reshape(nl, FILT_H, 2, 2, W_GROUP).transpose(0, 3, 1, 2, 4).reshape(nl, 2, FILT_H, 2 * W_GROUP)
    z3 = jnp.zeros_like(w3)
    w3d = jnp.stack([jnp.concatenate([w3, z3], axis=2), jnp.concatenate([z3, w3], axis=2)], axis=2)
    decd = hy_decay.reshape(nl, 2, 2, W_GROUP).transpose(0, 2, 1, 3).reshape(nl, 2, 1, 2 * W_GROUP)
    return w1p, twice(hy_b1), twice(hy_f1), w2p, twice(hy_b2), twice(hy_f2), w3d, decd


def _hyena_feats(seq):
    n = jnp.arange(seq, dtype=jnp.int32)
    posi = jnp.concatenate([seq - n, n])
    valid = (posi < seq).astype(F32)
    posi = jnp.minimum(posi, seq - 1)
    tlin = jnp.linspace(0.0, 1.0, seq, dtype=F32)[posi]
    bands = jnp.linspace(1e-4, POS_BANDS - 1, POS_BANDS, dtype=F32)
    ang = (2.0 * math.pi / seq) * posi.astype(F32)[:, None] * bands[None, :]
    feats = jnp.concatenate([tlin[:, None], jnp.cos(ang), -jnp.sin(ang)], axis=-1)
    pad = jnp.zeros((2 * seq, FILT_H - POS_EMB - 1), F32)
    f64 = jnp.concatenate([feats, pad, valid[:, None]], axis=-1)
    half = HY_FILTER_ROWS // 2
    return f64.reshape(-1, 2, half, FILT_H).transpose(0, 2, 1, 3).reshape(seq, 2 * FILT_H)


def _dft_mats(p):
    r = jnp.arange(p, dtype=jnp.int32)
    q = (r[:, None] * r[None, :]) % (2 * p)
    ang = q.astype(F32) * (math.pi / p)
    cosm = jnp.cos(ang)
    sinm = jnp.sin(ang)
    alt = jnp.where(r % 2 == 0, 1.0, -1.0).astype(F32)
    im_rows = jnp.where((r == 0)[:, None], alt[None, :], -sinm)
    fwd = jnp.concatenate([cosm, im_rows], axis=0)
    wre = jnp.where((r == 0)[None, :], 0.5, 1.0) / p
    inv_re = cosm * wre
    inv_im = jnp.where((r == 0)[None, :], alt[:, None] * (0.5 / p), -sinm / p)
    inv = jnp.concatenate([inv_re, inv_im], axis=1)
    return fwd.astype(BF16), inv.astype(BF16)


def kernel(x_prompt, x_sample, c_prompt, c_sample, ada_w, ada_b, norm1_g, w_in, gm_vnorm_g, gm_spatial_w, gm_spatial_b, hy_conv_w, hy_conv_b, hy_w1, hy_b1, hy_f1, hy_w2, hy_b2, hy_f2, hy_w3, hy_decay, hy_skip, gqa_qn_g, gqa_kn_g, mla_q_a_g, mla_w_q_b, mla_kv_a_g, mla_w_kv_b, mla_qn_g, mla_kn_g, group_norm_g, w_out, norm2_g, ffn_w_up, ffn_conv_w, ffn_conv_b, ffn_w_down):
    nl = DEPTH
    w_in_b = jnp.pad(w_in, ((0, 0), (0, 0), (0, IN_PAD - IN_COLS))).astype(BF16)
    w_out_b = w_out.astype(BF16)
    w_up_b = ffn_w_up.astype(BF16)
    w_down_b = ffn_w_down.astype(BF16)
    ws_b = gm_spatial_w.astype(BF16)
    gm_bias = jnp.broadcast_to(jnp.swapaxes(gm_spatial_b, 1, 2)[:, :, :, None],
                               (nl, CHUNK, A_HEADS, LANES)).reshape(nl, CHUNK, W_GROUP)
    row3 = lambda a: a.reshape(nl, 1, a.shape[-1])
    norm1 = row3(norm1_g)
    norm2 = row3(norm2_g)
    gm_g = row3(gm_vnorm_g)
    gg = row3(group_norm_g)
    hy_cw = hy_conv_w.reshape(nl, 3, 3, W_GROUP).transpose(0, 2, 1, 3)
    hy_cb = hy_conv_b.reshape(nl, 3, W_GROUP)
    hy_filter_params = _hy_filter_params(hy_w1, hy_b1, hy_f1, hy_w2, hy_b2, hy_f2, hy_w3, hy_decay)
    gqa_qg = row3(gqa_qn_g)
    gqa_kg = row3(gqa_kn_g)
    mla_qag = row3(mla_q_a_g)
    mla_kvg = row3(mla_kv_a_g)
    slot_pad = MLA_SLOT - NOPE - ROPE_D
    wq_b = jnp.pad(mla_w_q_b.reshape(nl, Q_LORA, D_HEADS, NOPE + ROPE_D),
                   ((0, 0), (0, 0), (0, 0), (0, slot_pad))).reshape(nl, Q_LORA, D_HEADS * MLA_SLOT).astype(BF16)
    wkv_b = mla_w_kv_b.astype(BF16)
    mla_qn = jnp.pad(row3(mla_qn_g), ((0, 0), (0, 0), (0, slot_pad)))
    mla_kn = jnp.pad(row3(mla_kn_g), ((0, 0), (0, 0), (0, slot_pad)))
    ffn_cb = row3(ffn_conv_b)

    nbp = x_prompt.shape[0]
    c_all = jnp.concatenate([c_prompt, c_sample], axis=0)
    mod = _ada_mod(c_all, ada_w, ada_b).reshape(nl, c_all.shape[0], 6, D_MODEL)

    fwd_b, inv_b = _dft_mats(HY_P)

    def trunk(x3, b0):
        bsz, seq, _ = x3.shape
        x = x3.reshape(bsz * seq, D_MODEL)
        gcos, gsin = _axial_tables(seq, C_HD // 2, C_HD)
        mcos, msin = _axial_tables(seq, ROPE_D // 2, ROPE_D)
        feats = _hyena_feats(seq)
        sub = ATTN_SCORE_ELEMS // seq
        for l in range(nl):
            mod_l = mod[l]
            proj = _in_proj(x, mod_l, b0, seq, norm1, w_in_b, l)
            ya = _gmlp(proj, gm_g, ws_b, gm_bias, l)
            kseq = _hy_filters(feats, *hy_filter_params, l, seq)
            hfilt = _hy_filter_dft(kseq, fwd_b, seq)
            yb = _hyena(proj, hy_cw, hy_cb, hy_skip, hfilt, fwd_b, inv_b, l, seq)
            gq, gk = _gqa_prep(proj, gcos, gsin, gqa_qg, gqa_kg, l, seq)
            yc = _attention(gq, gk, proj, GQA_V_COL // C_HD, seq, C_KV_HEADS, C_HEADS // C_KV_HEADS, C_HD, C_HD,
                            2 * sub, sub, "gqa_attn")
            mq, mk, mv = _mla_prep(proj, mcos, msin, mla_qag, mla_kvg, wq_b, wkv_b, mla_qn, mla_kn, l, seq)
            yd = _attention(mq, mk, mv, 0, seq, D_HEADS, 1, MLA_SLOT, V_HD, 4 * sub, sub, "mla_attn")
            x = _out_proj(ya, yb, yc, yd, x, mod_l, b0, seq, gg, w_out_b, l)
            x = _ffn(x, mod_l, b0, seq, norm2, w_up_b, ffn_conv_w, ffn_cb, w_down_b, l)
        return x.reshape(bsz, seq, D_MODEL)

    return trunk(x_prompt, 0), trunk(x_sample, nbp)
```

```python
import functools
import math

import jax
import jax.numpy as jnp
from jax import lax
from jax.experimental import pallas as pl
from jax.experimental.pallas import tpu as pltpu

F32 = jnp.float32
BF16 = jnp.bfloat16

D_MODEL = 2048
DEPTH = 4
GRID_W = 64
CHUNK = 128
W_GROUP = 512
A_HEADS = 4
POS_BANDS = 16
POS_EMB = 2 * POS_BANDS + 1
FILT_H = 64
C_HEADS = 4
C_KV_HEADS = 2
C_HD = 128
D_HEADS = 4
Q_LORA = 512
KV_LORA = 256
NOPE = 128
ROPE_D = 64
V_HD = 128
ROPE_THETA = 10000.0
D_FF = 5632
EPS = 1e-6
LOG2E = 1.4426950408889634
IN_A = 1024
IN_B = 1536
IN_C = 1024
IN_D = 832
IN_COLS = IN_A + IN_B + IN_C + IN_D
GQA_V_COL = IN_A + IN_B + (C_HEADS + C_KV_HEADS) * C_HD
IN_PAD = 4608

LANES = 128
SUBLANES = 8
BF16_ROWS = 16
NORM_ROWS = BF16_ROWS
VMEM_LIMIT = 56 * 1024 * 1024
FFN_VMEM_LIMIT = 60 * 1024 * 1024

ATTN_SCORE_ELEMS = 256 * 4096
HY_FILTER_ROWS = 512
HY_P = 512
FFN_HALO = BF16_ROWS
MLA_SLOT = 256


def _params(sem, vmem=VMEM_LIMIT):
    return pltpu.CompilerParams(dimension_semantics=sem, vmem_limit_bytes=vmem)


def _single(block_shape, index_map):
    return pl.BlockSpec(block_shape, index_map, pipeline_mode=pl.Buffered(1))


def _ada_kernel(c_ref, w_ref, b_ref, o_ref):
    s = jax.nn.silu(c_ref[...]).astype(BF16)
    o_ref[0] = jnp.dot(s, w_ref[0].astype(BF16), preferred_element_type=F32) + b_ref[0]


def _ada_mod(c_all, ada_w, ada_b):
    nb = c_all.shape[0]
    tn = 1024
    return pl.pallas_call(
        _ada_kernel,
        grid=(DEPTH, 6 * D_MODEL // tn),
        in_specs=[
            pl.BlockSpec((nb, D_MODEL), lambda l, j: (0, 0)),
            pl.BlockSpec((1, D_MODEL, tn), lambda l, j: (l, 0, j)),
            pl.BlockSpec((1, 1, tn), lambda l, j: (l, 0, j)),
        ],
        out_specs=pl.BlockSpec((1, nb, tn), lambda l, j: (l, 0, j)),
        out_shape=jax.ShapeDtypeStruct((DEPTH, nb, 6 * D_MODEL), F32),
        compiler_params=_params(("parallel", "parallel")),
        name="ada_mod",
    )(c_all, ada_w, ada_b.reshape(DEPTH, 1, 6 * D_MODEL))


def _mod_norm_rows(x, gs, shift):
    ms = jnp.mean(x * x, axis=-1, keepdims=True)
    return (x * lax.rsqrt(ms + EPS)) * gs + shift


def _inproj_kernel(xn_ref, x0_ref, modn_ref, mod0_ref, g_ref, w_ref, o_ref, ha_ref, hb_ref, *, tm):
    i = pl.program_id(0)
    rc = NORM_ROWS
    g = g_ref[...]

    def fill(x_ref, mod_ref, dst_ref):
        shift = mod_ref[0, 0:1, :]
        gs = g * (1.0 + mod_ref[0, 1:2, :])
        for c in range(tm // rc):
            rows = slice(c * rc, (c + 1) * rc)
            dst_ref[rows, :] = _mod_norm_rows(x_ref[rows, :], gs, shift).astype(BF16)

    @pl.when(i == 0)
    def _():
        fill(x0_ref, mod0_ref, ha_ref)

    def step(cur_ref, nxt_ref):
        fill(xn_ref, modn_ref, nxt_ref)
        o_ref[...] = jnp.dot(cur_ref[...], w_ref[...], preferred_element_type=F32).astype(o_ref.dtype)

    @pl.when(i % 2 == 0)
    def _():
        step(ha_ref, hb_ref)

    @pl.when(i % 2 == 1)
    def _():
        step(hb_ref, ha_ref)


def _in_proj(x, mod_l, b0, seq, g, w_in_b, layer):
    t = x.shape[0]
    tm = 512
    n = t // tm
    nxt = lambda i: jnp.minimum(i + 1, n - 1)
    return pl.pallas_call(
        functools.partial(_inproj_kernel, tm=tm),
        grid=(n,),
        in_specs=[
            pl.BlockSpec((tm, D_MODEL), lambda i: (nxt(i), 0)),
            _single((tm, D_MODEL), lambda i: (0, 0)),
            pl.BlockSpec((1, 6, D_MODEL), lambda i: (b0 + (nxt(i) * tm) // seq, 0, 0)),
            pl.BlockSpec((1, 6, D_MODEL), lambda i: (b0, 0, 0)),
            pl.BlockSpec((None, 1, D_MODEL), lambda i: (layer, 0, 0)),
            _single((None, D_MODEL, IN_PAD), lambda i: (layer, 0, 0)),
        ],
        out_specs=pl.BlockSpec((tm, IN_PAD), lambda i: (i, 0)),
        out_shape=jax.ShapeDtypeStruct((t, IN_PAD), BF16),
        scratch_shapes=[pltpu.VMEM((tm, D_MODEL), BF16), pltpu.VMEM((tm, D_MODEL), BF16)],
        compiler_params=_params(("arbitrary",)),
        name="in_proj",
    )(x, x, mod_l, mod_l, g, w_in_b)


def _gelu(x):
    return 0.5 * x * (1.0 + lax.erf(x * (1.0 / math.sqrt(2.0))))


def _gmlp_kernel(u_ref, v_ref, g_ref, ws_ref, bias_ref, o_ref, *, tm):
    g = g_ref[...]
    for n in range(tm // CHUNK):
        rows = slice(n * CHUNK, (n + 1) * CHUNK)
        u = _gelu(u_ref[rows, :].astype(F32))
        v = _gelu(v_ref[rows, :].astype(F32))
        vc = v - jnp.mean(v, axis=-1, keepdims=True)
        vn = (vc * lax.rsqrt(jnp.mean(vc * vc, axis=-1, keepdims=True) + EPS) * g).astype(BF16)
        for h in range(A_HEADS):
            cols = slice(h * LANES, (h + 1) * LANES)
            mixed = jnp.dot(ws_ref[h], vn[:, cols], preferred_element_type=F32) + bias_ref[:, cols]
            o_ref[rows, cols] = (u[:, cols] * mixed).astype(o_ref.dtype)


def _gmlp(proj, g, ws_b, bias_full, layer):
    t = proj.shape[0]
    tm = 512
    return pl.pallas_call(
        functools.partial(_gmlp_kernel, tm=tm),
        grid=(t // tm,),
        in_specs=[
            pl.BlockSpec((tm, W_GROUP), lambda i: (i, 0)),
            pl.BlockSpec((tm, W_GROUP), lambda i: (i, 1)),
            pl.BlockSpec((None, 1, W_GROUP), lambda i: (layer, 0, 0)),
            pl.BlockSpec((None, A_HEADS, CHUNK, CHUNK), lambda i: (layer, 0, 0, 0)),
            pl.BlockSpec((None, CHUNK, W_GROUP), lambda i: (layer, 0, 0)),
        ],
        out_specs=pl.BlockSpec((tm, W_GROUP), lambda i: (i, 0)),
        out_shape=jax.ShapeDtypeStruct((t, W_GROUP), BF16),
        compiler_params=_params(("parallel",)),
        name="gmlp",
    )(proj, proj, g, ws_b, bias_full)


def _hy_filter_kernel(f_ref, w1_ref, b1_ref, f1_ref, w2_ref, b2_ref, f2_ref, w3_ref, dec_ref, o_ref):
    hp = lax.Precision.HIGHEST
    feats = f_ref[...]
    h = jnp.sin(f1_ref[...] * (jnp.dot(feats, w1_ref[...], precision=hp, preferred_element_type=F32) + b1_ref[...]))
    h = jnp.sin(f2_ref[...] * (jnp.dot(h, w2_ref[...], precision=hp, preferred_element_type=F32) + b2_ref[...]))
    half = feats.shape[0]
    decay = jnp.abs(dec_ref[0])
    for grp in range(2):
        k = jnp.dot(h, w3_ref[0, grp], precision=hp, preferred_element_type=F32)
        tcol = feats[:, grp * FILT_H:grp * FILT_H + 1]
        valid = feats[:, (grp + 1) * FILT_H - 1:(grp + 1) * FILT_H]
        k = k * jnp.exp(-tcol * decay) * valid
        o_ref[0, grp * half:(grp + 1) * half, :] = k[:, :W_GROUP]
        o_ref[1, grp * half:(grp + 1) * half, :] = k[:, W_GROUP:]


def _hy_filters(feats, w1p, b1p, f1p, w2p, b2p, f2p, w3d, decd, layer, seq):
    rt = HY_FILTER_ROWS
    half = seq // rt
    c2 = lambda i: (layer, 0, 0)
    return pl.pallas_call(
        _hy_filter_kernel,
        grid=(2 * seq // rt,),
        in_specs=[
            pl.BlockSpec((rt // 2, LANES), lambda i: (i, 0)),
            pl.BlockSpec((None, LANES, LANES), c2),
            pl.BlockSpec((None, 1, LANES), c2),
            pl.BlockSpec((None, 1, LANES), c2),
            pl.BlockSpec((None, LANES, LANES), c2),
            pl.BlockSpec((None, 1, LANES), c2),
            pl.BlockSpec((None, 1, LANES), c2),
            pl.BlockSpec((None, 1, 2, LANES, 2 * W_GROUP), lambda i: (layer, jnp.where(i < half, 1, 0), 0, 0, 0)),
            pl.BlockSpec((None, 1, 1, 2 * W_GROUP), lambda i: (layer, jnp.where(i < half, 1, 0), 0, 0)),
        ],
        out_specs=pl.BlockSpec((2, rt, W_GROUP), lambda i: (0, i, 0)),
        out_shape=jax.ShapeDtypeStruct((2, 2 * seq, W_GROUP), F32),
        compiler_params=_params(("parallel",)),
        name="hy_filter",
    )(feats, w1p, b1p, f1p, w2p, b2p, f2p, w3d, decd)


def _hy_fdft_kernel(k_ref, fwd_ref, o_ref, prev_ref):
    e = pl.program_id(1)
    cur = jnp.dot(fwd_ref[...], k_ref[0].astype(BF16), preferred_element_type=F32)

    @pl.when(e > 0)
    def _():
        prev = prev_ref[...]
        odd = (lax.broadcasted_iota(jnp.int32, prev.shape, 0) & 1) == 1
        o_ref[0, 0] = cur + jnp.where(odd, -prev, prev)

    prev_ref[...] = cur


def _hy_filter_dft(kseq, fwd_b, seq):
    p = HY_P
    nb = seq // p
    return pl.pallas_call(
        _hy_fdft_kernel,
        grid=(2, 2 * nb),
        in_specs=[
            pl.BlockSpec((1, p, W_GROUP), lambda o, e: (o, e, 0)),
            pl.BlockSpec((2 * p, p), lambda o, e: (0, 0)),
        ],
        out_specs=pl.BlockSpec((1, 1, 2 * p, W_GROUP), lambda o, e: (o, jnp.maximum(e - 1, 0), 0, 0)),
        out_shape=jax.ShapeDtypeStruct((2, 2 * nb - 1, 2 * p, W_GROUP), F32),
        scratch_shapes=[pltpu.VMEM((2 * p, W_GROUP), F32)],
        compiler_params=_params(("arbitrary", "arbitrary")),
        name="hy_filter_dft",
    )(kseq, fwd_b)


def _dwconv_rows(src_ref, r0, rc, total, w, b):
    halo = BF16_ROWS
    xc = src_ref[r0:r0 + rc, :].astype(F32)
    zeros = jnp.zeros((halo, xc.shape[1]), F32)
    prev = src_ref[r0 - halo:r0, :].astype(F32) if r0 > 0 else zeros
    nxt = src_ref[r0 + rc:r0 + rc + halo, :].astype(F32) if r0 + rc < total else zeros
    cat = jnp.concatenate([prev, xc, nxt], axis=0)
    n = rc + 2 * halo
    dn = pltpu.roll(cat, 1, 0)[halo:halo + rc]
    up = pltpu.roll(cat, n - 1, 0)[halo:halo + rc]
    return dn * w[0:1] + xc * w[1:2] + up * w[2:3] + b


def _hyena_kernel(p1_ref, p2_ref, pv_ref, cw_ref, cb_ref, skip_ref, h_ref, fwd_ref, inv_ref, o_ref,
                  vz, x1z, x2z, vhat0, vhat1, yhat_a, yhat_b, *, seq, p):
    nb = seq // p
    rc = 256
    rm = 32
    yhat = (yhat_a, yhat_b)

    def conv_block(src_ref, dst, part, i):
        for r0 in range(i * p, (i + 1) * p, rc):
            dst[r0:r0 + rc, :] = _dwconv_rows(src_ref, r0, rc, seq, cw_ref[part], cb_ref[part:part + 1, :])

    for i in range(nb):
        conv_block(pv_ref, vz, 2, i)
        vhat0[i] = jnp.dot(fwd_ref[...], vz[i * p:(i + 1) * p, :].astype(BF16), preferred_element_type=F32)
        conv_block(p1_ref, x1z, 0, i)
        conv_block(p2_ref, x2z, 1, i)

    def block_spectrum(order, i, vhat, dst):
        for r in range(p // rm):
            rows_re = slice(r * rm, (r + 1) * rm)
            rows_im = slice(p + r * rm, p + (r + 1) * rm)
            acc_r = None
            acc_i = None
            for ip in range(nb):
                d = i - ip + (nb - 1)
                hr = h_ref[order, d, rows_re, :]
                hi = h_ref[order, d, rows_im, :]
                ur = vhat[ip, rows_re, :]
                ui = vhat[ip, rows_im, :]
                if r == 0:
                    first = lax.broadcasted_iota(jnp.int32, hr.shape, 0) == 0
                    tr = hr * ur - jnp.where(first, 0.0, hi * ui)
                    ti = jnp.where(first, hi * ui, hr * ui + hi * ur)
                else:
                    tr = hr * ur - hi * ui
                    ti = hr * ui + hi * ur
                acc_r = tr if acc_r is None else acc_r + tr
                acc_i = ti if acc_i is None else acc_i + ti
            dst[rows_re, :] = acc_r.astype(BF16)
            dst[rows_im, :] = acc_i.astype(BF16)

    for order in range(2):
        gate = x1z if order == 0 else x2z
        vhat = vhat0 if order == 0 else vhat1
        skip = skip_ref[order:order + 1, :]
        for i in range(nb + 1):
            if i >= 1:
                y = jnp.dot(inv_ref[...], yhat[(i - 1) % 2][...], preferred_element_type=F32)
            if i < nb:
                block_spectrum(order, i, vhat, yhat[i % 2])
            if i >= 1:
                rows = slice((i - 1) * p, i * p)
                res = gate[rows, :] * (y + skip * vz[rows, :])
                if order == 0:
                    vz[rows, :] = res
                    vhat1[i - 1] = jnp.dot(fwd_ref[...], res.astype(BF16), preferred_element_type=F32)
                else:
                    o_ref[rows, :] = res.astype(o_ref.dtype)


def _hyena(proj, cw, cb, skip, hfilt, fwd_b, inv_b, layer, seq):
    t = proj.shape[0]
    p = HY_P
    nb = seq // p
    nct = W_GROUP // LANES
    base = IN_A // LANES
    return pl.pallas_call(
        functools.partial(_hyena_kernel, seq=seq, p=p),
        grid=(nct, t // seq),
        in_specs=[
            pl.BlockSpec((seq, LANES), lambda c, b: (b, base + c)),
            pl.BlockSpec((seq, LANES), lambda c, b: (b, base + nct + c)),
            pl.BlockSpec((seq, LANES), lambda c, b: (b, base + 2 * nct + c)),
            pl.BlockSpec((None, 3, 3, LANES), lambda c, b: (layer, 0, 0, c)),
            pl.BlockSpec((None, 3, LANES), lambda c, b: (layer, 0, c)),
            pl.BlockSpec((None, 2, LANES), lambda c, b: (layer, 0, c)),
            _single((2, 2 * nb - 1, 2 * p, LANES), lambda c, b: (0, 0, 0, c)),
            _single((2 * p, p), lambda c, b: (0, 0)),
            _single((p, 2 * p), lambda c, b: (0, 0)),
        ],
        out_specs=pl.BlockSpec((seq, LANES), lambda c, b: (b, c)),
        out_shape=jax.ShapeDtypeStruct((t, W_GROUP), BF16),
        scratch_shapes=[
            pltpu.VMEM((seq, LANES), F32),
            pltpu.VMEM((seq, LANES), F32),
            pltpu.VMEM((seq, LANES), F32),
            pltpu.VMEM((nb, 2 * p, LANES), F32),
            pltpu.VMEM((nb, 2 * p, LANES), F32),
            pltpu.VMEM((2 * p, LANES), BF16),
            pltpu.VMEM((2 * p, LANES), BF16),
        ],
        compiler_params=_params(("parallel", "parallel")),
        name="hyena",
    )(proj, proj, proj, cw, cb, skip, hfilt, fwd_b, inv_b)


def _rope_lanes(x, cos_t, sin_t, half):
    lane = lax.broadcasted_iota(jnp.int32, x.shape, 1)
    lower = (lane % (2 * half)) < half
    partner = jnp.where(lower, pltpu.roll(x, LANES - half, 1), pltpu.roll(x, half, 1))
    return x * cos_t + partner * sin_t


def _rms_rows(x, g, width):
    ss = jnp.sum(x * x, axis=-1, keepdims=True) * (1.0 / width)
    return x * lax.rsqrt(ss + EPS) * g


def _gqa_prep_kernel(q_ref, k_ref, cos_ref, sin_ref, qg_ref, kg_ref, qo_ref, ko_ref):
    cos_t = cos_ref[...]
    sin_t = sin_ref[...]
    scale = LOG2E / math.sqrt(C_HD)
    for h in range(C_HEADS):
        cols = slice(h * C_HD, (h + 1) * C_HD)
        q = _rope_lanes(_rms_rows(q_ref[:, cols].astype(F32), qg_ref[...], C_HD), cos_t, sin_t, C_HD // 4)
        qo_ref[:, cols] = (q * scale).astype(BF16)
    for h in range(C_KV_HEADS):
        cols = slice(h * C_HD, (h + 1) * C_HD)
        k = _rope_lanes(_rms_rows(k_ref[:, cols].astype(F32), kg_ref[...], C_HD), cos_t, sin_t, C_HD // 4)
        ko_ref[:, cols] = k.astype(BF16)


def _gqa_prep(proj, cos_t, sin_t, qg, kg, layer, seq):
    t = proj.shape[0]
    tm = 512
    qw = C_HEADS * C_HD
    kw = C_KV_HEADS * C_HD
    c0 = IN_A + IN_B
    g2 = lambda i: (layer, 0, 0)
    return pl.pallas_call(
        _gqa_prep_kernel,
        grid=(t // tm,),
        in_specs=[
            pl.BlockSpec((tm, qw), lambda i: (i, c0 // qw)),
            pl.BlockSpec((tm, kw), lambda i: (i, (c0 + qw) // kw)),
            pl.BlockSpec((tm, LANES), lambda i: (i % (seq // tm), 0)),
            pl.BlockSpec((tm, LANES), lambda i: (i % (seq // tm), 0)),
            pl.BlockSpec((None, 1, C_HD), g2),
            pl.BlockSpec((None, 1, C_HD), g2),
        ],
        out_specs=[
            pl.BlockSpec((tm, qw), lambda i: (i, 0)),
            pl.BlockSpec((tm, kw), lambda i: (i, 0)),
        ],
        out_shape=[
            jax.ShapeDtypeStruct((t, qw), BF16),
            jax.ShapeDtypeStruct((t, kw), BF16),
        ],
        compiler_params=_params(("parallel",)),
        name="gqa_prep",
    )(proj, proj, cos_t, sin_t, qg, kg)


def _mla_prep_kernel(qa_ref, ckv_ref, kr_ref, cos_ref, sin_ref, qag_ref, kvg_ref, wq_ref, wkv_ref,
                     qn_ref, kn_ref, qo_ref, ko_ref, vo_ref):
    cos_t = cos_ref[...]
    sin_t = sin_ref[...]
    width = NOPE + ROPE_D
    scale = LOG2E / math.sqrt(width)
    qa = _rms_rows(qa_ref[...].astype(F32), qag_ref[...], Q_LORA).astype(BF16)
    q = jnp.dot(qa, wq_ref[...], preferred_element_type=F32)
    ckv = _rms_rows(ckv_ref[...].astype(F32), kvg_ref[...], KV_LORA).astype(BF16)
    kv = jnp.dot(ckv, wkv_ref[...], preferred_element_type=F32)
    kr = kr_ref[...].astype(F32)
    kr_ss = jnp.sum(kr * kr, axis=-1, keepdims=True)
    qn = qn_ref[...]
    kn = kn_ref[...]
    for h in range(D_HEADS):
        base = h * MLA_SLOT
        qh = _rms_rows(q[:, base:base + MLA_SLOT], qn, width)
        qo_ref[:, base:base + NOPE] = (qh[:, :NOPE] * scale).astype(BF16)
        qr = _rope_lanes(qh[:, NOPE:], cos_t, sin_t, ROPE_D // 4)
        qo_ref[:, base + NOPE:base + MLA_SLOT] = (qr * scale).astype(BF16)
        k_nope = kv[:, base:base + NOPE]
        ss = (jnp.sum(k_nope * k_nope, axis=-1, keepdims=True) + kr_ss) * (1.0 / width)
        inv = lax.rsqrt(ss + EPS)
        ko_ref[:, base:base + NOPE] = (k_nope * inv * kn[:, :NOPE]).astype(BF16)
        krn = _rope_lanes(kr * inv * kn[:, NOPE:], cos_t, sin_t, ROPE_D // 4)
        ko_ref[:, base + NOPE:base + MLA_SLOT] = krn.astype(BF16)
        vo_ref[:, h * V_HD:(h + 1) * V_HD] = kv[:, base + NOPE:base + NOPE + V_HD].astype(BF16)


def _mla_prep(proj, cos_t, sin_t, qag, kvg, wq_b, wkv_b, qn_p, kn_p, layer, seq):
    t = proj.shape[0]
    tm = 512
    c0 = IN_A + IN_B + IN_C
    g2 = lambda i: (layer, 0, 0)
    hw = D_HEADS * MLA_SLOT
    return pl.pallas_call(
        _mla_prep_kernel,
        grid=(t // tm,),
        in_specs=[
            pl.BlockSpec((tm, Q_LORA), lambda i: (i, c0 // Q_LORA)),
            pl.BlockSpec((tm, KV_LORA), lambda i: (i, (c0 + Q_LORA) // KV_LORA)),
            pl.BlockSpec((tm, LANES), lambda i: (i, (c0 + Q_LORA + KV_LORA) // LANES)),
            pl.BlockSpec((tm, LANES), lambda i: (i % (seq // tm), 0)),
            pl.BlockSpec((tm, LANES), lambda i: (i % (seq // tm), 0)),
            pl.BlockSpec((None, 1, Q_LORA), g2),
            pl.BlockSpec((None, 1, KV_LORA), g2),
            pl.BlockSpec((None, Q_LORA, hw), g2),
            pl.BlockSpec((None, KV_LORA, hw), g2),
            pl.BlockSpec((None, 1, MLA_SLOT), g2),
            pl.BlockSpec((None, 1, MLA_SLOT), g2),
        ],
        out_specs=[
            pl.BlockSpec((tm, hw), lambda i: (i, 0)),
            pl.BlockSpec((tm, hw), lambda i: (i, 0)),
            pl.BlockSpec((tm, D_HEADS * V_HD), lambda i: (i, 0)),
        ],
        out_shape=[
            jax.ShapeDtypeStruct((t, hw), BF16),
            jax.ShapeDtypeStruct((t, hw), BF16),
            jax.ShapeDtypeStruct((t, D_HEADS * V_HD), BF16),
        ],
        compiler_params=_params(("parallel",)),
        name="mla_prep",
    )(proj, proj, proj, cos_t, sin_t, qag, kvg, wq_b, wkv_b, qn_p, kn_p)


def _attn_kernel(q_ref, k_ref, v_ref, o_ref, vext_ref, *, groups, dk, dv, sub):
    @pl.when(pl.program_id(2) == 0)
    def _():
        vext_ref[:, :dv] = v_ref[...]
        vext_ref[:, dv:] = jnp.ones((v_ref.shape[0], dv), BF16)

    k = k_ref[...]
    v = vext_ref[...]
    tq = q_ref.shape[0]
    chains = [(g, slice(r * sub, (r + 1) * sub)) for g in range(groups) for r in range(tq // sub)]

    def scores(c):
        g, rows = chains[c]
        q = q_ref[rows, g * dk:(g + 1) * dk]
        return lax.dot_general(q, k, (((1,), (1,)), ((), ())), preferred_element_type=F32)

    s = scores(0)
    for c, (g, rows) in enumerate(chains):
        s_next = scores(c + 1) if c + 1 < len(chains) else None
        m = jnp.max(s, axis=-1, keepdims=True)
        pexp = jnp.exp2(s - m).astype(BF16)
        o = jnp.dot(pexp, v, preferred_element_type=F32)
        o_ref[rows, g * dv:(g + 1) * dv] = (o[:, :dv] / o[:, dv:]).astype(o_ref.dtype)
        s = s_next


def _attention(q, k, v, v_col0, seq, kv_heads, groups, dk, dv, tq, sub, name):
    t = q.shape[0]
    nq = seq // tq
    return pl.pallas_call(
        functools.partial(_attn_kernel, groups=groups, dk=dk, dv=dv, sub=sub),
        grid=(t // seq, kv_heads, nq),
        in_specs=[
            pl.BlockSpec((tq, groups * dk), lambda b, h, i: (b * nq + i, h)),
            pl.BlockSpec((seq, dk), lambda b, h, i: (b, h)),
            pl.BlockSpec((seq, dv), lambda b, h, i: (b, v_col0 + h)),
        ],
        out_specs=pl.BlockSpec((tq, groups * dv), lambda b, h, i: (b * nq + i, h)),
        out_shape=jax.ShapeDtypeStruct((t, kv_heads * groups * dv), BF16),
        scratch_shapes=[pltpu.VMEM((seq, 2 * dv), BF16)],
        compiler_params=_params(("parallel", "parallel", "arbitrary")),
        name=name,
    )(q, k, v)


def _outproj_kernel(ya_ref, yb_ref, yc_ref, yd_ref, x_ref, mod_ref, gg_ref, w_ref, o_ref):
    acc = None
    for gi, y_ref in enumerate((ya_ref, yb_ref, yc_ref, yd_ref)):
        rows = slice(gi * W_GROUP, (gi + 1) * W_GROUP)
        yn = _rms_rows(y_ref[...].astype(F32), gg_ref[:, rows], W_GROUP).astype(BF16)
        part = jnp.dot(yn, w_ref[rows, :], preferred_element_type=F32)
        acc = part if acc is None else acc + part
    o_ref[...] = x_ref[...] + mod_ref[0, 2:3, :] * acc


def _out_proj(ya, yb, yc, yd, x, mod_l, b0, seq, gg, w_out_b, layer):
    t = x.shape[0]
    tm = 512
    yspec = pl.BlockSpec((tm, W_GROUP), lambda i: (i, 0))
    return pl.pallas_call(
        _outproj_kernel,
        grid=(t // tm,),
        in_specs=[
            yspec, yspec, yspec, yspec,
            pl.BlockSpec((tm, D_MODEL), lambda i: (i, 0)),
            pl.BlockSpec((1, 6, D_MODEL), lambda i: (b0 + (i * tm) // seq, 0, 0)),
            pl.BlockSpec((None, 1, D_MODEL), lambda i: (layer, 0, 0)),
            _single((None, D_MODEL, D_MODEL), lambda i: (layer, 0, 0)),
        ],
        out_specs=pl.BlockSpec((tm, D_MODEL), lambda i: (i, 0)),
        out_shape=jax.ShapeDtypeStruct((t, D_MODEL), F32),
        compiler_params=_params(("parallel",)),
        name="out_proj",
    )(ya, yb, yc, yd, x, mod_l, gg, w_out_b)


def _ffn_kernel(x_ref, xp_ref, xn_ref, mod_ref, g_ref, wg_ref, wu_ref, cwg_ref, cwu_ref, cbg_ref, cbu_ref,
                wd_ref, o_ref, h_ref, *, tm, seq):
    i = pl.program_id(0)
    j = pl.program_id(1)
    nj = pl.num_programs(1)
    rc = NORM_ROWS
    halo = FFN_HALO

    @pl.when(j == 0)
    def _():
        shift = mod_ref[0, 3:4, :]
        gs = g_ref[...] * (1.0 + mod_ref[0, 4:5, :])
        has_prev = ((i * tm) % seq) != 0
        has_next = (((i + 1) * tm) % seq) != 0
        hp = _mod_norm_rows(xp_ref[...], gs, shift)
        hn = _mod_norm_rows(xn_ref[...], gs, shift)
        h_ref[pl.ds(0, halo), :] = jnp.where(has_prev, hp, 0.0).astype(BF16)
        h_ref[pl.ds(halo + tm, halo), :] = jnp.where(has_next, hn, 0.0).astype(BF16)

        chains = 8

        def body(c, carry):
            for k in range(chains):
                r0 = pl.multiple_of((c * chains + k) * rc, rc)
                h_ref[pl.ds(halo + r0, rc), :] = _mod_norm_rows(x_ref[pl.ds(r0, rc), :], gs, shift).astype(BF16)
            return carry

        lax.fori_loop(0, tm // (rc * chains), body, 0)
        o_ref[...] = jnp.zeros(o_ref.shape, F32)

    hfull = h_ref[...]
    n = tm + 2 * halo

    def conv(u, cw_ref, cb_ref):
        dn = pltpu.roll(u, 1, 0)[halo:halo + tm]
        up = pltpu.roll(u, n - 1, 0)[halo:halo + tm]
        return dn * cw_ref[0:1, :] + u[halo:halo + tm] * cw_ref[1:2, :] + up * cw_ref[2:3, :] + cb_ref[...]

    gate = conv(jnp.dot(hfull, wg_ref[...], preferred_element_type=F32), cwg_ref, cbg_ref)
    upv = conv(jnp.dot(hfull, wu_ref[...], preferred_element_type=F32), cwu_ref, cbu_ref)
    act = (jax.nn.silu(gate) * upv).astype(BF16)
    o_ref[...] += jnp.dot(act, wd_ref[...], preferred_element_type=F32)

    @pl.when(j == nj - 1)
    def _():
        o_ref[...] = x_ref[...] + mod_ref[0, 5:6, :] * o_ref[...]


def _ffn(x, mod_l, b0, seq, g, w_up_b, cw, cb, w_down_b, layer):
    t = x.shape[0]
    tm, tf = 1024, 512
    nf = D_FF // tf
    hb = tm // FFN_HALO
    last = t // FFN_HALO - 1
    return pl.pallas_call(
        functools.partial(_ffn_kernel, tm=tm, seq=seq),
        grid=(t // tm, nf),
        in_specs=[
            pl.BlockSpec((tm, D_MODEL), lambda i, j: (i, 0)),
            pl.BlockSpec((FFN_HALO, D_MODEL), lambda i, j: (jnp.maximum(i * hb - 1, 0), 0)),
            pl.BlockSpec((FFN_HALO, D_MODEL), lambda i, j: (jnp.minimum((i + 1) * hb, last), 0)),
            pl.BlockSpec((1, 6, D_MODEL), lambda i, j: (b0 + (i * tm) // seq, 0, 0)),
            pl.BlockSpec((None, 1, D_MODEL), lambda i, j: (layer, 0, 0)),
            pl.BlockSpec((None, D_MODEL, tf), lambda i, j: (layer, 0, j)),
            pl.BlockSpec((None, D_MODEL, tf), lambda i, j: (layer, 0, nf + j)),
            pl.BlockSpec((None, 3, tf), lambda i, j: (layer, 0, j)),
            pl.BlockSpec((None, 3, tf), lambda i, j: (layer, 0, nf + j)),
            pl.BlockSpec((None, 1, tf), lambda i, j: (layer, 0, j)),
            pl.BlockSpec((None, 1, tf), lambda i, j: (layer, 0, nf + j)),
            pl.BlockSpec((None, tf, D_MODEL), lambda i, j: (layer, j, 0)),
        ],
        out_specs=pl.BlockSpec((tm, D_MODEL), lambda i, j: (i, 0)),
        out_shape=jax.ShapeDtypeStruct((t, D_MODEL), F32),
        scratch_shapes=[pltpu.VMEM((tm + 2 * FFN_HALO, D_MODEL), BF16)],
        compiler_params=_params(("parallel", "arbitrary"), FFN_VMEM_LIMIT),
        name="ffn",
    )(x, x, x, mod_l, g, w_up_b, w_up_b, cw, cw, cb, cb, w_down_b)


def _axial_tables(seq, sec, lanes_used):
    pos = jnp.arange(seq, dtype=jnp.int32)
    row = (pos // GRID_W).astype(F32)
    col = (pos % GRID_W).astype(F32)
    inv = ROPE_THETA ** (-jnp.arange(0, sec, 2, dtype=F32) / sec)
    half = sec // 2
    lane = jnp.arange(LANES)
    in_use = lane < lanes_used
    which = (lane // sec) % 2
    freq = inv[lane % half]
    ang = jnp.where(which[None, :] == 0, row[:, None], col[:, None]) * freq[None, :]
    cos_t = jnp.where(in_use[None, :], jnp.cos(ang), 1.0)
    sign = jnp.where((lane % sec) < half, -1.0, 1.0)
    sin_t = jnp.where(in_use[None, :], jnp.sin(ang) * sign[None, :], 0.0)
    return cos_t.astype(F32), sin_t.astype(F32)


def _hy_filter_params(hy_w1, hy_b1, hy_f1, hy_w2, hy_b2, hy_f2, hy_w3, hy_decay):
    nl = hy_w1.shape[0]
    zero = jnp.zeros((nl, FILT_H, FILT_H), F32)
    bdiag = lambda w: jnp.concatenate([jnp.concatenate([w, zero], axis=2), jnp.concatenate([zero, w], axis=2)], axis=1)
    twice = lambda a: jnp.tile(a.reshape(nl, 1, FILT_H), (1, 1, 2))
    w1p = bdiag(jnp.pad(hy_w1, ((0, 0), (0, FILT_H - POS_EMB), (0, 0))))
    w2p = bdiag(hy_w2)
    w3 = hy_w3.reshape(nl, FILT_H, 2, 2, W_GROUP).transpose(0, 3, 1, 2, 4).reshape(nl, 2, FILT_H, 2 * W_GROUP)
    z3 = jnp.zeros_like(w3)
    w3d = jnp.stack([jnp.concatenate([w3, z3], axis=2), jnp.concatenate([z3, w3], axis=2)], axis=2)
    decd = hy_decay.reshape(nl, 2, 2, W_GROUP).transpose(0, 2, 1, 3).reshape(nl, 2, 1, 2 * W_GROUP)
    return w1p, twice(hy_b1), twice(hy_f1), w2p, twice(hy_b2), twice(hy_f2), w3d, decd


def _hyena_feats(seq):
    n = jnp.arange(seq, dtype=jnp.int32)
    posi = jnp.concatenate([seq - n, n])
    valid = (posi < seq).astype(F32)
    posi = jnp.minimum(posi, seq - 1)
    tl = jnp.linspace(0.0, 1.0, seq, dtype=F32)
    tlin = jnp.concatenate([tl[seq - 1:], tl[:0:-1], tl])
    bands = jnp.linspace(1e-4, POS_BANDS - 1, POS_BANDS, dtype=F32)
    ang = (2.0 * math.pi / seq) * posi.astype(F32)[:, None] * bands[None, :]
    feats = jnp.concatenate([tlin[:, None], jnp.cos(ang), -jnp.sin(ang)], axis=-1)
    pad = jnp.zeros((2 * seq, FILT_H - POS_EMB - 1), F32)
    f64 = jnp.concatenate([feats, pad, valid[:, None]], axis=-1)
    half = HY_FILTER_ROWS // 2
    return f64.reshape(-1, 2, half, FILT_H).transpose(0, 2, 1, 3).reshape(seq, 2 * FILT_H)


def _dft_mats(p):
    r = jnp.arange(p, dtype=jnp.int32)
    q = (r[:, None] * r[None, :]) % (2 * p)
    ang = q.astype(F32) * (math.pi / p)
    cosm = jnp.cos(ang)
    sinm = jnp.sin(ang)
    alt = jnp.where(r % 2 == 0, 1.0, -1.0).astype(F32)
    im_rows = jnp.where((r == 0)[:, None], alt[None, :], -sinm)
    fwd = jnp.concatenate([cosm, im_rows], axis=0)
    wre = jnp.where((r == 0)[None, :], 0.5, 1.0) / p
    inv_re = cosm * wre
    inv_im = jnp.where((r == 0)[None, :], alt[:, None] * (0.5 / p), -sinm / p)
    inv = jnp.concatenate([inv_re, inv_im], axis=1)
    return fwd.astype(BF16), inv.astype(BF16)


def kernel(x_prompt, x_sample, c_prompt, c_sample, ada_w, ada_b, norm1_g, w_in, gm_vnorm_g, gm_spatial_w, gm_spatial_b, hy_conv_w, hy_conv_b, hy_w1, hy_b1, hy_f1, hy_w2, hy_b2, hy_f2, hy_w3, hy_decay, hy_skip, gqa_qn_g, gqa_kn_g, mla_q_a_g, mla_w_q_b, mla_kv_a_g, mla_w_kv_b, mla_qn_g, mla_kn_g, group_norm_g, w_out, norm2_g, ffn_w_up, ffn_conv_w, ffn_conv_b, ffn_w_down):
    nl = DEPTH
    w_in_b = jnp.pad(w_in.astype(BF16), ((0, 0), (0, 0), (0, IN_PAD - IN_COLS)))
    w_out_b = w_out.astype(BF16)
    w_up_b = ffn_w_up.astype(BF16)
    w_down_b = ffn_w_down.astype(BF16)
    ws_b = gm_spatial_w.astype(BF16)
    gm_bias = jnp.broadcast_to(jnp.swapaxes(gm_spatial_b, 1, 2)[:, :, :, None],
                               (nl, CHUNK, A_HEADS, LANES)).reshape(nl, CHUNK, W_GROUP)
    row3 = lambda a: a.reshape(nl, 1, a.shape[-1])
    norm1 = row3(norm1_g)
    norm2 = row3(norm2_g)
    gm_g = row3(gm_vnorm_g)
    gg = row3(group_norm_g)
    hy_cw = hy_conv_w.reshape(nl, 3, 3, W_GROUP).transpose(0, 2, 1, 3)
    hy_cb = hy_conv_b.reshape(nl, 3, W_GROUP)
    hy_filter_params = _hy_filter_params(hy_w1, hy_b1, hy_f1, hy_w2, hy_b2, hy_f2, hy_w3, hy_decay)
    gqa_qg = row3(gqa_qn_g)
    gqa_kg = row3(gqa_kn_g)
    mla_qag = row3(mla_q_a_g)
    mla_kvg = row3(mla_kv_a_g)
    slot_pad = MLA_SLOT - NOPE - ROPE_D
    wq_b = jnp.pad(mla_w_q_b.reshape(nl, Q_LORA, D_HEADS, NOPE + ROPE_D),
                   ((0, 0), (0, 0), (0, 0), (0, slot_pad))).reshape(nl, Q_LORA, D_HEADS * MLA_SLOT).astype(BF16)
    wkv_b = mla_w_kv_b.astype(BF16)
    mla_qn = jnp.pad(row3(mla_qn_g), ((0, 0), (0, 0), (0, slot_pad)))
    mla_kn = jnp.pad(row3(mla_kn_g), ((0, 0), (0, 0), (0, slot_pad)))
    ffn_cb = row3(ffn_conv_b)

    nbp = x_prompt.shape[0]
    c_all = jnp.concatenate([c_prompt, c_sample], axis=0)
    mod = _ada_mod(c_all, ada_w, ada_b).reshape(nl, c_all.shape[0], 6, D_MODEL)

    fwd_b, inv_b = _dft_mats(HY_P)

    def trunk(x3, b0):
        bsz, seq, _ = x3.shape
        x = x3.reshape(bsz * seq, D_MODEL)
        gcos, gsin = _axial_tables(seq, C_HD // 2, C_HD)
        mcos, msin = _axial_tables(seq, ROPE_D // 2, ROPE_D)
        feats = _hyena_feats(seq)
        sub = ATTN_SCORE_ELEMS // seq
        for l in range(nl):
            mod_l = mod[l]
            proj = _in_proj(x, mod_l, b0, seq, norm1, w_in_b, l)
            ya = _gmlp(proj, gm_g, ws_b, gm_bias, l)
            kseq = _hy_filters(feats, *hy_filter_params, l, seq)
            hfilt = _hy_filter_dft(kseq, fwd_b, seq)
            yb = _hyena(proj, hy_cw, hy_cb, hy_skip, hfilt, fwd_b, inv_b, l, seq)
            gq, gk = _gqa_prep(proj, gcos, gsin, gqa_qg, gqa_kg, l, seq)
            yc = _attention(gq, gk, proj, GQA_V_COL // C_HD, seq, C_KV_HEADS, C_HEADS // C_KV_HEADS, C_HD, C_HD,
                            2 * sub, sub, "gqa_attn")
            mq, mk, mv = _mla_prep(proj, mcos, msin, mla_qag, mla_kvg, wq_b, wkv_b, mla_qn, mla_kn, l, seq)
            yd = _attention(mq, mk, mv, 0, seq, D_HEADS, 1, MLA_SLOT, V_HD, 4 * sub, sub, "mla_attn")
            x = _out_proj(ya, yb, yc, yd, x, mod_l, b0, seq, gg, w_out_b, l)
            x = _ffn(x, mod_l, b0, seq, norm2, w_up_b, ffn_conv_w, ffn_cb, w_down_b, l)
        return x.reshape(bsz, seq, D_MODEL)

    return trunk(x_prompt, 0), trunk(x_sample, nbp)
```

```python
import functools
import math

import jax
import jax.numpy as jnp
from jax import lax
from jax.experimental import pallas as pl
from jax.experimental.pallas import tpu as pltpu

F32 = jnp.float32
BF16 = jnp.bfloat16

D_MODEL = 2048
DEPTH = 4
GRID_W = 64
CHUNK = 128
W_GROUP = 512
A_HEADS = 4
POS_BANDS = 16
POS_EMB = 2 * POS_BANDS + 1
FILT_H = 64
C_HEADS = 4
C_KV_HEADS = 2
C_HD = 128
D_HEADS = 4
Q_LORA = 512
KV_LORA = 256
NOPE = 128
ROPE_D = 64
V_HD = 128
ROPE_THETA = 10000.0
D_FF = 5632
EPS = 1e-6
LOG2E = 1.4426950408889634
IN_A = 1024
IN_B = 1536
IN_C = 1024
IN_D = 832
IN_COLS = IN_A + IN_B + IN_C + IN_D
GQA_V_COL = IN_A + IN_B + (C_HEADS + C_KV_HEADS) * C_HD
IN_PAD = 4608

LANES = 128
BF16_ROWS = 16
NORM_ROWS = BF16_ROWS
VMEM_LIMIT = 56 * 1024 * 1024
FFN_VMEM_LIMIT = 60 * 1024 * 1024

ATTN_SCORE_ELEMS = 256 * 4096
HY_P = 512
FFN_HALO = BF16_ROWS
MLA_SLOT = 256


def _params(sem, vmem=VMEM_LIMIT):
    return pltpu.CompilerParams(dimension_semantics=sem, vmem_limit_bytes=vmem)


def _single(block_shape, index_map):
    return pl.BlockSpec(block_shape, index_map, pipeline_mode=pl.Buffered(1))


def _ada_kernel(c_ref, w_ref, b_ref, o_ref):
    s = jax.nn.silu(c_ref[...]).astype(BF16)
    o_ref[0] = jnp.dot(s, w_ref[0].astype(BF16), preferred_element_type=F32) + b_ref[0]


def _ada_mod(c_all, ada_w, ada_b):
    nb = c_all.shape[0]
    tn = 1024
    return pl.pallas_call(
        _ada_kernel,
        grid=(DEPTH, 6 * D_MODEL // tn),
        in_specs=[
            pl.BlockSpec((nb, D_MODEL), lambda l, j: (0, 0)),
            pl.BlockSpec((1, D_MODEL, tn), lambda l, j: (l, 0, j)),
            pl.BlockSpec((1, 1, tn), lambda l, j: (l, 0, j)),
        ],
        out_specs=pl.BlockSpec((1, nb, tn), lambda l, j: (l, 0, j)),
        out_shape=jax.ShapeDtypeStruct((DEPTH, nb, 6 * D_MODEL), F32),
        compiler_params=_params(("parallel", "parallel")),
        name="ada_mod",
    )(c_all, ada_w, ada_b.reshape(DEPTH, 1, 6 * D_MODEL))


def _mod_norm_rows(x, gs, shift):
    ms = jnp.mean(x * x, axis=-1, keepdims=True)
    return (x * lax.rsqrt(ms + EPS)) * gs + shift


def _inproj_kernel(xn_ref, x0_ref, modn_ref, mod0_ref, g_ref, w_ref, o_ref, ha_ref, hb_ref, *, tm):
    i = pl.program_id(0)
    rc = NORM_ROWS
    g = g_ref[...]

    def fill(x_ref, mod_ref, dst_ref):
        shift = mod_ref[0, 0:1, :]
        gs = g * (1.0 + mod_ref[0, 1:2, :])
        for c in range(tm // rc):
            rows = slice(c * rc, (c + 1) * rc)
            dst_ref[rows, :] = _mod_norm_rows(x_ref[rows, :], gs, shift).astype(BF16)

    @pl.when(i == 0)
    def _():
        fill(x0_ref, mod0_ref, ha_ref)

    def step(cur_ref, nxt_ref):
        fill(xn_ref, modn_ref, nxt_ref)
        o_ref[...] = jnp.dot(cur_ref[...], w_ref[...], preferred_element_type=F32).astype(o_ref.dtype)

    @pl.when(i % 2 == 0)
    def _():
        step(ha_ref, hb_ref)

    @pl.when(i % 2 == 1)
    def _():
        step(hb_ref, ha_ref)


def _in_proj(x, mod_l, b0, seq, g, w_in_b, layer):
    t = x.shape[0]
    tm = 512
    n = t // tm
    nxt = lambda i: jnp.minimum(i + 1, n - 1)
    return pl.pallas_call(
        functools.partial(_inproj_kernel, tm=tm),
        grid=(n,),
        in_specs=[
            pl.BlockSpec((tm, D_MODEL), lambda i: (nxt(i), 0)),
            _single((tm, D_MODEL), lambda i: (0, 0)),
            pl.BlockSpec((1, 6, D_MODEL), lambda i: (b0 + (nxt(i) * tm) // seq, 0, 0)),
            pl.BlockSpec((1, 6, D_MODEL), lambda i: (b0, 0, 0)),
            pl.BlockSpec((None, 1, D_MODEL), lambda i: (layer, 0, 0)),
            _single((None, D_MODEL, IN_PAD), lambda i: (layer, 0, 0)),
        ],
        out_specs=pl.BlockSpec((tm, IN_PAD), lambda i: (i, 0)),
        out_shape=jax.ShapeDtypeStruct((t, IN_PAD), BF16),
        scratch_shapes=[pltpu.VMEM((tm, D_MODEL), BF16), pltpu.VMEM((tm, D_MODEL), BF16)],
        compiler_params=_params(("arbitrary",)),
        name="in_proj",
    )(x, x, mod_l, mod_l, g, w_in_b)


def _gelu(x):
    return 0.5 * x * (1.0 + lax.erf(x * (1.0 / math.sqrt(2.0))))


def _gmlp_kernel(u_ref, v_ref, g_ref, ws_ref, bias_ref, o_ref, *, tm):
    g = g_ref[...]
    for n in range(tm // CHUNK):
        rows = slice(n * CHUNK, (n + 1) * CHUNK)
        u = _gelu(u_ref[rows, :].astype(F32))
        v = _gelu(v_ref[rows, :].astype(F32))
        vc = v - jnp.mean(v, axis=-1, keepdims=True)
        vn = (vc * lax.rsqrt(jnp.mean(vc * vc, axis=-1, keepdims=True) + EPS) * g).astype(BF16)
        for h in range(A_HEADS):
            cols = slice(h * LANES, (h + 1) * LANES)
            mixed = jnp.dot(ws_ref[h], vn[:, cols], preferred_element_type=F32) + bias_ref[:, cols]
            o_ref[rows, cols] = (u[:, cols] * mixed).astype(o_ref.dtype)


def _gmlp(proj, g, ws_b, bias_full, layer):
    t = proj.shape[0]
    tm = 512
    return pl.pallas_call(
        functools.partial(_gmlp_kernel, tm=tm),
        grid=(t // tm,),
        in_specs=[
            pl.BlockSpec((tm, W_GROUP), lambda i: (i, 0)),
            pl.BlockSpec((tm, W_GROUP), lambda i: (i, 1)),
            pl.BlockSpec((None, 1, W_GROUP), lambda i: (layer, 0, 0)),
            pl.BlockSpec((None, A_HEADS, CHUNK, CHUNK), lambda i: (layer, 0, 0, 0)),
            pl.BlockSpec((None, CHUNK, W_GROUP), lambda i: (layer, 0, 0)),
        ],
        out_specs=pl.BlockSpec((tm, W_GROUP), lambda i: (i, 0)),
        out_shape=jax.ShapeDtypeStruct((t, W_GROUP), BF16),
        compiler_params=_params(("parallel",)),
        name="gmlp",
    )(proj, proj, g, ws_b, bias_full)


def _hy_filter_kernel(f_ref, w1_ref, b1_ref, f1_ref, w2_ref, b2_ref, f2_ref, w3_ref, dec_ref, fwd_ref, o_ref,
                      k_s, prev_s):
    e = pl.program_id(0)

    @pl.when(e == 0)
    def _():
        prev_s[...] = jnp.zeros(prev_s.shape, F32)

    hp = lax.Precision.HIGHEST
    feats = f_ref[...]
    h = jnp.sin(f1_ref[...] * (jnp.dot(feats, w1_ref[...], precision=hp, preferred_element_type=F32) + b1_ref[...]))
    h = jnp.sin(f2_ref[...] * (jnp.dot(h, w2_ref[...], precision=hp, preferred_element_type=F32) + b2_ref[...]))
    half = feats.shape[0]
    decay = jnp.abs(dec_ref[0])
    for grp in range(2):
        k = jnp.dot(h, w3_ref[0, grp], precision=hp, preferred_element_type=F32)
        tcol = feats[:, grp * FILT_H:grp * FILT_H + 1]
        valid = feats[:, (grp + 1) * FILT_H - 1:(grp + 1) * FILT_H]
        k_s[grp * half:(grp + 1) * half, :] = k * jnp.exp(-tcol * decay) * valid

    odd = (lax.broadcasted_iota(jnp.int32, (fwd_ref.shape[0], W_GROUP), 0) & 1) == 1
    for order in range(2):
        taps = k_s[:, order * W_GROUP:(order + 1) * W_GROUP].astype(BF16)
        cur = jnp.dot(fwd_ref[...], taps, preferred_element_type=F32)
        prev = prev_s[order]
        o_ref[order, 0] = cur + jnp.where(odd, -prev, prev)
        prev_s[order] = cur


def _hy_filters(feats, w1p, b1p, f1p, w2p, b2p, f2p, w3d, decd, fwd_b, layer, seq):
    p = HY_P
    nb = seq // p
    c2 = lambda e: (layer, 0, 0)
    return pl.pallas_call(
        _hy_filter_kernel,
        grid=(2 * nb,),
        in_specs=[
            pl.BlockSpec((p // 2, LANES), lambda e: (e, 0)),
            pl.BlockSpec((None, LANES, LANES), c2),
            pl.BlockSpec((None, 1, LANES), c2),
            pl.BlockSpec((None, 1, LANES), c2),
            pl.BlockSpec((None, LANES, LANES), c2),
            pl.BlockSpec((None, 1, LANES), c2),
            pl.BlockSpec((None, 1, LANES), c2),
            pl.BlockSpec((None, 1, 2, LANES, 2 * W_GROUP), lambda e: (layer, jnp.where(e < nb, 1, 0), 0, 0, 0)),
            pl.BlockSpec((None, 1, 1, 2 * W_GROUP), lambda e: (layer, jnp.where(e < nb, 1, 0), 0, 0)),
            pl.BlockSpec((2 * p, p), lambda e: (0, 0)),
        ],
        out_specs=pl.BlockSpec((2, 1, 2 * p, W_GROUP), lambda e: (0, jnp.maximum(e - 1, 0), 0, 0)),
        out_shape=jax.ShapeDtypeStruct((2, 2 * nb - 1, 2 * p, W_GROUP), F32),
        scratch_shapes=[pltpu.VMEM((p, 2 * W_GROUP), F32), pltpu.VMEM((2, 2 * p, W_GROUP), F32)],
        compiler_params=_params(("arbitrary",)),
        name="hy_filter",
    )(feats, w1p, b1p, f1p, w2p, b2p, f2p, w3d, decd, fwd_b)


def _dwconv_rows(src_ref, r0, rc, total, w, b):
    halo = BF16_ROWS
    xc = src_ref[r0:r0 + rc, :].astype(F32)
    zeros = jnp.zeros((halo, xc.shape[1]), F32)
    prev = src_ref[r0 - halo:r0, :].astype(F32) if r0 > 0 else zeros
    nxt = src_ref[r0 + rc:r0 + rc + halo, :].astype(F32) if r0 + rc < total else zeros
    cat = jnp.concatenate([prev, xc, nxt], axis=0)
    n = rc + 2 * halo
    dn = pltpu.roll(cat, 1, 0)[halo:halo + rc]
    up = pltpu.roll(cat, n - 1, 0)[halo:halo + rc]
    return dn * w[0:1] + xc * w[1:2] + up * w[2:3] + b


def _hyena_kernel(p1_ref, p2_ref, pv_ref, cw_ref, cb_ref, skip_ref, h_ref, fwd_ref, inv_ref, o_ref,
                  vz, x1z, x2z, vhat0, vhat1, yhat_a, yhat_b, *, seq, p):
    nb = seq // p
    rc = 256
    rm = 32
    yhat = (yhat_a, yhat_b)

    def conv_block(src_ref, dst, part, i):
        for r0 in range(i * p, (i + 1) * p, rc):
            dst[r0:r0 + rc, :] = _dwconv_rows(src_ref, r0, rc, seq, cw_ref[part], cb_ref[part:part + 1, :])

    for i in range(nb):
        conv_block(pv_ref, vz, 2, i)
        vhat0[i] = jnp.dot(fwd_ref[...], vz[i * p:(i + 1) * p, :].astype(BF16), preferred_element_type=F32)
        conv_block(p1_ref, x1z, 0, i)
        conv_block(p2_ref, x2z, 1, i)

    def block_spectrum(order, i, vhat, dst):
        for r in range(p // rm):
            rows_re = slice(r * rm, (r + 1) * rm)
            rows_im = slice(p + r * rm, p + (r + 1) * rm)
            acc_r = None
            acc_i = None
            for ip in range(nb):
                d = i - ip + (nb - 1)
                hr = h_ref[order, d, rows_re, :]
                hi = h_ref[order, d, rows_im, :]
                ur = vhat[ip, rows_re, :]
                ui = vhat[ip, rows_im, :]
                if r == 0:
                    first = lax.broadcasted_iota(jnp.int32, hr.shape, 0) == 0
                    tr = hr * ur - jnp.where(first, 0.0, hi * ui)
                    ti = jnp.where(first, hi * ui, hr * ui + hi * ur)
                else:
                    tr = hr * ur - hi * ui
                    ti = hr * ui + hi * ur
                acc_r = tr if acc_r is None else acc_r + tr
                acc_i = ti if acc_i is None else acc_i + ti
            dst[rows_re, :] = acc_r.astype(BF16)
            dst[rows_im, :] = acc_i.astype(BF16)

    for order in range(2):
        gate = x1z if order == 0 else x2z
        vhat = vhat0 if order == 0 else vhat1
        skip = skip_ref[order:order + 1, :]
        for i in range(nb + 1):
            if i >= 1:
                y = jnp.dot(inv_ref[...], yhat[(i - 1) % 2][...], preferred_element_type=F32)
            if i < nb:
                block_spectrum(order, i, vhat, yhat[i % 2])
            if i >= 1:
                rows = slice((i - 1) * p, i * p)
                res = gate[rows, :] * (y + skip * vz[rows, :])
                if order == 0:
                    vz[rows, :] = res
                    vhat1[i - 1] = jnp.dot(fwd_ref[...], res.astype(BF16), preferred_element_type=F32)
                else:
                    o_ref[rows, :] = res.astype(o_ref.dtype)


def _hyena(proj, cw, cb, skip, hfilt, fwd_b, inv_b, layer, seq):
    t = proj.shape[0]
    p = HY_P
    nb = seq // p
    nct = W_GROUP // LANES
    base = IN_A // LANES
    return pl.pallas_call(
        functools.partial(_hyena_kernel, seq=seq, p=p),
        grid=(nct, t // seq),
        in_specs=[
            pl.BlockSpec((seq, LANES), lambda c, b: (b, base + c)),
            pl.BlockSpec((seq, LANES), lambda c, b: (b, base + nct + c)),
            pl.BlockSpec((seq, LANES), lambda c, b: (b, base + 2 * nct + c)),
            pl.BlockSpec((None, 3, 3, LANES), lambda c, b: (layer, 0, 0, c)),
            pl.BlockSpec((None, 3, LANES), lambda c, b: (layer, 0, c)),
            pl.BlockSpec((None, 2, LANES), lambda c, b: (layer, 0, c)),
            _single((2, 2 * nb - 1, 2 * p, LANES), lambda c, b: (0, 0, 0, c)),
            _single((2 * p, p), lambda c, b: (0, 0)),
            _single((p, 2 * p), lambda c, b: (0, 0)),
        ],
        out_specs=pl.BlockSpec((seq, LANES), lambda c, b: (b, c)),
        out_shape=jax.ShapeDtypeStruct((t, W_GROUP), BF16),
        scratch_shapes=[
            pltpu.VMEM((seq, LANES), F32),
            pltpu.VMEM((seq, LANES), F32),
            pltpu.VMEM((seq, LANES), F32),
            pltpu.VMEM((nb, 2 * p, LANES), F32),
            pltpu.VMEM((nb, 2 * p, LANES), F32),
            pltpu.VMEM((2 * p, LANES), BF16),
            pltpu.VMEM((2 * p, LANES), BF16),
        ],
        compiler_params=_params(("parallel", "parallel")),
        name="hyena",
    )(proj, proj, proj, cw, cb, skip, hfilt, fwd_b, inv_b)


def _rope_lanes(x, cos_t, sin_t, half):
    lane = lax.broadcasted_iota(jnp.int32, x.shape, 1)
    lower = (lane % (2 * half)) < half
    partner = jnp.where(lower, pltpu.roll(x, LANES - half, 1), pltpu.roll(x, half, 1))
    return x * cos_t + partner * sin_t


def _rms_rows(x, g, width):
    ss = jnp.sum(x * x, axis=-1, keepdims=True) * (1.0 / width)
    return x * lax.rsqrt(ss + EPS) * g


def _gqa_prep_kernel(q_ref, k_ref, cos_ref, sin_ref, qg_ref, kg_ref, qo_ref, ko_ref):
    cos_t = cos_ref[...]
    sin_t = sin_ref[...]
    scale = LOG2E / math.sqrt(C_HD)
    for h in range(C_HEADS):
        cols = slice(h * C_HD, (h + 1) * C_HD)
        q = _rope_lanes(_rms_rows(q_ref[:, cols].astype(F32), qg_ref[...], C_HD), cos_t, sin_t, C_HD // 4)
        qo_ref[:, cols] = (q * scale).astype(BF16)
    for h in range(C_KV_HEADS):
        cols = slice(h * C_HD, (h + 1) * C_HD)
        k = _rope_lanes(_rms_rows(k_ref[:, cols].astype(F32), kg_ref[...], C_HD), cos_t, sin_t, C_HD // 4)
        ko_ref[:, cols] = k.astype(BF16)


def _gqa_prep(proj, cos_t, sin_t, qg, kg, layer, seq):
    t = proj.shape[0]
    tm = 512
    qw = C_HEADS * C_HD
    kw = C_KV_HEADS * C_HD
    c0 = IN_A + IN_B
    g2 = lambda i: (layer, 0, 0)
    return pl.pallas_call(
        _gqa_prep_kernel,
        grid=(t // tm,),
        in_specs=[
            pl.BlockSpec((tm, qw), lambda i: (i, c0 // qw)),
            pl.BlockSpec((tm, kw), lambda i: (i, (c0 + qw) // kw)),
            pl.BlockSpec((tm, LANES), lambda i: (i % (seq // tm), 0)),
            pl.BlockSpec((tm, LANES), lambda i: (i % (seq // tm), 0)),
            pl.BlockSpec((None, 1, C_HD), g2),
            pl.BlockSpec((None, 1, C_HD), g2),
        ],
        out_specs=[
            pl.BlockSpec((tm, qw), lambda i: (i, 0)),
            pl.BlockSpec((tm, kw), lambda i: (i, 0)),
        ],
        out_shape=[
            jax.ShapeDtypeStruct((t, qw), BF16),
            jax.ShapeDtypeStruct((t, kw), BF16),
        ],
        compiler_params=_params(("parallel",)),
        name="gqa_prep",
    )(proj, proj, cos_t, sin_t, qg, kg)


def _mla_prep_kernel(qa_ref, ckv_ref, kr_ref, cos_ref, sin_ref, qag_ref, kvg_ref, wq_ref, wkv_ref,
                     qn_ref, kn_ref, qo_ref, ko_ref, vo_ref):
    cos_t = cos_ref[...]
    sin_t = sin_ref[...]
    width = NOPE + ROPE_D
    scale = LOG2E / math.sqrt(width)
    qa = _rms_rows(qa_ref[...].astype(F32), qag_ref[...], Q_LORA).astype(BF16)
    q = jnp.dot(qa, wq_ref[...], preferred_element_type=F32)
    ckv = _rms_rows(ckv_ref[...].astype(F32), kvg_ref[...], KV_LORA).astype(BF16)
    kv = jnp.dot(ckv, wkv_ref[...], preferred_element_type=F32)
    kr = kr_ref[...].astype(F32)
    kr_ss = jnp.sum(kr * kr, axis=-1, keepdims=True)
    qn = qn_ref[...]
    kn = kn_ref[...]
    for h in range(D_HEADS):
        base = h * MLA_SLOT
        qh = _rms_rows(q[:, base:base + MLA_SLOT], qn, width)
        qo_ref[:, base:base + NOPE] = (qh[:, :NOPE] * scale).astype(BF16)
        qr = _rope_lanes(qh[:, NOPE:], cos_t, sin_t, ROPE_D // 4)
        qo_ref[:, base + NOPE:base + MLA_SLOT] = (qr * scale).astype(BF16)
        k_nope = kv[:, base:base + NOPE]
        ss = (jnp.sum(k_nope * k_nope, axis=-1, keepdims=True) + kr_ss) * (1.0 / width)
        inv = lax.rsqrt(ss + EPS)
        ko_ref[:, base:base + NOPE] = (k_nope * inv * kn[:, :NOPE]).astype(BF16)
        krn = _rope_lanes(kr * inv * kn[:, NOPE:], cos_t, sin_t, ROPE_D // 4)
        ko_ref[:, base + NOPE:base + MLA_SLOT] = krn.astype(BF16)
        vo_ref[:, h * V_HD:(h + 1) * V_HD] = kv[:, base + NOPE:base + NOPE + V_HD].astype(BF16)


def _mla_prep(proj, cos_t, sin_t, qag, kvg, wq_b, wkv_b, qn_p, kn_p, layer, seq):
    t = proj.shape[0]
    tm = 512
    c0 = IN_A + IN_B + IN_C
    g2 = lambda i: (layer, 0, 0)
    hw = D_HEADS * MLA_SLOT
    return pl.pallas_call(
        _mla_prep_kernel,
        grid=(t // tm,),
        in_specs=[
            pl.BlockSpec((tm, Q_LORA), lambda i: (i, c0 // Q_LORA)),
            pl.BlockSpec((tm, KV_LORA), lambda i: (i, (c0 + Q_LORA) // KV_LORA)),
            pl.BlockSpec((tm, LANES), lambda i: (i, (c0 + Q_LORA + KV_LORA) // LANES)),
            pl.BlockSpec((tm, LANES), lambda i: (i % (seq // tm), 0)),
            pl.BlockSpec((tm, LANES), lambda i: (i % (seq // tm), 0)),
            pl.BlockSpec((None, 1, Q_LORA), g2),
            pl.BlockSpec((None, 1, KV_LORA), g2),
            pl.BlockSpec((None, Q_LORA, hw), g2),
            pl.BlockSpec((None, KV_LORA, hw), g2),
            pl.BlockSpec((None, 1, MLA_SLOT), g2),
            pl.BlockSpec((None, 1, MLA_SLOT), g2),
        ],
        out_specs=[
            pl.BlockSpec((tm, hw), lambda i: (i, 0)),
            pl.BlockSpec((tm, hw), lambda i: (i, 0)),
            pl.BlockSpec((tm, D_HEADS * V_HD), lambda i: (i, 0)),
        ],
        out_shape=[
            jax.ShapeDtypeStruct((t, hw), BF16),
            jax.ShapeDtypeStruct((t, hw), BF16),
            jax.ShapeDtypeStruct((t, D_HEADS * V_HD), BF16),
        ],
        compiler_params=_params(("parallel",)),
        name="mla_prep",
    )(proj, proj, proj, cos_t, sin_t, qag, kvg, wq_b, wkv_b, qn_p, kn_p)


def _attn_kernel(q_ref, k_ref, v_ref, o_ref, vext_ref, *, groups, dk, dv, sub):
    @pl.when(pl.program_id(2) == 0)
    def _():
        vext_ref[:, :dv] = v_ref[...]
        vext_ref[:, dv:] = jnp.ones((v_ref.shape[0], dv), BF16)

    k = k_ref[...]
    v = vext_ref[...]
    tq = q_ref.shape[0]
    chains = [(g, slice(r * sub, (r + 1) * sub)) for g in range(groups) for r in range(tq // sub)]

    def scores(c):
        g, rows = chains[c]
        q = q_ref[rows, g * dk:(g + 1) * dk]
        return lax.dot_general(q, k, (((1,), (1,)), ((), ())), preferred_element_type=F32)

    s = scores(0)
    for c, (g, rows) in enumerate(chains):
        s_next = scores(c + 1) if c + 1 < len(chains) else None
        m = jnp.max(s, axis=-1, keepdims=True)
        pexp = jnp.exp2(s - m).astype(BF16)
        o = jnp.dot(pexp, v, preferred_element_type=F32)
        o_ref[rows, g * dv:(g + 1) * dv] = (o[:, :dv] / o[:, dv:]).astype(o_ref.dtype)
        s = s_next


def _attention(q, k, v, v_col0, seq, kv_heads, groups, dk, dv, tq, sub, name):
    t = q.shape[0]
    nq = seq // tq
    return pl.pallas_call(
        functools.partial(_attn_kernel, groups=groups, dk=dk, dv=dv, sub=sub),
        grid=(t // seq, kv_heads, nq),
        in_specs=[
            pl.BlockSpec((tq, groups * dk), lambda b, h, i: (b * nq + i, h)),
            pl.BlockSpec((seq, dk), lambda b, h, i: (b, h)),
            pl.BlockSpec((seq, dv), lambda b, h, i: (b, v_col0 + h)),
        ],
        out_specs=pl.BlockSpec((tq, groups * dv), lambda b, h, i: (b * nq + i, h)),
        out_shape=jax.ShapeDtypeStruct((t, kv_heads * groups * dv), BF16),
        scratch_shapes=[pltpu.VMEM((seq, 2 * dv), BF16)],
        compiler_params=_params(("parallel", "parallel", "arbitrary")),
        name=name,
    )(q, k, v)


def _outproj_kernel(ya_ref, yb_ref, yc_ref, yd_ref, x_ref, mod_ref, gg_ref, w_ref, o_ref):
    acc = None
    for gi, y_ref in enumerate((ya_ref, yb_ref, yc_ref, yd_ref)):
        rows = slice(gi * W_GROUP, (gi + 1) * W_GROUP)
        yn = _rms_rows(y_ref[...].astype(F32), gg_ref[:, rows], W_GROUP).astype(BF16)
        part = jnp.dot(yn, w_ref[rows, :], preferred_element_type=F32)
        acc = part if acc is None else acc + part
    o_ref[...] = x_ref[...] + mod_ref[0, 2:3, :] * acc


def _out_proj(ya, yb, yc, yd, x, mod_l, b0, seq, gg, w_out_b, layer):
    t = x.shape[0]
    tm = 512
    yspec = pl.BlockSpec((tm, W_GROUP), lambda i: (i, 0))
    return pl.pallas_call(
        _outproj_kernel,
        grid=(t // tm,),
        in_specs=[
            yspec, yspec, yspec, yspec,
            pl.BlockSpec((tm, D_MODEL), lambda i: (i, 0)),
            pl.BlockSpec((1, 6, D_MODEL), lambda i: (b0 + (i * tm) // seq, 0, 0)),
            pl.BlockSpec((None, 1, D_MODEL), lambda i: (layer, 0, 0)),
            _single((None, D_MODEL, D_MODEL), lambda i: (layer, 0, 0)),
        ],
        out_specs=pl.BlockSpec((tm, D_MODEL), lambda i: (i, 0)),
        out_shape=jax.ShapeDtypeStruct((t, D_MODEL), F32),
        compiler_params=_params(("parallel",)),
        name="out_proj",
    )(ya, yb, yc, yd, x, mod_l, gg, w_out_b)


def _ffn_kernel(x_ref, xp_ref, xn_ref, mod_ref, g_ref, wg_ref, wu_ref, cwg_ref, cwu_ref, cbg_ref, cbu_ref,
                wd_ref, o_ref, h_ref, *, tm, seq):
    i = pl.program_id(0)
    j = pl.program_id(1)
    nj = pl.num_programs(1)
    rc = NORM_ROWS
    halo = FFN_HALO

    @pl.when(j == 0)
    def _():
        shift = mod_ref[0, 3:4, :]
        gs = g_ref[...] * (1.0 + mod_ref[0, 4:5, :])
        has_prev = ((i * tm) % seq) != 0
        has_next = (((i + 1) * tm) % seq) != 0
        hp = _mod_norm_rows(xp_ref[...], gs, shift)
        hn = _mod_norm_rows(xn_ref[...], gs, shift)
        h_ref[pl.ds(0, halo), :] = jnp.where(has_prev, hp, 0.0).astype(BF16)
        h_ref[pl.ds(halo + tm, halo), :] = jnp.where(has_next, hn, 0.0).astype(BF16)

        chains = 8

        def body(c, carry):
            for k in range(chains):
                r0 = pl.multiple_of((c * chains + k) * rc, rc)
                h_ref[pl.ds(halo + r0, rc), :] = _mod_norm_rows(x_ref[pl.ds(r0, rc), :], gs, shift).astype(BF16)
            return carry

        lax.fori_loop(0, tm // (rc * chains), body, 0)
        o_ref[...] = jnp.zeros(o_ref.shape, F32)

    hfull = h_ref[...]
    n = tm + 2 * halo

    def conv(u, cw_ref, cb_ref):
        dn = pltpu.roll(u, 1, 0)[halo:halo + tm]
        up = pltpu.roll(u, n - 1, 0)[halo:halo + tm]
        return dn * cw_ref[0:1, :] + u[halo:halo + tm] * cw_ref[1:2, :] + up * cw_ref[2:3, :] + cb_ref[...]

    gate = conv(jnp.dot(hfull, wg_ref[...], preferred_element_type=F32), cwg_ref, cbg_ref)
    upv = conv(jnp.dot(hfull, wu_ref[...], preferred_element_type=F32), cwu_ref, cbu_ref)
    act = (jax.nn.silu(gate) * upv).astype(BF16)
    o_ref[...] += jnp.dot(act, wd_ref[...], preferred_element_type=F32)

    @pl.when(j == nj - 1)
    def _():
        o_ref[...] = x_ref[...] + mod_ref[0, 5:6, :] * o_ref[...]


def _ffn(x, mod_l, b0, seq, g, w_up_b, cw, cb, w_down_b, layer):
    t = x.shape[0]
    tm, tf = 1024, 512
    nf = D_FF // tf
    hb = tm // FFN_HALO
    last = t // FFN_HALO - 1
    return pl.pallas_call(
        functools.partial(_ffn_kernel, tm=tm, seq=seq),
        grid=(t // tm, nf),
        in_specs=[
            pl.BlockSpec((tm, D_MODEL), lambda i, j: (i, 0)),
            pl.BlockSpec((FFN_HALO, D_MODEL), lambda i, j: (jnp.maximum(i * hb - 1, 0), 0)),
            pl.BlockSpec((FFN_HALO, D_MODEL), lambda i, j: (jnp.minimum((i + 1) * hb, last), 0)),
            pl.BlockSpec((1, 6, D_MODEL), lambda i, j: (b0 + (i * tm) // seq, 0, 0)),
            pl.BlockSpec((None, 1, D_MODEL), lambda i, j: (layer, 0, 0)),
            pl.BlockSpec((None, D_MODEL, tf), lambda i, j: (layer, 0, j)),
            pl.BlockSpec((None, D_MODEL, tf), lambda i, j: (layer, 0, nf + j)),
            pl.BlockSpec((None, 3, tf), lambda i, j: (layer, 0, j)),
            pl.BlockSpec((None, 3, tf), lambda i, j: (layer, 0, nf + j)),
            pl.BlockSpec((None, 1, tf), lambda i, j: (layer, 0, j)),
            pl.BlockSpec((None, 1, tf), lambda i, j: (layer, 0, nf + j)),
            pl.BlockSpec((None, tf, D_MODEL), lambda i, j: (layer, j, 0)),
        ],
        out_specs=pl.BlockSpec((tm, D_MODEL), lambda i, j: (i, 0)),
        out_shape=jax.ShapeDtypeStruct((t, D_MODEL), F32),
        scratch_shapes=[pltpu.VMEM((tm + 2 * FFN_HALO, D_MODEL), BF16)],
        compiler_params=_params(("parallel", "arbitrary"), FFN_VMEM_LIMIT),
        name="ffn",
    )(x, x, x, mod_l, g, w_up_b, w_up_b, cw, cw, cb, cb, w_down_b)


def _axial_tables(seq, sec, lanes_used):
    pos = jnp.arange(seq, dtype=jnp.int32)
    row = (pos // GRID_W).astype(F32)
    col = (pos % GRID_W).astype(F32)
    inv = ROPE_THETA ** (-jnp.arange(0, sec, 2, dtype=F32) / sec)
    half = sec // 2
    lane = jnp.arange(LANES)
    in_use = lane < lanes_used
    which = (lane // sec) % 2
    freq = inv[lane % half]
    ang = jnp.where(which[None, :] == 0, row[:, None], col[:, None]) * freq[None, :]
    cos_t = jnp.where(in_use[None, :], jnp.cos(ang), 1.0)
    sign = jnp.where((lane % sec) < half, -1.0, 1.0)
    sin_t = jnp.where(in_use[None, :], jnp.sin(ang) * sign[None, :], 0.0)
    return cos_t.astype(F32), sin_t.astype(F32)


def _hy_filter_params(hy_w1, hy_b1, hy_f1, hy_w2, hy_b2, hy_f2, hy_w3, hy_decay):
    nl = hy_w1.shape[0]
    zero = jnp.zeros((nl, FILT_H, FILT_H), F32)
    bdiag = lambda w: jnp.concatenate([jnp.concatenate([w, zero], axis=2), jnp.concatenate([zero, w], axis=2)], axis=1)
    twice = lambda a: jnp.tile(a.reshape(nl, 1, FILT_H), (1, 1, 2))
    w1p = bdiag(jnp.pad(hy_w1, ((0, 0), (0, FILT_H - POS_EMB), (0, 0))))
    w2p = bdiag(hy_w2)
    w3 = hy_w3.reshape(nl, FILT_H, 2, 2, W_GROUP).transpose(0, 3, 1, 2, 4).reshape(nl, 2, FILT_H, 2 * W_GROUP)
    z3 = jnp.zeros_like(w3)
    w3d = jnp.stack([jnp.concatenate([w3, z3], axis=2), jnp.concatenate([z3, w3], axis=2)], axis=2)
    decd = hy_decay.reshape(nl, 2, 2, W_GROUP).transpose(0, 2, 1, 3).reshape(nl, 2, 1, 2 * W_GROUP)
    return w1p, twice(hy_b1), twice(hy_f1), w2p, twice(hy_b2), twice(hy_f2), w3d, decd


def _hyena_feats(seq):
    n = jnp.arange(seq, dtype=jnp.int32)
    posi = jnp.concatenate([seq - n, n])
    valid = (posi < seq).astype(F32)
    posi = jnp.minimum(posi, seq - 1)
    tl = jnp.linspace(0.0, 1.0, seq, dtype=F32)
    tlin = jnp.concatenate([tl[seq - 1:], tl[:0:-1], tl])
    bands = jnp.linspace(1e-4, POS_BANDS - 1, POS_BANDS, dtype=F32)
    ang = (2.0 * math.pi / seq) * posi.astype(F32)[:, None] * bands[None, :]
    feats = jnp.concatenate([tlin[:, None], jnp.cos(ang), -jnp.sin(ang)], axis=-1)
    pad = jnp.zeros((2 * seq, FILT_H - POS_EMB - 1), F32)
    f64 = jnp.concatenate([feats, pad, valid[:, None]], axis=-1)
    half = HY_P // 2
    return f64.reshape(-1, 2, half, FILT_H).transpose(0, 2, 1, 3).reshape(seq, 2 * FILT_H)


def _dft_mats(p):
    r = jnp.arange(p, dtype=jnp.int32)
    q = (r[:, None] * r[None, :]) % (2 * p)
    ang = q.astype(F32) * (math.pi / p)
    cosm = jnp.cos(ang)
    sinm = jnp.sin(ang)
    alt = jnp.where(r % 2 == 0, 1.0, -1.0).astype(F32)
    im_rows = jnp.where((r == 0)[:, None], alt[None, :], -sinm)
    fwd = jnp.concatenate([cosm, im_rows], axis=0)
    wre = jnp.where((r == 0)[None, :], 0.5, 1.0) / p
    inv_re = cosm * wre
    inv_im = jnp.where((r == 0)[None, :], alt[:, None] * (0.5 / p), -sinm / p)
    inv = jnp.concatenate([inv_re, inv_im], axis=1)
    return fwd.astype(BF16), inv.astype(BF16)


def kernel(x_prompt, x_sample, c_prompt, c_sample, ada_w, ada_b, norm1_g, w_in, gm_vnorm_g, gm_spatial_w, gm_spatial_b, hy_conv_w, hy_conv_b, hy_w1, hy_b1, hy_f1, hy_w2, hy_b2, hy_f2, hy_w3, hy_decay, hy_skip, gqa_qn_g, gqa_kn_g, mla_q_a_g, mla_w_q_b, mla_kv_a_g, mla_w_kv_b, mla_qn_g, mla_kn_g, group_norm_g, w_out, norm2_g, ffn_w_up, ffn_conv_w, ffn_conv_b, ffn_w_down):
    nl = DEPTH
    w_in_b = jnp.pad(w_in.astype(BF16), ((0, 0), (0, 0), (0, IN_PAD - IN_COLS)))
    w_out_b = w_out.astype(BF16)
    w_up_b = ffn_w_up.astype(BF16)
    w_down_b = ffn_w_down.astype(BF16)
    ws_b = gm_spatial_w.astype(BF16)
    gm_bias = jnp.broadcast_to(jnp.swapaxes(gm_spatial_b, 1, 2)[:, :, :, None],
                               (nl, CHUNK, A_HEADS, LANES)).reshape(nl, CHUNK, W_GROUP)
    row3 = lambda a: a.reshape(nl, 1, a.shape[-1])
    norm1 = row3(norm1_g)
    norm2 = row3(norm2_g)
    gm_g = row3(gm_vnorm_g)
    gg = row3(group_norm_g)
    hy_cw = hy_conv_w.reshape(nl, 3, 3, W_GROUP).transpose(0, 2, 1, 3)
    hy_cb = hy_conv_b.reshape(nl, 3, W_GROUP)
    hy_filter_params = _hy_filter_params(hy_w1, hy_b1, hy_f1, hy_w2, hy_b2, hy_f2, hy_w3, hy_decay)
    gqa_qg = row3(gqa_qn_g)
    gqa_kg = row3(gqa_kn_g)
    mla_qag = row3(mla_q_a_g)
    mla_kvg = row3(mla_kv_a_g)
    slot_pad = MLA_SLOT - NOPE - ROPE_D
    wq_b = jnp.pad(mla_w_q_b.reshape(nl, Q_LORA, D_HEADS, NOPE + ROPE_D),
                   ((0, 0), (0, 0), (0, 0), (0, slot_pad))).reshape(nl, Q_LORA, D_HEADS * MLA_SLOT).astype(BF16)
    wkv_b = mla_w_kv_b.astype(BF16)
    mla_qn = jnp.pad(row3(mla_qn_g), ((0, 0), (0, 0), (0, slot_pad)))
    mla_kn = jnp.pad(row3(mla_kn_g), ((0, 0), (0, 0), (0, slot_pad)))
    ffn_cb = row3(ffn_conv_b)

    nbp = x_prompt.shape[0]
    c_all = jnp.concatenate([c_prompt, c_sample], axis=0)
    mod = _ada_mod(c_all, ada_w, ada_b).reshape(nl, c_all.shape[0], 6, D_MODEL)

    fwd_b, inv_b = _dft_mats(HY_P)

    def trunk(x3, b0):
        bsz, seq, _ = x3.shape
        x = x3.reshape(bsz * seq, D_MODEL)
        gcos, gsin = _axial_tables(seq, C_HD // 2, C_HD)
        mcos, msin = _axial_tables(seq, ROPE_D // 2, ROPE_D)
        feats = _hyena_feats(seq)
        sub = ATTN_SCORE_ELEMS // seq
        for l in range(nl):
            mod_l = mod[l]
            proj = _in_proj(x, mod_l, b0, seq, norm1, w_in_b, l)
            ya = _gmlp(proj, gm_g, ws_b, gm_bias, l)
            hfilt = _hy_filters(feats, *hy_filter_params, fwd_b, l, seq)
            yb = _hyena(proj, hy_cw, hy_cb, hy_skip, hfilt, fwd_b, inv_b, l, seq)
            gq, gk = _gqa_prep(proj, gcos, gsin, gqa_qg, gqa_kg, l, seq)
            yc = _attention(gq, gk, proj, GQA_V_COL // C_HD, seq, C_KV_HEADS, C_HEADS // C_KV_HEADS, C_HD, C_HD,
                            2 * sub, sub, "gqa_attn")
            mq, mk, mv = _mla_prep(proj, mcos, msin, mla_qag, mla_kvg, wq_b, wkv_b, mla_qn, mla_kn, l, seq)
            yd = _attention(mq, mk, mv, 0, seq, D_HEADS, 1, MLA_SLOT, V_HD, 4 * sub, sub, "mla_attn")
            x = _out_proj(ya, yb, yc, yd, x, mod_l, b0, seq, gg, w_out_b, l)
            x = _ffn(x, mod_l, b0, seq, norm2, w_up_b, ffn_conv_w, ffn_cb, w_down_b, l)
        return x.reshape(bsz, seq, D_MODEL)

    return trunk(x_prompt, 0), trunk(x_sample, nbp)
```

```python
import functools
import math

import jax
import jax.numpy as jnp
from jax import lax
from jax.experimental import pallas as pl
from jax.experimental.pallas import tpu as pltpu

F32 = jnp.float32
BF16 = jnp.bfloat16

D_MODEL = 2048
DEPTH = 4
GRID_W = 64
CHUNK = 128
W_GROUP = 512
A_HEADS = 4
POS_BANDS = 16
POS_EMB = 2 * POS_BANDS + 1
FILT_H = 64
C_HEADS = 4
C_KV_HEADS = 2
C_HD = 128
D_HEADS = 4
Q_LORA = 512
KV_LORA = 256
NOPE = 128
ROPE_D = 64
V_HD = 128
ROPE_THETA = 10000.0
D_FF = 5632
EPS = 1e-6
LOG2E = 1.4426950408889634
IN_A = 1024
IN_B = 1536
IN_C = 1024
IN_D = 832
IN_COLS = IN_A + IN_B + IN_C + IN_D
GQA_V_COL = IN_A + IN_B + (C_HEADS + C_KV_HEADS) * C_HD
IN_PAD = 4608

LANES = 128
BF16_ROWS = 16
NORM_ROWS = BF16_ROWS
VMEM_LIMIT = 56 * 1024 * 1024
FFN_VMEM_LIMIT = 60 * 1024 * 1024

ATTN_SCORE_ELEMS = 256 * 4096
HY_P = 512
FFN_HALO = BF16_ROWS
MLA_SLOT = 256


def _params(sem, vmem=VMEM_LIMIT):
    return pltpu.CompilerParams(dimension_semantics=sem, vmem_limit_bytes=vmem)


def _single(block_shape, index_map):
    return pl.BlockSpec(block_shape, index_map, pipeline_mode=pl.Buffered(1))


def _ada_kernel(c_ref, w_ref, b_ref, o_ref):
    s = jax.nn.silu(c_ref[...]).astype(BF16)
    o_ref[0] = jnp.dot(s, w_ref[0].astype(BF16), preferred_element_type=F32) + b_ref[0]


def _ada_mod(c_all, ada_w, ada_b):
    nb = c_all.shape[0]
    tn = 1024
    return pl.pallas_call(
        _ada_kernel,
        grid=(DEPTH, 6 * D_MODEL // tn),
        in_specs=[
            pl.BlockSpec((nb, D_MODEL), lambda l, j: (0, 0)),
            pl.BlockSpec((1, D_MODEL, tn), lambda l, j: (l, 0, j)),
            pl.BlockSpec((1, 1, tn), lambda l, j: (l, 0, j)),
        ],
        out_specs=pl.BlockSpec((1, nb, tn), lambda l, j: (l, 0, j)),
        out_shape=jax.ShapeDtypeStruct((DEPTH, nb, 6 * D_MODEL), F32),
        compiler_params=_params(("parallel", "parallel")),
        name="ada_mod",
    )(c_all, ada_w, ada_b.reshape(DEPTH, 1, 6 * D_MODEL))


def _mod_norm_rows(x, gs, shift):
    ms = jnp.mean(x * x, axis=-1, keepdims=True)
    return (x * lax.rsqrt(ms + EPS)) * gs + shift


def _inproj_kernel(xn_ref, x0_ref, modn_ref, mod0_ref, g_ref, w_ref, o_ref, ha_ref, hb_ref, *, tm):
    i = pl.program_id(0)
    rc = NORM_ROWS
    g = g_ref[...]

    def fill(x_ref, mod_ref, dst_ref):
        shift = mod_ref[0, 0:1, :]
        gs = g * (1.0 + mod_ref[0, 1:2, :])
        for c in range(tm // rc):
            rows = slice(c * rc, (c + 1) * rc)
            dst_ref[rows, :] = _mod_norm_rows(x_ref[rows, :], gs, shift).astype(BF16)

    @pl.when(i == 0)
    def _():
        fill(x0_ref, mod0_ref, ha_ref)

    def step(cur_ref, nxt_ref):
        fill(xn_ref, modn_ref, nxt_ref)
        o_ref[...] = jnp.dot(cur_ref[...], w_ref[...], preferred_element_type=F32).astype(o_ref.dtype)

    @pl.when(i % 2 == 0)
    def _():
        step(ha_ref, hb_ref)

    @pl.when(i % 2 == 1)
    def _():
        step(hb_ref, ha_ref)


def _in_proj(x, mod_l, b0, seq, g, w_in_b, layer):
    t = x.shape[0]
    tm = 512
    n = t // tm
    nxt = lambda i: jnp.minimum(i + 1, n - 1)
    return pl.pallas_call(
        functools.partial(_inproj_kernel, tm=tm),
        grid=(n,),
        in_specs=[
            pl.BlockSpec((tm, D_MODEL), lambda i: (nxt(i), 0)),
            _single((tm, D_MODEL), lambda i: (0, 0)),
            pl.BlockSpec((1, 6, D_MODEL), lambda i: (b0 + (nxt(i) * tm) // seq, 0, 0)),
            pl.BlockSpec((1, 6, D_MODEL), lambda i: (b0, 0, 0)),
            pl.BlockSpec((None, 1, D_MODEL), lambda i: (layer, 0, 0)),
            _single((None, D_MODEL, IN_PAD), lambda i: (layer, 0, 0)),
        ],
        out_specs=pl.BlockSpec((tm, IN_PAD), lambda i: (i, 0)),
        out_shape=jax.ShapeDtypeStruct((t, IN_PAD), BF16),
        scratch_shapes=[pltpu.VMEM((tm, D_MODEL), BF16), pltpu.VMEM((tm, D_MODEL), BF16)],
        compiler_params=_params(("arbitrary",)),
        name="in_proj",
    )(x, x, mod_l, mod_l, g, w_in_b)


def _gelu(x):
    return 0.5 * x * (1.0 + lax.erf(x * (1.0 / math.sqrt(2.0))))


def _gmlp_kernel(u_ref, v_ref, g_ref, ws_ref, bias_ref, o_ref, *, tm):
    g = g_ref[...]
    for n in range(tm // CHUNK):
        rows = slice(n * CHUNK, (n + 1) * CHUNK)
        u = _gelu(u_ref[rows, :].astype(F32))
        v = _gelu(v_ref[rows, :].astype(F32))
        vc = v - jnp.mean(v, axis=-1, keepdims=True)
        vn = (vc * lax.rsqrt(jnp.mean(vc * vc, axis=-1, keepdims=True) + EPS) * g).astype(BF16)
        for h in range(A_HEADS):
            cols = slice(h * LANES, (h + 1) * LANES)
            mixed = jnp.dot(ws_ref[h], vn[:, cols], preferred_element_type=F32) + bias_ref[:, cols]
            o_ref[rows, cols] = (u[:, cols] * mixed).astype(o_ref.dtype)


def _gmlp(proj, g, ws_b, bias_full, layer):
    t = proj.shape[0]
    tm = 512
    return pl.pallas_call(
        functools.partial(_gmlp_kernel, tm=tm),
        grid=(t // tm,),
        in_specs=[
            pl.BlockSpec((tm, W_GROUP), lambda i: (i, 0)),
            pl.BlockSpec((tm, W_GROUP), lambda i: (i, 1)),
            pl.BlockSpec((None, 1, W_GROUP), lambda i: (layer, 0, 0)),
            pl.BlockSpec((None, A_HEADS, CHUNK, CHUNK), lambda i: (layer, 0, 0, 0)),
            pl.BlockSpec((None, CHUNK, W_GROUP), lambda i: (layer, 0, 0)),
        ],
        out_specs=pl.BlockSpec((tm, W_GROUP), lambda i: (i, 0)),
        out_shape=jax.ShapeDtypeStruct((t, W_GROUP), BF16),
        compiler_params=_params(("parallel",)),
        name="gmlp",
    )(proj, proj, g, ws_b, bias_full)


def _hy_filter_kernel(f_ref, w1_ref, b1_ref, f1_ref, w2_ref, b2_ref, f2_ref, w3_ref, dec_ref, fwd_ref, o_ref,
                      k_s, prev_s):
    e = pl.program_id(0)

    @pl.when(e == 0)
    def _():
        prev_s[...] = jnp.zeros(prev_s.shape, F32)

    hp = lax.Precision.HIGHEST
    feats = f_ref[...]
    h = jnp.sin(f1_ref[...] * (jnp.dot(feats, w1_ref[...], precision=hp, preferred_element_type=F32) + b1_ref[...]))
    h = jnp.sin(f2_ref[...] * (jnp.dot(h, w2_ref[...], precision=hp, preferred_element_type=F32) + b2_ref[...]))
    half = feats.shape[0]
    decay = jnp.abs(dec_ref[0])
    for grp in range(2):
        k = jnp.dot(h, w3_ref[0, grp], precision=hp, preferred_element_type=F32)
        tcol = feats[:, grp * FILT_H:grp * FILT_H + 1]
        valid = feats[:, (grp + 1) * FILT_H - 1:(grp + 1) * FILT_H]
        k_s[grp * half:(grp + 1) * half, :] = k * jnp.exp(-tcol * decay) * valid

    odd = (lax.broadcasted_iota(jnp.int32, (fwd_ref.shape[0], W_GROUP), 0) & 1) == 1
    for order in range(2):
        taps = k_s[:, order * W_GROUP:(order + 1) * W_GROUP].astype(BF16)
        cur = jnp.dot(fwd_ref[...], taps, preferred_element_type=F32)
        prev = prev_s[order]
        o_ref[order, 0] = cur + jnp.where(odd, -prev, prev)
        prev_s[order] = cur


def _hy_filters(feats, w1p, b1p, f1p, w2p, b2p, f2p, w3d, decd, fwd_b, layer, seq):
    p = HY_P
    nb = seq // p
    c2 = lambda e: (layer, 0, 0)
    return pl.pallas_call(
        _hy_filter_kernel,
        grid=(2 * nb,),
        in_specs=[
            pl.BlockSpec((p // 2, LANES), lambda e: (e, 0)),
            pl.BlockSpec((None, LANES, LANES), c2),
            pl.BlockSpec((None, 1, LANES), c2),
            pl.BlockSpec((None, 1, LANES), c2),
            pl.BlockSpec((None, LANES, LANES), c2),
            pl.BlockSpec((None, 1, LANES), c2),
            pl.BlockSpec((None, 1, LANES), c2),
            pl.BlockSpec((None, 1, 2, LANES, 2 * W_GROUP), lambda e: (layer, jnp.where(e < nb, 1, 0), 0, 0, 0)),
            pl.BlockSpec((None, 1, 1, 2 * W_GROUP), lambda e: (layer, jnp.where(e < nb, 1, 0), 0, 0)),
            pl.BlockSpec((2 * p, p), lambda e: (0, 0)),
        ],
        out_specs=pl.BlockSpec((2, 1, 2 * p, W_GROUP), lambda e: (0, jnp.maximum(e - 1, 0), 0, 0)),
        out_shape=jax.ShapeDtypeStruct((2, 2 * nb - 1, 2 * p, W_GROUP), F32),
        scratch_shapes=[pltpu.VMEM((p, 2 * W_GROUP), F32), pltpu.VMEM((2, 2 * p, W_GROUP), F32)],
        compiler_params=_params(("arbitrary",)),
        name="hy_filter",
    )(feats, w1p, b1p, f1p, w2p, b2p, f2p, w3d, decd, fwd_b)


def _dwconv_rows(src_ref, r0, rc, total, w, b):
    halo = BF16_ROWS
    xc = src_ref[r0:r0 + rc, :].astype(F32)
    zeros = jnp.zeros((halo, xc.shape[1]), F32)
    prev = src_ref[r0 - halo:r0, :].astype(F32) if r0 > 0 else zeros
    nxt = src_ref[r0 + rc:r0 + rc + halo, :].astype(F32) if r0 + rc < total else zeros
    cat = jnp.concatenate([prev, xc, nxt], axis=0)
    n = rc + 2 * halo
    dn = pltpu.roll(cat, 1, 0)[halo:halo + rc]
    up = pltpu.roll(cat, n - 1, 0)[halo:halo + rc]
    return dn * w[0:1] + xc * w[1:2] + up * w[2:3] + b


def _hyena_kernel(p1_ref, p2_ref, pv_ref, cw_ref, cb_ref, skip_ref, h_ref, fwd_ref, inv_ref, o_ref,
                  vz, x1z, x2z, vhat0, vhat1, yhat_a, yhat_b, *, seq, p):
    nb = seq // p
    rc = 256
    rm = 32
    yhat = (yhat_a, yhat_b)

    def conv_block(src_ref, dst, part, i):
        for r0 in range(i * p, (i + 1) * p, rc):
            dst[r0:r0 + rc, :] = _dwconv_rows(src_ref, r0, rc, seq, cw_ref[part], cb_ref[part:part + 1, :])

    for i in range(nb):
        conv_block(pv_ref, vz, 2, i)
        vhat0[i] = jnp.dot(fwd_ref[...], vz[i * p:(i + 1) * p, :].astype(BF16), preferred_element_type=F32)
        conv_block(p1_ref, x1z, 0, i)
        conv_block(p2_ref, x2z, 1, i)

    def block_spectrum(order, i, vhat, dst):
        for r in range(p // rm):
            rows_re = slice(r * rm, (r + 1) * rm)
            rows_im = slice(p + r * rm, p + (r + 1) * rm)
            acc_r = None
            acc_i = None
            for ip in range(nb):
                d = i - ip + (nb - 1)
                hr = h_ref[order, d, rows_re, :]
                hi = h_ref[order, d, rows_im, :]
                ur = vhat[ip, rows_re, :]
                ui = vhat[ip, rows_im, :]
                if r == 0:
                    first = lax.broadcasted_iota(jnp.int32, hr.shape, 0) == 0
                    tr = hr * ur - jnp.where(first, 0.0, hi * ui)
                    ti = jnp.where(first, hi * ui, hr * ui + hi * ur)
                else:
                    tr = hr * ur - hi * ui
                    ti = hr * ui + hi * ur
                acc_r = tr if acc_r is None else acc_r + tr
                acc_i = ti if acc_i is None else acc_i + ti
            dst[rows_re, :] = acc_r.astype(BF16)
            dst[rows_im, :] = acc_i.astype(BF16)

    for order in range(2):
        gate = x1z if order == 0 else x2z
        vhat = vhat0 if order == 0 else vhat1
        skip = skip_ref[order:order + 1, :]
        for i in range(nb + 1):
            if i >= 1:
                y = jnp.dot(inv_ref[...], yhat[(i - 1) % 2][...], preferred_element_type=F32)
            if i < nb:
                block_spectrum(order, i, vhat, yhat[i % 2])
            if i >= 1:
                rows = slice((i - 1) * p, i * p)
                res = gate[rows, :] * (y + skip * vz[rows, :])
                if order == 0:
                    vz[rows, :] = res
                    vhat1[i - 1] = jnp.dot(fwd_ref[...], res.astype(BF16), preferred_element_type=F32)
                else:
                    o_ref[rows, :] = res.astype(o_ref.dtype)


def _hyena(proj, cw, cb, skip, hfilt, fwd_b, inv_b, layer, seq):
    t = proj.shape[0]
    p = HY_P
    nb = seq // p
    nct = W_GROUP // LANES
    base = IN_A // LANES
    return pl.pallas_call(
        functools.partial(_hyena_kernel, seq=seq, p=p),
        grid=(nct, t // seq),
        in_specs=[
            pl.BlockSpec((seq, LANES), lambda c, b: (b, base + c)),
            pl.BlockSpec((seq, LANES), lambda c, b: (b, base + nct + c)),
            pl.BlockSpec((seq, LANES), lambda c, b: (b, base + 2 * nct + c)),
            pl.BlockSpec((None, 3, 3, LANES), lambda c, b: (layer, 0, 0, c)),
            pl.BlockSpec((None, 3, LANES), lambda c, b: (layer, 0, c)),
            pl.BlockSpec((None, 2, LANES), lambda c, b: (layer, 0, c)),
            _single((2, 2 * nb - 1, 2 * p, LANES), lambda c, b: (0, 0, 0, c)),
            _single((2 * p, p), lambda c, b: (0, 0)),
            _single((p, 2 * p), lambda c, b: (0, 0)),
        ],
        out_specs=pl.BlockSpec((seq, LANES), lambda c, b: (b, c)),
        out_shape=jax.ShapeDtypeStruct((t, W_GROUP), BF16),
        scratch_shapes=[
            pltpu.VMEM((seq, LANES), F32),
            pltpu.VMEM((seq, LANES), F32),
            pltpu.VMEM((seq, LANES), F32),
            pltpu.VMEM((nb, 2 * p, LANES), F32),
            pltpu.VMEM((nb, 2 * p, LANES), F32),
            pltpu.VMEM((2 * p, LANES), BF16),
            pltpu.VMEM((2 * p, LANES), BF16),
        ],
        compiler_params=_params(("parallel", "parallel")),
        name="hyena",
    )(proj, proj, proj, cw, cb, skip, hfilt, fwd_b, inv_b)


def _rope_lanes(x, cos_t, sin_t, half):
    lane = lax.broadcasted_iota(jnp.int32, x.shape, 1)
    lower = (lane % (2 * half)) < half
    partner = jnp.where(lower, pltpu.roll(x, LANES - half, 1), pltpu.roll(x, half, 1))
    return x * cos_t + partner * sin_t


def _rms_rows(x, g, width):
    ss = jnp.sum(x * x, axis=-1, keepdims=True) * (1.0 / width)
    return x * lax.rsqrt(ss + EPS) * g


def _mla_prep_kernel(qa_ref, ckv_ref, kr_ref, cos_ref, sin_ref, qag_ref, kvg_ref, wq_ref, wkv_ref,
                     qn_ref, kn_ref, qo_ref, ko_ref, vo_ref):
    cos_t = cos_ref[...]
    sin_t = sin_ref[...]
    width = NOPE + ROPE_D
    scale = LOG2E / math.sqrt(width)
    qa = _rms_rows(qa_ref[...].astype(F32), qag_ref[...], Q_LORA).astype(BF16)
    q = jnp.dot(qa, wq_ref[...], preferred_element_type=F32)
    ckv = _rms_rows(ckv_ref[...].astype(F32), kvg_ref[...], KV_LORA).astype(BF16)
    kv = jnp.dot(ckv, wkv_ref[...], preferred_element_type=F32)
    kr = kr_ref[...].astype(F32)
    kr_ss = jnp.sum(kr * kr, axis=-1, keepdims=True)
    qn = qn_ref[...]
    kn = kn_ref[...]
    for h in range(D_HEADS):
        base = h * MLA_SLOT
        qh = _rms_rows(q[:, base:base + MLA_SLOT], qn, width)
        qo_ref[:, base:base + NOPE] = (qh[:, :NOPE] * scale).astype(BF16)
        qr = _rope_lanes(qh[:, NOPE:], cos_t, sin_t, ROPE_D // 4)
        qo_ref[:, base + NOPE:base + MLA_SLOT] = (qr * scale).astype(BF16)
        k_nope = kv[:, base:base + NOPE]
        ss = (jnp.sum(k_nope * k_nope, axis=-1, keepdims=True) + kr_ss) * (1.0 / width)
        inv = lax.rsqrt(ss + EPS)
        ko_ref[:, base:base + NOPE] = (k_nope * inv * kn[:, :NOPE]).astype(BF16)
        krn = _rope_lanes(kr * inv * kn[:, NOPE:], cos_t, sin_t, ROPE_D // 4)
        ko_ref[:, base + NOPE:base + MLA_SLOT] = krn.astype(BF16)
        vo_ref[:, h * V_HD:(h + 1) * V_HD] = kv[:, base + NOPE:base + NOPE + V_HD].astype(BF16)


def _mla_prep(proj, cos_t, sin_t, qag, kvg, wq_b, wkv_b, qn_p, kn_p, layer, seq):
    t = proj.shape[0]
    tm = 512
    c0 = IN_A + IN_B + IN_C
    g2 = lambda i: (layer, 0, 0)
    hw = D_HEADS * MLA_SLOT
    return pl.pallas_call(
        _mla_prep_kernel,
        grid=(t // tm,),
        in_specs=[
            pl.BlockSpec((tm, Q_LORA), lambda i: (i, c0 // Q_LORA)),
            pl.BlockSpec((tm, KV_LORA), lambda i: (i, (c0 + Q_LORA) // KV_LORA)),
            pl.BlockSpec((tm, LANES), lambda i: (i, (c0 + Q_LORA + KV_LORA) // LANES)),
            pl.BlockSpec((tm, LANES), lambda i: (i % (seq // tm), 0)),
            pl.BlockSpec((tm, LANES), lambda i: (i % (seq // tm), 0)),
            pl.BlockSpec((None, 1, Q_LORA), g2),
            pl.BlockSpec((None, 1, KV_LORA), g2),
            pl.BlockSpec((None, Q_LORA, hw), g2),
            pl.BlockSpec((None, KV_LORA, hw), g2),
            pl.BlockSpec((None, 1, MLA_SLOT), g2),
            pl.BlockSpec((None, 1, MLA_SLOT), g2),
        ],
        out_specs=[
            pl.BlockSpec((tm, hw), lambda i: (i, 0)),
            pl.BlockSpec((tm, hw), lambda i: (i, 0)),
            pl.BlockSpec((tm, D_HEADS * V_HD), lambda i: (i, 0)),
        ],
        out_shape=[
            jax.ShapeDtypeStruct((t, hw), BF16),
            jax.ShapeDtypeStruct((t, hw), BF16),
            jax.ShapeDtypeStruct((t, D_HEADS * V_HD), BF16),
        ],
        compiler_params=_params(("parallel",)),
        name="mla_prep",
    )(proj, proj, proj, cos_t, sin_t, qag, kvg, wq_b, wkv_b, qn_p, kn_p)


def _extend_values(v_ref, vext_ref, dv):
    vext_ref[:, :dv] = v_ref[...]
    vext_ref[:, dv:] = jnp.ones((v_ref.shape[0], dv), BF16)


def _softmax_chains(get_q, k, v, o_ref, tq, groups, dv, sub):
    chains = [(g, slice(r * sub, (r + 1) * sub)) for g in range(groups) for r in range(tq // sub)]

    def scores(c):
        g, rows = chains[c]
        return lax.dot_general(get_q(g, rows), k, (((1,), (1,)), ((), ())), preferred_element_type=F32)

    s = scores(0)
    for c, (g, rows) in enumerate(chains):
        s_next = scores(c + 1) if c + 1 < len(chains) else None
        m = jnp.max(s, axis=-1, keepdims=True)
        pexp = jnp.exp2(s - m).astype(BF16)
        o = jnp.dot(pexp, v, preferred_element_type=F32)
        o_ref[rows, g * dv:(g + 1) * dv] = (o[:, :dv] / o[:, dv:]).astype(o_ref.dtype)
        s = s_next


def _attn_kernel(q_ref, k_ref, v_ref, o_ref, vext_ref, *, groups, dk, dv, sub):
    @pl.when(pl.program_id(2) == 0)
    def _():
        _extend_values(v_ref, vext_ref, dv)

    get_q = lambda g, rows: q_ref[rows, g * dk:(g + 1) * dk]
    _softmax_chains(get_q, k_ref[...], vext_ref[...], o_ref, q_ref.shape[0], groups, dv, sub)


def _gqa_attn_kernel(q_ref, k_ref, v_ref, cq_ref, sq_ref, ck_ref, sk_ref, qg_ref, kg_ref, o_ref, kprep_ref, vext_ref,
                     *, sub):
    groups = C_HEADS // C_KV_HEADS
    half = C_HD // 4

    @pl.when(pl.program_id(2) == 0)
    def _():
        _extend_values(v_ref, vext_ref, C_HD)
        kn = _rms_rows(k_ref[...].astype(F32), kg_ref[...], C_HD)
        kprep_ref[...] = _rope_lanes(kn, ck_ref[...], sk_ref[...], half).astype(BF16)

    scale = LOG2E / math.sqrt(C_HD)
    qg = qg_ref[...]

    def get_q(g, rows):
        qn = _rms_rows(q_ref[rows, g * C_HD:(g + 1) * C_HD].astype(F32), qg, C_HD)
        return (_rope_lanes(qn, cq_ref[rows, :], sq_ref[rows, :], half) * scale).astype(BF16)

    _softmax_chains(get_q, kprep_ref[...], vext_ref[...], o_ref, q_ref.shape[0], groups, C_HD, sub)


def _gqa_attention(proj, cos_t, sin_t, qg, kg, layer, seq, sub):
    t = proj.shape[0]
    groups = C_HEADS // C_KV_HEADS
    tq = 2 * sub
    nq = seq // tq
    q_col0 = (IN_A + IN_B) // (groups * C_HD)
    k_col0 = (IN_A + IN_B + C_HEADS * C_HD) // C_HD
    g2 = lambda b, h, i: (layer, 0, 0)
    whole = lambda b, h, i: (0, 0)
    return pl.pallas_call(
        functools.partial(_gqa_attn_kernel, sub=sub),
        grid=(t // seq, C_KV_HEADS, nq),
        in_specs=[
            pl.BlockSpec((tq, groups * C_HD), lambda b, h, i: (b * nq + i, q_col0 + h)),
            pl.BlockSpec((seq, C_HD), lambda b, h, i: (b, k_col0 + h)),
            pl.BlockSpec((seq, C_HD), lambda b, h, i: (b, GQA_V_COL // C_HD + h)),
            pl.BlockSpec((tq, LANES), lambda b, h, i: (i, 0)),
            pl.BlockSpec((tq, LANES), lambda b, h, i: (i, 0)),
            pl.BlockSpec((seq, LANES), whole),
            pl.BlockSpec((seq, LANES), whole),
            pl.BlockSpec((None, 1, C_HD), g2),
            pl.BlockSpec((None, 1, C_HD), g2),
        ],
        out_specs=pl.BlockSpec((tq, groups * C_HD), lambda b, h, i: (b * nq + i, h)),
        out_shape=jax.ShapeDtypeStruct((t, C_HEADS * C_HD), BF16),
        scratch_shapes=[pltpu.VMEM((seq, C_HD), BF16), pltpu.VMEM((seq, 2 * C_HD), BF16)],
        compiler_params=_params(("parallel", "parallel", "arbitrary")),
        name="gqa_attn",
    )(proj, proj, proj, cos_t, sin_t, cos_t, sin_t, qg, kg)


def _attention(q, k, v, v_col0, seq, kv_heads, groups, dk, dv, tq, sub, name):
    t = q.shape[0]
    nq = seq // tq
    return pl.pallas_call(
        functools.partial(_attn_kernel, groups=groups, dk=dk, dv=dv, sub=sub),
        grid=(t // seq, kv_heads, nq),
        in_specs=[
            pl.BlockSpec((tq, groups * dk), lambda b, h, i: (b * nq + i, h)),
            pl.BlockSpec((seq, dk), lambda b, h, i: (b, h)),
            pl.BlockSpec((seq, dv), lambda b, h, i: (b, v_col0 + h)),
        ],
        out_specs=pl.BlockSpec((tq, groups * dv), lambda b, h, i: (b * nq + i, h)),
        out_shape=jax.ShapeDtypeStruct((t, kv_heads * groups * dv), BF16),
        scratch_shapes=[pltpu.VMEM((seq, 2 * dv), BF16)],
        compiler_params=_params(("parallel", "parallel", "arbitrary")),
        name=name,
    )(q, k, v)


def _outproj_kernel(ya_ref, yb_ref, yc_ref, yd_ref, x_ref, mod_ref, gg_ref, w_ref, o_ref):
    acc = None
    for gi, y_ref in enumerate((ya_ref, yb_ref, yc_ref, yd_ref)):
        rows = slice(gi * W_GROUP, (gi + 1) * W_GROUP)
        yn = _rms_rows(y_ref[...].astype(F32), gg_ref[:, rows], W_GROUP).astype(BF16)
        part = jnp.dot(yn, w_ref[rows, :], preferred_element_type=F32)
        acc = part if acc is None else acc + part
    o_ref[...] = x_ref[...] + mod_ref[0, 2:3, :] * acc


def _out_proj(ya, yb, yc, yd, x, mod_l, b0, seq, gg, w_out_b, layer):
    t = x.shape[0]
    tm = 512
    yspec = pl.BlockSpec((tm, W_GROUP), lambda i: (i, 0))
    return pl.pallas_call(
        _outproj_kernel,
        grid=(t // tm,),
        in_specs=[
            yspec, yspec, yspec, yspec,
            pl.BlockSpec((tm, D_MODEL), lambda i: (i, 0)),
            pl.BlockSpec((1, 6, D_MODEL), lambda i: (b0 + (i * tm) // seq, 0, 0)),
            pl.BlockSpec((None, 1, D_MODEL), lambda i: (layer, 0, 0)),
            _single((None, D_MODEL, D_MODEL), lambda i: (layer, 0, 0)),
        ],
        out_specs=pl.BlockSpec((tm, D_MODEL), lambda i: (i, 0)),
        out_shape=jax.ShapeDtypeStruct((t, D_MODEL), F32),
        compiler_params=_params(("parallel",)),
        name="out_proj",
    )(ya, yb, yc, yd, x, mod_l, gg, w_out_b)


def _ffn_kernel(x_ref, xp_ref, xn_ref, mod_ref, g_ref, wg_ref, wu_ref, cwg_ref, cwu_ref, cbg_ref, cbu_ref,
                wd_ref, o_ref, h_ref, *, tm, seq):
    i = pl.program_id(0)
    j = pl.program_id(1)
    nj = pl.num_programs(1)
    rc = NORM_ROWS
    halo = FFN_HALO

    @pl.when(j == 0)
    def _():
        shift = mod_ref[0, 3:4, :]
        gs = g_ref[...] * (1.0 + mod_ref[0, 4:5, :])
        has_prev = ((i * tm) % seq) != 0
        has_next = (((i + 1) * tm) % seq) != 0
        hp = _mod_norm_rows(xp_ref[...], gs, shift)
        hn = _mod_norm_rows(xn_ref[...], gs, shift)
        h_ref[pl.ds(0, halo), :] = jnp.where(has_prev, hp, 0.0).astype(BF16)
        h_ref[pl.ds(halo + tm, halo), :] = jnp.where(has_next, hn, 0.0).astype(BF16)

        chains = 8

        def body(c, carry):
            for k in range(chains):
                r0 = pl.multiple_of((c * chains + k) * rc, rc)
                h_ref[pl.ds(halo + r0, rc), :] = _mod_norm_rows(x_ref[pl.ds(r0, rc), :], gs, shift).astype(BF16)
            return carry

        lax.fori_loop(0, tm // (rc * chains), body, 0)
        o_ref[...] = jnp.zeros(o_ref.shape, F32)

    hfull = h_ref[...]
    n = tm + 2 * halo

    def conv(u, cw_ref, cb_ref):
        dn = pltpu.roll(u, 1, 0)[halo:halo + tm]
        up = pltpu.roll(u, n - 1, 0)[halo:halo + tm]
        return dn * cw_ref[0:1, :] + u[halo:halo + tm] * cw_ref[1:2, :] + up * cw_ref[2:3, :] + cb_ref[...]

    gate = conv(jnp.dot(hfull, wg_ref[...], preferred_element_type=F32), cwg_ref, cbg_ref)
    upv = conv(jnp.dot(hfull, wu_ref[...], preferred_element_type=F32), cwu_ref, cbu_ref)
    act = (jax.nn.silu(gate) * upv).astype(BF16)
    o_ref[...] += jnp.dot(act, wd_ref[...], preferred_element_type=F32)

    @pl.when(j == nj - 1)
    def _():
        o_ref[...] = x_ref[...] + mod_ref[0, 5:6, :] * o_ref[...]


def _ffn(x, mod_l, b0, seq, g, w_up_b, cw, cb, w_down_b, layer):
    t = x.shape[0]
    tm, tf = 1024, 512
    nf = D_FF // tf
    hb = tm // FFN_HALO
    last = t // FFN_HALO - 1
    return pl.pallas_call(
        functools.partial(_ffn_kernel, tm=tm, seq=seq),
        grid=(t // tm, nf),
        in_specs=[
            pl.BlockSpec((tm, D_MODEL), lambda i, j: (i, 0)),
            pl.BlockSpec((FFN_HALO, D_MODEL), lambda i, j: (jnp.maximum(i * hb - 1, 0), 0)),
            pl.BlockSpec((FFN_HALO, D_MODEL), lambda i, j: (jnp.minimum((i + 1) * hb, last), 0)),
            pl.BlockSpec((1, 6, D_MODEL), lambda i, j: (b0 + (i * tm) // seq, 0, 0)),
            pl.BlockSpec((None, 1, D_MODEL), lambda i, j: (layer, 0, 0)),
            pl.BlockSpec((None, D_MODEL, tf), lambda i, j: (layer, 0, j)),
            pl.BlockSpec((None, D_MODEL, tf), lambda i, j: (layer, 0, nf + j)),
            pl.BlockSpec((None, 3, tf), lambda i, j: (layer, 0, j)),
            pl.BlockSpec((None, 3, tf), lambda i, j: (layer, 0, nf + j)),
            pl.BlockSpec((None, 1, tf), lambda i, j: (layer, 0, j)),
            pl.BlockSpec((None, 1, tf), lambda i, j: (layer, 0, nf + j)),
            pl.BlockSpec((None, tf, D_MODEL), lambda i, j: (layer, j, 0)),
        ],
        out_specs=pl.BlockSpec((tm, D_MODEL), lambda i, j: (i, 0)),
        out_shape=jax.ShapeDtypeStruct((t, D_MODEL), F32),
        scratch_shapes=[pltpu.VMEM((tm + 2 * FFN_HALO, D_MODEL), BF16)],
        compiler_params=_params(("parallel", "arbitrary"), FFN_VMEM_LIMIT),
        name="ffn",
    )(x, x, x, mod_l, g, w_up_b, w_up_b, cw, cw, cb, cb, w_down_b)


def _axial_tables(seq, sec, lanes_used):
    pos = jnp.arange(seq, dtype=jnp.int32)
    row = (pos // GRID_W).astype(F32)
    col = (pos % GRID_W).astype(F32)
    inv = ROPE_THETA ** (-jnp.arange(0, sec, 2, dtype=F32) / sec)
    half = sec // 2
    lane = jnp.arange(LANES)
    in_use = lane < lanes_used
    which = (lane // sec) % 2
    freq = inv[lane % half]
    ang = jnp.where(which[None, :] == 0, row[:, None], col[:, None]) * freq[None, :]
    cos_t = jnp.where(in_use[None, :], jnp.cos(ang), 1.0)
    sign = jnp.where((lane % sec) < half, -1.0, 1.0)
    sin_t = jnp.where(in_use[None, :], jnp.sin(ang) * sign[None, :], 0.0)
    return cos_t.astype(F32), sin_t.astype(F32)


def _hy_filter_params(hy_w1, hy_b1, hy_f1, hy_w2, hy_b2, hy_f2, hy_w3, hy_decay):
    nl = hy_w1.shape[0]
    zero = jnp.zeros((nl, FILT_H, FILT_H), F32)
    bdiag = lambda w: jnp.concatenate([jnp.concatenate([w, zero], axis=2), jnp.concatenate([zero, w], axis=2)], axis=1)
    twice = lambda a: jnp.tile(a.reshape(nl, 1, FILT_H), (1, 1, 2))
    w1p = bdiag(jnp.pad(hy_w1, ((0, 0), (0, FILT_H - POS_EMB), (0, 0))))
    w2p = bdiag(hy_w2)
    w3 = hy_w3.reshape(nl, FILT_H, 2, 2, W_GROUP).transpose(0, 3, 1, 2, 4).reshape(nl, 2, FILT_H, 2 * W_GROUP)
    z3 = jnp.zeros_like(w3)
    w3d = jnp.stack([jnp.concatenate([w3, z3], axis=2), jnp.concatenate([z3, w3], axis=2)], axis=2)
    decd = hy_decay.reshape(nl, 2, 2, W_GROUP).transpose(0, 2, 1, 3).reshape(nl, 2, 1, 2 * W_GROUP)
    return w1p, twice(hy_b1), twice(hy_f1), w2p, twice(hy_b2), twice(hy_f2), w3d, decd


def _hyena_feats(seq):
    n = jnp.arange(seq, dtype=jnp.int32)
    posi = jnp.concatenate([seq - n, n])
    valid = (posi < seq).astype(F32)
    posi = jnp.minimum(posi, seq - 1)
    tl = jnp.linspace(0.0, 1.0, seq, dtype=F32)
    tlin = jnp.concatenate([tl[seq - 1:], tl[:0:-1], tl])
    bands = jnp.linspace(1e-4, POS_BANDS - 1, POS_BANDS, dtype=F32)
    ang = (2.0 * math.pi / seq) * posi.astype(F32)[:, None] * bands[None, :]
    feats = jnp.concatenate([tlin[:, None], jnp.cos(ang), -jnp.sin(ang)], axis=-1)
    pad = jnp.zeros((2 * seq, FILT_H - POS_EMB - 1), F32)
    f64 = jnp.concatenate([feats, pad, valid[:, None]], axis=-1)
    half = HY_P // 2
    return f64.reshape(-1, 2, half, FILT_H).transpose(0, 2, 1, 3).reshape(seq, 2 * FILT_H)


def _dft_mats(p):
    r = jnp.arange(p, dtype=jnp.int32)
    q = (r[:, None] * r[None, :]) % (2 * p)
    ang = q.astype(F32) * (math.pi / p)
    cosm = jnp.cos(ang)
    sinm = jnp.sin(ang)
    alt = jnp.where(r % 2 == 0, 1.0, -1.0).astype(F32)
    im_rows = jnp.where((r == 0)[:, None], alt[None, :], -sinm)
    fwd = jnp.concatenate([cosm, im_rows], axis=0)
    wre = jnp.where((r == 0)[None, :], 0.5, 1.0) / p
    inv_re = cosm * wre
    inv_im = jnp.where((r == 0)[None, :], alt[:, None] * (0.5 / p), -sinm / p)
    inv = jnp.concatenate([inv_re, inv_im], axis=1)
    return fwd.astype(BF16), inv.astype(BF16)


def kernel(x_prompt, x_sample, c_prompt, c_sample, ada_w, ada_b, norm1_g, w_in, gm_vnorm_g, gm_spatial_w, gm_spatial_b, hy_conv_w, hy_conv_b, hy_w1, hy_b1, hy_f1, hy_w2, hy_b2, hy_f2, hy_w3, hy_decay, hy_skip, gqa_qn_g, gqa_kn_g, mla_q_a_g, mla_w_q_b, mla_kv_a_g, mla_w_kv_b, mla_qn_g, mla_kn_g, group_norm_g, w_out, norm2_g, ffn_w_up, ffn_conv_w, ffn_conv_b, ffn_w_down):
    nl = DEPTH
    w_in_b = jnp.pad(w_in.astype(BF16), ((0, 0), (0, 0), (0, IN_PAD - IN_COLS)))
    w_out_b = w_out.astype(BF16)
    w_up_b = ffn_w_up.astype(BF16)
    w_down_b = ffn_w_down.astype(BF16)
    ws_b = gm_spatial_w.astype(BF16)
    gm_bias = jnp.broadcast_to(jnp.swapaxes(gm_spatial_b, 1, 2)[:, :, :, None],
                               (nl, CHUNK, A_HEADS, LANES)).reshape(nl, CHUNK, W_GROUP)
    row3 = lambda a: a.reshape(nl, 1, a.shape[-1])
    norm1 = row3(norm1_g)
    norm2 = row3(norm2_g)
    gm_g = row3(gm_vnorm_g)
    gg = row3(group_norm_g)
    hy_cw = hy_conv_w.reshape(nl, 3, 3, W_GROUP).transpose(0, 2, 1, 3)
    hy_cb = hy_conv_b.reshape(nl, 3, W_GROUP)
    hy_filter_params = _hy_filter_params(hy_w1, hy_b1, hy_f1, hy_w2, hy_b2, hy_f2, hy_w3, hy_decay)
    gqa_qg = row3(gqa_qn_g)
    gqa_kg = row3(gqa_kn_g)
    mla_qag = row3(mla_q_a_g)
    mla_kvg = row3(mla_kv_a_g)
    slot_pad = MLA_SLOT - NOPE - ROPE_D
    wq_b = jnp.pad(mla_w_q_b.reshape(nl, Q_LORA, D_HEADS, NOPE + ROPE_D),
                   ((0, 0), (0, 0), (0, 0), (0, slot_pad))).reshape(nl, Q_LORA, D_HEADS * MLA_SLOT).astype(BF16)
    wkv_b = mla_w_kv_b.astype(BF16)
    mla_qn = jnp.pad(row3(mla_qn_g), ((0, 0), (0, 0), (0, slot_pad)))
    mla_kn = jnp.pad(row3(mla_kn_g), ((0, 0), (0, 0), (0, slot_pad)))
    ffn_cb = row3(ffn_conv_b)

    nbp = x_prompt.shape[0]
    c_all = jnp.concatenate([c_prompt, c_sample], axis=0)
    mod = _ada_mod(c_all, ada_w, ada_b).reshape(nl, c_all.shape[0], 6, D_MODEL)

    fwd_b, inv_b = _dft_mats(HY_P)

    def trunk(x3, b0):
        bsz, seq, _ = x3.shape
        x = x3.reshape(bsz * seq, D_MODEL)
        gcos, gsin = _axial_tables(seq, C_HD // 2, C_HD)
        mcos, msin = _axial_tables(seq, ROPE_D // 2, ROPE_D)
        feats = _hyena_feats(seq)
        sub = ATTN_SCORE_ELEMS // seq
        for l in range(nl):
            mod_l = mod[l]
            proj = _in_proj(x, mod_l, b0, seq, norm1, w_in_b, l)
            ya = _gmlp(proj, gm_g, ws_b, gm_bias, l)
            hfilt = _hy_filters(feats, *hy_filter_params, fwd_b, l, seq)
            yb = _hyena(proj, hy_cw, hy_cb, hy_skip, hfilt, fwd_b, inv_b, l, seq)
            yc = _gqa_attention(proj, gcos, gsin, gqa_qg, gqa_kg, l, seq, sub)
            mq, mk, mv = _mla_prep(proj, mcos, msin, mla_qag, mla_kvg, wq_b, wkv_b, mla_qn, mla_kn, l, seq)
            yd = _attention(mq, mk, mv, 0, seq, D_HEADS, 1, MLA_SLOT, V_HD, 4 * sub, sub, "mla_attn")
            x = _out_proj(ya, yb, yc, yd, x, mod_l, b0, seq, gg, w_out_b, l)
            x = _ffn(x, mod_l, b0, seq, norm2, w_up_b, ffn_conv_w, ffn_cb, w_down_b, l)
        return x.reshape(bsz, seq, D_MODEL)

    return trunk(x_prompt, 0), trunk(x_sample, nbp)
```

```python
import functools
import math

import jax
import jax.numpy as jnp
from jax import lax
from jax.experimental import pallas as pl
from jax.experimental.pallas import tpu as pltpu

F32 = jnp.float32
BF16 = jnp.bfloat16

D_MODEL = 2048
DEPTH = 4
GRID_W = 64
CHUNK = 128
W_GROUP = 512
A_HEADS = 4
POS_BANDS = 16
POS_EMB = 2 * POS_BANDS + 1
FILT_H = 64
C_HEADS = 4
C_KV_HEADS = 2
C_HD = 128
D_HEADS = 4
Q_LORA = 512
KV_LORA = 256
NOPE = 128
ROPE_D = 64
V_HD = 128
ROPE_THETA = 10000.0
D_FF = 5632
EPS = 1e-6
LOG2E = 1.4426950408889634
IN_A = 1024
IN_B = 1536
IN_C = 1024
IN_D = 832
IN_COLS = IN_A + IN_B + IN_C + IN_D
GQA_V_COL = IN_A + IN_B + (C_HEADS + C_KV_HEADS) * C_HD
IN_PAD = 4608

LANES = 128
BF16_ROWS = 16
NORM_ROWS = BF16_ROWS
VMEM_LIMIT = 56 * 1024 * 1024
FFN_VMEM_LIMIT = 60 * 1024 * 1024

ATTN_SCORE_ELEMS = 256 * 4096
HY_P = 512
FFN_HALO = BF16_ROWS
MLA_SLOT = 256


def _params(sem, vmem=VMEM_LIMIT):
    return pltpu.CompilerParams(dimension_semantics=sem, vmem_limit_bytes=vmem)


def _single(block_shape, index_map):
    return pl.BlockSpec(block_shape, index_map, pipeline_mode=pl.Buffered(1))


def _ada_kernel(c_ref, w_ref, b_ref, o_ref):
    s = jax.nn.silu(c_ref[...]).astype(BF16)
    o_ref[0] = jnp.dot(s, w_ref[0].astype(BF16), preferred_element_type=F32) + b_ref[0]


def _ada_mod(c_all, ada_w, ada_b):
    nb = c_all.shape[0]
    tn = 1024
    return pl.pallas_call(
        _ada_kernel,
        grid=(DEPTH, 6 * D_MODEL // tn),
        in_specs=[
            pl.BlockSpec((nb, D_MODEL), lambda l, j: (0, 0)),
            pl.BlockSpec((1, D_MODEL, tn), lambda l, j: (l, 0, j)),
            pl.BlockSpec((1, 1, tn), lambda l, j: (l, 0, j)),
        ],
        out_specs=pl.BlockSpec((1, nb, tn), lambda l, j: (l, 0, j)),
        out_shape=jax.ShapeDtypeStruct((DEPTH, nb, 6 * D_MODEL), F32),
        compiler_params=_params(("parallel", "parallel")),
        name="ada_mod",
    )(c_all, ada_w, ada_b.reshape(DEPTH, 1, 6 * D_MODEL))


def _mod_norm_rows(x, gs, shift):
    ms = jnp.mean(x * x, axis=-1, keepdims=True)
    return (x * lax.rsqrt(ms + EPS)) * gs + shift


def _inproj_kernel(xn_ref, x0_ref, modn_ref, mod0_ref, g_ref, w_ref, o_ref, ha_ref, hb_ref, *, tm):
    i = pl.program_id(0)
    rc = NORM_ROWS
    g = g_ref[...]

    def fill(x_ref, mod_ref, dst_ref):
        shift = mod_ref[0, 0:1, :]
        gs = g * (1.0 + mod_ref[0, 1:2, :])
        for c in range(tm // rc):
            rows = slice(c * rc, (c + 1) * rc)
            dst_ref[rows, :] = _mod_norm_rows(x_ref[rows, :], gs, shift).astype(BF16)

    @pl.when(i == 0)
    def _():
        fill(x0_ref, mod0_ref, ha_ref)

    def step(cur_ref, nxt_ref):
        fill(xn_ref, modn_ref, nxt_ref)
        o_ref[...] = jnp.dot(cur_ref[...], w_ref[...], preferred_element_type=F32).astype(o_ref.dtype)

    @pl.when(i % 2 == 0)
    def _():
        step(ha_ref, hb_ref)

    @pl.when(i % 2 == 1)
    def _():
        step(hb_ref, ha_ref)


def _in_proj(x, mod_l, b0, seq, g, w_in_b, layer):
    t = x.shape[0]
    tm = 512
    n = t // tm
    nxt = lambda i: jnp.minimum(i + 1, n - 1)
    return pl.pallas_call(
        functools.partial(_inproj_kernel, tm=tm),
        grid=(n,),
        in_specs=[
            pl.BlockSpec((tm, D_MODEL), lambda i: (nxt(i), 0)),
            _single((tm, D_MODEL), lambda i: (0, 0)),
            pl.BlockSpec((1, 6, D_MODEL), lambda i: (b0 + (nxt(i) * tm) // seq, 0, 0)),
            pl.BlockSpec((1, 6, D_MODEL), lambda i: (b0, 0, 0)),
            pl.BlockSpec((None, 1, D_MODEL), lambda i: (layer, 0, 0)),
            _single((None, D_MODEL, IN_PAD), lambda i: (layer, 0, 0)),
        ],
        out_specs=pl.BlockSpec((tm, IN_PAD), lambda i: (i, 0)),
        out_shape=jax.ShapeDtypeStruct((t, IN_PAD), BF16),
        scratch_shapes=[pltpu.VMEM((tm, D_MODEL), BF16), pltpu.VMEM((tm, D_MODEL), BF16)],
        compiler_params=_params(("arbitrary",)),
        name="in_proj",
    )(x, x, mod_l, mod_l, g, w_in_b)


def _gelu(x):
    return 0.5 * x * (1.0 + lax.erf(x * (1.0 / math.sqrt(2.0))))


def _gmlp_kernel(u_ref, v_ref, g_ref, ws_ref, bias_ref, o_ref, *, tm):
    g = g_ref[...]
    for n in range(tm // CHUNK):
        rows = slice(n * CHUNK, (n + 1) * CHUNK)
        u = _gelu(u_ref[rows, :].astype(F32))
        v = _gelu(v_ref[rows, :].astype(F32))
        vc = v - jnp.mean(v, axis=-1, keepdims=True)
        vn = (vc * lax.rsqrt(jnp.mean(vc * vc, axis=-1, keepdims=True) + EPS) * g).astype(BF16)
        for h in range(A_HEADS):
            cols = slice(h * LANES, (h + 1) * LANES)
            mixed = jnp.dot(ws_ref[h], vn[:, cols], preferred_element_type=F32) + bias_ref[:, cols]
            o_ref[rows, cols] = (u[:, cols] * mixed).astype(o_ref.dtype)


def _gmlp(proj, g, ws_b, bias_full, layer):
    t = proj.shape[0]
    tm = 512
    return pl.pallas_call(
        functools.partial(_gmlp_kernel, tm=tm),
        grid=(t // tm,),
        in_specs=[
            pl.BlockSpec((tm, W_GROUP), lambda i: (i, 0)),
            pl.BlockSpec((tm, W_GROUP), lambda i: (i, 1)),
            pl.BlockSpec((None, 1, W_GROUP), lambda i: (layer, 0, 0)),
            pl.BlockSpec((None, A_HEADS, CHUNK, CHUNK), lambda i: (layer, 0, 0, 0)),
            pl.BlockSpec((None, CHUNK, W_GROUP), lambda i: (layer, 0, 0)),
        ],
        out_specs=pl.BlockSpec((tm, W_GROUP), lambda i: (i, 0)),
        out_shape=jax.ShapeDtypeStruct((t, W_GROUP), BF16),
        compiler_params=_params(("parallel",)),
        name="gmlp",
    )(proj, proj, g, ws_b, bias_full)


def _hy_filter_kernel(f_ref, w1_ref, b1_ref, f1_ref, w2_ref, b2_ref, f2_ref, w3_ref, dec_ref, fwd_ref, o_ref,
                      k_s, prev_s):
    e = pl.program_id(0)

    @pl.when(e == 0)
    def _():
        prev_s[...] = jnp.zeros(prev_s.shape, F32)

    hp = lax.Precision.HIGHEST
    feats = f_ref[...]
    h = jnp.sin(f1_ref[...] * (jnp.dot(feats, w1_ref[...], precision=hp, preferred_element_type=F32) + b1_ref[...]))
    h = jnp.sin(f2_ref[...] * (jnp.dot(h, w2_ref[...], precision=hp, preferred_element_type=F32) + b2_ref[...]))
    half = feats.shape[0]
    decay = jnp.abs(dec_ref[0])
    for grp in range(2):
        k = jnp.dot(h, w3_ref[0, grp], precision=hp, preferred_element_type=F32)
        tcol = feats[:, grp * FILT_H:grp * FILT_H + 1]
        valid = feats[:, (grp + 1) * FILT_H - 1:(grp + 1) * FILT_H]
        k_s[grp * half:(grp + 1) * half, :] = k * jnp.exp(-tcol * decay) * valid

    odd = (lax.broadcasted_iota(jnp.int32, (fwd_ref.shape[0], W_GROUP), 0) & 1) == 1
    for order in range(2):
        taps = k_s[:, order * W_GROUP:(order + 1) * W_GROUP].astype(BF16)
        cur = jnp.dot(fwd_ref[...], taps, preferred_element_type=F32)
        prev = prev_s[order]
        o_ref[order, 0] = cur + jnp.where(odd, -prev, prev)
        prev_s[order] = cur


def _hy_filters(feats, w1p, b1p, f1p, w2p, b2p, f2p, w3d, decd, fwd_b, layer, seq):
    p = HY_P
    nb = seq // p
    c2 = lambda e: (layer, 0, 0)
    return pl.pallas_call(
        _hy_filter_kernel,
        grid=(2 * nb,),
        in_specs=[
            pl.BlockSpec((p // 2, LANES), lambda e: (e, 0)),
            pl.BlockSpec((None, LANES, LANES), c2),
            pl.BlockSpec((None, 1, LANES), c2),
            pl.BlockSpec((None, 1, LANES), c2),
            pl.BlockSpec((None, LANES, LANES), c2),
            pl.BlockSpec((None, 1, LANES), c2),
            pl.BlockSpec((None, 1, LANES), c2),
            pl.BlockSpec((None, 1, 2, LANES, 2 * W_GROUP), lambda e: (layer, jnp.where(e < nb, 1, 0), 0, 0, 0)),
            pl.BlockSpec((None, 1, 1, 2 * W_GROUP), lambda e: (layer, jnp.where(e < nb, 1, 0), 0, 0)),
            pl.BlockSpec((2 * p, p), lambda e: (0, 0)),
        ],
        out_specs=pl.BlockSpec((2, 1, 2 * p, W_GROUP), lambda e: (0, jnp.maximum(e - 1, 0), 0, 0)),
        out_shape=jax.ShapeDtypeStruct((2, 2 * nb - 1, 2 * p, W_GROUP), F32),
        scratch_shapes=[pltpu.VMEM((p, 2 * W_GROUP), F32), pltpu.VMEM((2, 2 * p, W_GROUP), F32)],
        compiler_params=_params(("arbitrary",)),
        name="hy_filter",
    )(feats, w1p, b1p, f1p, w2p, b2p, f2p, w3d, decd, fwd_b)


def _dwconv_rows(src_ref, r0, rc, total, w, b):
    halo = BF16_ROWS
    xc = src_ref[r0:r0 + rc, :].astype(F32)
    zeros = jnp.zeros((halo, xc.shape[1]), F32)
    prev = src_ref[r0 - halo:r0, :].astype(F32) if r0 > 0 else zeros
    nxt = src_ref[r0 + rc:r0 + rc + halo, :].astype(F32) if r0 + rc < total else zeros
    cat = jnp.concatenate([prev, xc, nxt], axis=0)
    n = rc + 2 * halo
    dn = pltpu.roll(cat, 1, 0)[halo:halo + rc]
    up = pltpu.roll(cat, n - 1, 0)[halo:halo + rc]
    return dn * w[0:1] + xc * w[1:2] + up * w[2:3] + b


def _hyena_kernel(p1_ref, p2_ref, pv_ref, cw_ref, cb_ref, skip_ref, h_ref, fwd_ref, inv_ref, o_ref,
                  vz, x1z, x2z, vhat0, vhat1, yhat_a, yhat_b, *, seq, p):
    nb = seq // p
    rc = 256
    rm = 32
    yhat = (yhat_a, yhat_b)

    def conv_block(src_ref, dst, part, i):
        for r0 in range(i * p, (i + 1) * p, rc):
            dst[r0:r0 + rc, :] = _dwconv_rows(src_ref, r0, rc, seq, cw_ref[part], cb_ref[part:part + 1, :])

    for i in range(nb):
        conv_block(pv_ref, vz, 2, i)
        vhat0[i] = jnp.dot(fwd_ref[...], vz[i * p:(i + 1) * p, :].astype(BF16), preferred_element_type=F32)
        conv_block(p1_ref, x1z, 0, i)
        conv_block(p2_ref, x2z, 1, i)

    def block_spectrum(order, i, vhat, dst):
        for r in range(p // rm):
            rows_re = slice(r * rm, (r + 1) * rm)
            rows_im = slice(p + r * rm, p + (r + 1) * rm)
            acc_r = None
            acc_i = None
            for ip in range(nb):
                d = i - ip + (nb - 1)
                hr = h_ref[order, d, rows_re, :]
                hi = h_ref[order, d, rows_im, :]
                ur = vhat[ip, rows_re, :]
                ui = vhat[ip, rows_im, :]
                if r == 0:
                    first = lax.broadcasted_iota(jnp.int32, hr.shape, 0) == 0
                    tr = hr * ur - jnp.where(first, 0.0, hi * ui)
                    ti = jnp.where(first, hi * ui, hr * ui + hi * ur)
                else:
                    tr = hr * ur - hi * ui
                    ti = hr * ui + hi * ur
                acc_r = tr if acc_r is None else acc_r + tr
                acc_i = ti if acc_i is None else acc_i + ti
            dst[rows_re, :] = acc_r.astype(BF16)
            dst[rows_im, :] = acc_i.astype(BF16)

    for order in range(2):
        gate = x1z if order == 0 else x2z
        vhat = vhat0 if order == 0 else vhat1
        skip = skip_ref[order:order + 1, :]
        for i in range(nb + 1):
            if i >= 1:
                y = jnp.dot(inv_ref[...], yhat[(i - 1) % 2][...], preferred_element_type=F32)
            if i < nb:
                block_spectrum(order, i, vhat, yhat[i % 2])
            if i >= 1:
                rows = slice((i - 1) * p, i * p)
                res = gate[rows, :] * (y + skip * vz[rows, :])
                if order == 0:
                    vz[rows, :] = res
                    vhat1[i - 1] = jnp.dot(fwd_ref[...], res.astype(BF16), preferred_element_type=F32)
                else:
                    o_ref[rows, :] = res.astype(o_ref.dtype)


def _hyena(proj, cw, cb, skip, hfilt, fwd_b, inv_b, layer, seq):
    t = proj.shape[0]
    p = HY_P
    nb = seq // p
    nct = W_GROUP // LANES
    base = IN_A // LANES
    return pl.pallas_call(
        functools.partial(_hyena_kernel, seq=seq, p=p),
        grid=(nct, t // seq),
        in_specs=[
            pl.BlockSpec((seq, LANES), lambda c, b: (b, base + c)),
            pl.BlockSpec((seq, LANES), lambda c, b: (b, base + nct + c)),
            pl.BlockSpec((seq, LANES), lambda c, b: (b, base + 2 * nct + c)),
            pl.BlockSpec((None, 3, 3, LANES), lambda c, b: (layer, 0, 0, c)),
            pl.BlockSpec((None, 3, LANES), lambda c, b: (layer, 0, c)),
            pl.BlockSpec((None, 2, LANES), lambda c, b: (layer, 0, c)),
            _single((2, 2 * nb - 1, 2 * p, LANES), lambda c, b: (0, 0, 0, c)),
            _single((2 * p, p), lambda c, b: (0, 0)),
            _single((p, 2 * p), lambda c, b: (0, 0)),
        ],
        out_specs=pl.BlockSpec((seq, LANES), lambda c, b: (b, c)),
        out_shape=jax.ShapeDtypeStruct((t, W_GROUP), BF16),
        scratch_shapes=[
            pltpu.VMEM((seq, LANES), F32),
            pltpu.VMEM((seq, LANES), F32),
            pltpu.VMEM((seq, LANES), F32),
            pltpu.VMEM((nb, 2 * p, LANES), F32),
            pltpu.VMEM((nb, 2 * p, LANES), F32),
            pltpu.VMEM((2 * p, LANES), BF16),
            pltpu.VMEM((2 * p, LANES), BF16),
        ],
        compiler_params=_params(("parallel", "parallel")),
        name="hyena",
    )(proj, proj, proj, cw, cb, skip, hfilt, fwd_b, inv_b)


def _rope_lanes(x, cos_t, sin_t, half):
    lane = lax.broadcasted_iota(jnp.int32, x.shape, 1)
    lower = (lane % (2 * half)) < half
    partner = jnp.where(lower, pltpu.roll(x, LANES - half, 1), pltpu.roll(x, half, 1))
    return x * cos_t + partner * sin_t


def _rms_rows(x, g, width):
    ss = jnp.sum(x * x, axis=-1, keepdims=True) * (1.0 / width)
    return x * lax.rsqrt(ss + EPS) * g


def _mla_prep_kernel(qa_ref, ckv_ref, kr_ref, cos_ref, sin_ref, qag_ref, kvg_ref, wq_ref, wkv_ref,
                     qn_ref, kn_ref, qo_ref, ko_ref, vo_ref):
    cos_t = cos_ref[...]
    sin_t = sin_ref[...]
    width = NOPE + ROPE_D
    scale = LOG2E / math.sqrt(width)
    qa = _rms_rows(qa_ref[...].astype(F32), qag_ref[...], Q_LORA).astype(BF16)
    q = jnp.dot(qa, wq_ref[...], preferred_element_type=F32)
    ckv = _rms_rows(ckv_ref[...].astype(F32), kvg_ref[...], KV_LORA).astype(BF16)
    kv = jnp.dot(ckv, wkv_ref[...], preferred_element_type=F32)
    kr = kr_ref[...].astype(F32)
    kr_ss = jnp.sum(kr * kr, axis=-1, keepdims=True)
    qn = qn_ref[...]
    kn = kn_ref[...]
    for h in range(D_HEADS):
        base = h * MLA_SLOT
        qh = _rms_rows(q[:, base:base + MLA_SLOT], qn, width)
        qo_ref[:, base:base + NOPE] = (qh[:, :NOPE] * scale).astype(BF16)
        qr = _rope_lanes(qh[:, NOPE:], cos_t, sin_t, ROPE_D // 4)
        qo_ref[:, base + NOPE:base + MLA_SLOT] = (qr * scale).astype(BF16)
        k_nope = kv[:, base:base + NOPE]
        ss = (jnp.sum(k_nope * k_nope, axis=-1, keepdims=True) + kr_ss) * (1.0 / width)
        inv = lax.rsqrt(ss + EPS)
        ko_ref[:, base:base + NOPE] = (k_nope * inv * kn[:, :NOPE]).astype(BF16)
        krn = _rope_lanes(kr * inv * kn[:, NOPE:], cos_t, sin_t, ROPE_D // 4)
        ko_ref[:, base + NOPE:base + MLA_SLOT] = krn.astype(BF16)
        vo_ref[:, h * V_HD:(h + 1) * V_HD] = kv[:, base + NOPE:base + NOPE + V_HD].astype(BF16)


def _mla_prep(proj, cos_t, sin_t, qag, kvg, wq_b, wkv_b, qn_p, kn_p, layer, seq):
    t = proj.shape[0]
    tm = 512
    c0 = IN_A + IN_B + IN_C
    g2 = lambda i: (layer, 0, 0)
    hw = D_HEADS * MLA_SLOT
    return pl.pallas_call(
        _mla_prep_kernel,
        grid=(t // tm,),
        in_specs=[
            pl.BlockSpec((tm, Q_LORA), lambda i: (i, c0 // Q_LORA)),
            pl.BlockSpec((tm, KV_LORA), lambda i: (i, (c0 + Q_LORA) // KV_LORA)),
            pl.BlockSpec((tm, LANES), lambda i: (i, (c0 + Q_LORA + KV_LORA) // LANES)),
            pl.BlockSpec((tm, LANES), lambda i: (i % (seq // tm), 0)),
            pl.BlockSpec((tm, LANES), lambda i: (i % (seq // tm), 0)),
            pl.BlockSpec((None, 1, Q_LORA), g2),
            pl.BlockSpec((None, 1, KV_LORA), g2),
            pl.BlockSpec((None, Q_LORA, hw), g2),
            pl.BlockSpec((None, KV_LORA, hw), g2),
            pl.BlockSpec((None, 1, MLA_SLOT), g2),
            pl.BlockSpec((None, 1, MLA_SLOT), g2),
        ],
        out_specs=[
            pl.BlockSpec((tm, hw), lambda i: (i, 0)),
            pl.BlockSpec((tm, hw), lambda i: (i, 0)),
            pl.BlockSpec((tm, D_HEADS * V_HD), lambda i: (i, 0)),
        ],
        out_shape=[
            jax.ShapeDtypeStruct((t, hw), BF16),
            jax.ShapeDtypeStruct((t, hw), BF16),
            jax.ShapeDtypeStruct((t, D_HEADS * V_HD), BF16),
        ],
        compiler_params=_params(("parallel",)),
        name="mla_prep",
    )(proj, proj, proj, cos_t, sin_t, qag, kvg, wq_b, wkv_b, qn_p, kn_p)


def _extend_values(v_ref, vext_ref, dv):
    vext_ref[:, :dv] = v_ref[...]
    vext_ref[:, dv:] = jnp.ones((v_ref.shape[0], dv), BF16)


def _softmax_chains(get_q, k, v, o_ref, tq, groups, dv, sub):
    chains = [(g, slice(r * sub, (r + 1) * sub)) for g in range(groups) for r in range(tq // sub)]

    def scores(c):
        g, rows = chains[c]
        return lax.dot_general(get_q(g, rows), k, (((1,), (1,)), ((), ())), preferred_element_type=F32)

    s = scores(0)
    for c, (g, rows) in enumerate(chains):
        s_next = scores(c + 1) if c + 1 < len(chains) else None
        m = jnp.max(s, axis=-1, keepdims=True)
        pexp = jnp.exp2(s - m).astype(BF16)
        o = jnp.dot(pexp, v, preferred_element_type=F32)
        o_ref[rows, g * dv:(g + 1) * dv] = (o[:, :dv] / o[:, dv:]).astype(o_ref.dtype)
        s = s_next


def _attn_kernel(q_ref, k_ref, v_ref, o_ref, vext_ref, *, groups, dk, dv, sub):
    @pl.when(pl.program_id(2) == 0)
    def _():
        _extend_values(v_ref, vext_ref, dv)

    get_q = lambda g, rows: q_ref[rows, g * dk:(g + 1) * dk]
    _softmax_chains(get_q, k_ref[...], vext_ref[...], o_ref, q_ref.shape[0], groups, dv, sub)


def _gqa_attn_kernel(q_ref, k_ref, v_ref, cq_ref, sq_ref, ck_ref, sk_ref, qg_ref, kg_ref, o_ref, kprep_ref, vext_ref,
                     *, sub):
    groups = C_HEADS // C_KV_HEADS
    half = C_HD // 4

    @pl.when(pl.program_id(2) == 0)
    def _():
        _extend_values(v_ref, vext_ref, C_HD)
        kn = _rms_rows(k_ref[...].astype(F32), kg_ref[...], C_HD)
        kprep_ref[...] = _rope_lanes(kn, ck_ref[...], sk_ref[...], half).astype(BF16)

    scale = LOG2E / math.sqrt(C_HD)
    qg = qg_ref[...]

    def get_q(g, rows):
        qn = _rms_rows(q_ref[rows, g * C_HD:(g + 1) * C_HD].astype(F32), qg, C_HD)
        return (_rope_lanes(qn, cq_ref[rows, :], sq_ref[rows, :], half) * scale).astype(BF16)

    _softmax_chains(get_q, kprep_ref[...], vext_ref[...], o_ref, q_ref.shape[0], groups, C_HD, sub)


def _gqa_attention(proj, cos_t, sin_t, qg, kg, layer, seq, sub):
    t = proj.shape[0]
    groups = C_HEADS // C_KV_HEADS
    tq = 2 * sub
    nq = seq // tq
    q_col0 = (IN_A + IN_B) // (groups * C_HD)
    k_col0 = (IN_A + IN_B + C_HEADS * C_HD) // C_HD
    g2 = lambda b, h, i: (layer, 0, 0)
    whole = lambda b, h, i: (0, 0)
    return pl.pallas_call(
        functools.partial(_gqa_attn_kernel, sub=sub),
        grid=(t // seq, C_KV_HEADS, nq),
        in_specs=[
            pl.BlockSpec((tq, groups * C_HD), lambda b, h, i: (b * nq + i, q_col0 + h)),
            pl.BlockSpec((seq, C_HD), lambda b, h, i: (b, k_col0 + h)),
            pl.BlockSpec((seq, C_HD), lambda b, h, i: (b, GQA_V_COL // C_HD + h)),
            pl.BlockSpec((tq, LANES), lambda b, h, i: (i, 0)),
            pl.BlockSpec((tq, LANES), lambda b, h, i: (i, 0)),
            pl.BlockSpec((seq, LANES), whole),
            pl.BlockSpec((seq, LANES), whole),
            pl.BlockSpec((None, 1, C_HD), g2),
            pl.BlockSpec((None, 1, C_HD), g2),
        ],
        out_specs=pl.BlockSpec((tq, groups * C_HD), lambda b, h, i: (b * nq + i, h)),
        out_shape=jax.ShapeDtypeStruct((t, C_HEADS * C_HD), BF16),
        scratch_shapes=[pltpu.VMEM((seq, C_HD), BF16), pltpu.VMEM((seq, 2 * C_HD), BF16)],
        compiler_params=_params(("parallel", "parallel", "arbitrary")),
        name="gqa_attn",
    )(proj, proj, proj, cos_t, sin_t, cos_t, sin_t, qg, kg)


def _attention(q, k, v, v_col0, seq, kv_heads, groups, dk, dv, tq, sub, name):
    t = q.shape[0]
    nq = seq // tq
    return pl.pallas_call(
        functools.partial(_attn_kernel, groups=groups, dk=dk, dv=dv, sub=sub),
        grid=(t // seq, kv_heads, nq),
        in_specs=[
            pl.BlockSpec((tq, groups * dk), lambda b, h, i: (b * nq + i, h)),
            pl.BlockSpec((seq, dk), lambda b, h, i: (b, h)),
            pl.BlockSpec((seq, dv), lambda b, h, i: (b, v_col0 + h)),
        ],
        out_specs=pl.BlockSpec((tq, groups * dv), lambda b, h, i: (b * nq + i, h)),
        out_shape=jax.ShapeDtypeStruct((t, kv_heads * groups * dv), BF16),
        scratch_shapes=[pltpu.VMEM((seq, 2 * dv), BF16)],
        compiler_params=_params(("parallel", "parallel", "arbitrary")),
        name=name,
    )(q, k, v)


def _outproj_kernel(ya_ref, yb_ref, yc_ref, yd_ref, x_ref, mod_ref, gg_ref, w_ref, o_ref):
    acc = None
    for gi, y_ref in enumerate((ya_ref, yb_ref, yc_ref, yd_ref)):
        rows = slice(gi * W_GROUP, (gi + 1) * W_GROUP)
        yn = _rms_rows(y_ref[...].astype(F32), gg_ref[:, rows], W_GROUP).astype(BF16)
        part = jnp.dot(yn, w_ref[rows, :], preferred_element_type=F32)
        acc = part if acc is None else acc + part
    o_ref[...] = x_ref[...] + mod_ref[0, 2:3, :] * acc


def _out_proj(ya, yb, yc, yd, x, mod_l, b0, seq, gg, w_out_b, layer):
    t = x.shape[0]
    tm = 512
    yspec = pl.BlockSpec((tm, W_GROUP), lambda i: (i, 0))
    return pl.pallas_call(
        _outproj_kernel,
        grid=(t // tm,),
        in_specs=[
            yspec, yspec, yspec, yspec,
            pl.BlockSpec((tm, D_MODEL), lambda i: (i, 0)),
            pl.BlockSpec((1, 6, D_MODEL), lambda i: (b0 + (i * tm) // seq, 0, 0)),
            pl.BlockSpec((None, 1, D_MODEL), lambda i: (layer, 0, 0)),
            _single((None, D_MODEL, D_MODEL), lambda i: (layer, 0, 0)),
        ],
        out_specs=pl.BlockSpec((tm, D_MODEL), lambda i: (i, 0)),
        out_shape=jax.ShapeDtypeStruct((t, D_MODEL), F32),
        compiler_params=_params(("parallel",)),
        name="out_proj",
    )(ya, yb, yc, yd, x, mod_l, gg, w_out_b)


def _ffn_kernel(x_ref, xp_ref, xn_ref, mod_ref, g_ref, wg_ref, wu_ref, cwg_ref, cwu_ref, cbg_ref, cbu_ref,
                wd_ref, o_ref, h_ref, *, tm, seq):
    i = pl.program_id(0)
    j = pl.program_id(1)
    nj = pl.num_programs(1)
    rc = NORM_ROWS
    halo = FFN_HALO

    @pl.when(j == 0)
    def _():
        shift = mod_ref[0, 3:4, :]
        gs = g_ref[...] * (1.0 + mod_ref[0, 4:5, :])
        has_prev = ((i * tm) % seq) != 0
        has_next = (((i + 1) * tm) % seq) != 0
        hp = jnp.where(has_prev, _mod_norm_rows(xp_ref[...], gs, shift), 0.0)
        hn = jnp.where(has_next, _mod_norm_rows(xn_ref[...], gs, shift), 0.0)
        last = lax.broadcasted_iota(jnp.int32, hp.shape, 0) == halo - 1
        h_ref[pl.ds(tm, halo), :] = jnp.where(last, hp, hn).astype(BF16)

        chains = 8

        def body(c, carry):
            for k in range(chains):
                r0 = pl.multiple_of((c * chains + k) * rc, rc)
                h_ref[pl.ds(r0, rc), :] = _mod_norm_rows(x_ref[pl.ds(r0, rc), :], gs, shift).astype(BF16)
            return carry

        lax.fori_loop(0, tm // (rc * chains), body, 0)
        o_ref[...] = jnp.zeros(o_ref.shape, F32)

    hfull = h_ref[...]
    n = tm + halo

    def conv(u, cw_ref, cb_ref):
        dn = pltpu.roll(u, 1, 0)[:tm]
        up = pltpu.roll(u, n - 1, 0)[:tm]
        return dn * cw_ref[0:1, :] + u[:tm] * cw_ref[1:2, :] + up * cw_ref[2:3, :] + cb_ref[...]

    gate = conv(jnp.dot(hfull, wg_ref[...], preferred_element_type=F32), cwg_ref, cbg_ref)
    upv = conv(jnp.dot(hfull, wu_ref[...], preferred_element_type=F32), cwu_ref, cbu_ref)
    act = (jax.nn.silu(gate) * upv).astype(BF16)
    o_ref[...] += jnp.dot(act, wd_ref[...], preferred_element_type=F32)

    @pl.when(j == nj - 1)
    def _():
        o_ref[...] = x_ref[...] + mod_ref[0, 5:6, :] * o_ref[...]


def _ffn(x, mod_l, b0, seq, g, w_up_b, cw, cb, w_down_b, layer):
    t = x.shape[0]
    tm, tf = 1024, 512
    nf = D_FF // tf
    hb = tm // FFN_HALO
    last = t // FFN_HALO - 1
    return pl.pallas_call(
        functools.partial(_ffn_kernel, tm=tm, seq=seq),
        grid=(t // tm, nf),
        in_specs=[
            pl.BlockSpec((tm, D_MODEL), lambda i, j: (i, 0)),
            pl.BlockSpec((FFN_HALO, D_MODEL), lambda i, j: (jnp.maximum(i * hb - 1, 0), 0)),
            pl.BlockSpec((FFN_HALO, D_MODEL), lambda i, j: (jnp.minimum((i + 1) * hb, last), 0)),
            pl.BlockSpec((1, 6, D_MODEL), lambda i, j: (b0 + (i * tm) // seq, 0, 0)),
            pl.BlockSpec((None, 1, D_MODEL), lambda i, j: (layer, 0, 0)),
            pl.BlockSpec((None, D_MODEL, tf), lambda i, j: (layer, 0, j)),
            pl.BlockSpec((None, D_MODEL, tf), lambda i, j: (layer, 0, nf + j)),
            pl.BlockSpec((None, 3, tf), lambda i, j: (layer, 0, j)),
            pl.BlockSpec((None, 3, tf), lambda i, j: (layer, 0, nf + j)),
            pl.BlockSpec((None, 1, tf), lambda i, j: (layer, 0, j)),
            pl.BlockSpec((None, 1, tf), lambda i, j: (layer, 0, nf + j)),
            pl.BlockSpec((None, tf, D_MODEL), lambda i, j: (layer, j, 0)),
        ],
        out_specs=pl.BlockSpec((tm, D_MODEL), lambda i, j: (i, 0)),
        out_shape=jax.ShapeDtypeStruct((t, D_MODEL), F32),
        scratch_shapes=[pltpu.VMEM((tm + FFN_HALO, D_MODEL), BF16)],
        compiler_params=_params(("parallel", "arbitrary"), FFN_VMEM_LIMIT),
        name="ffn",
    )(x, x, x, mod_l, g, w_up_b, w_up_b, cw, cw, cb, cb, w_down_b)


def _axial_tables(seq, sec, lanes_used):
    pos = jnp.arange(seq, dtype=jnp.int32)
    row = (pos // GRID_W).astype(F32)
    col = (pos % GRID_W).astype(F32)
    inv = ROPE_THETA ** (-jnp.arange(0, sec, 2, dtype=F32) / sec)
    half = sec // 2
    lane = jnp.arange(LANES)
    in_use = lane < lanes_used
    which = (lane // sec) % 2
    freq = inv[lane % half]
    ang = jnp.where(which[None, :] == 0, row[:, None], col[:, None]) * freq[None, :]
    cos_t = jnp.where(in_use[None, :], jnp.cos(ang), 1.0)
    sign = jnp.where((lane % sec) < half, -1.0, 1.0)
    sin_t = jnp.where(in_use[None, :], jnp.sin(ang) * sign[None, :], 0.0)
    return cos_t.astype(F32), sin_t.astype(F32)


def _hy_filter_params(hy_w1, hy_b1, hy_f1, hy_w2, hy_b2, hy_f2, hy_w3, hy_decay):
    nl = hy_w1.shape[0]
    zero = jnp.zeros((nl, FILT_H, FILT_H), F32)
    bdiag = lambda w: jnp.concatenate([jnp.concatenate([w, zero], axis=2), jnp.concatenate([zero, w], axis=2)], axis=1)
    twice = lambda a: jnp.tile(a.reshape(nl, 1, FILT_H), (1, 1, 2))
    w1p = bdiag(jnp.pad(hy_w1, ((0, 0), (0, FILT_H - POS_EMB), (0, 0))))
    w2p = bdiag(hy_w2)
    w3 = hy_w3.reshape(nl, FILT_H, 2, 2, W_GROUP).transpose(0, 3, 1, 2, 4).reshape(nl, 2, FILT_H, 2 * W_GROUP)
    z3 = jnp.zeros_like(w3)
    w3d = jnp.stack([jnp.concatenate([w3, z3], axis=2), jnp.concatenate([z3, w3], axis=2)], axis=2)
    decd = hy_decay.reshape(nl, 2, 2, W_GROUP).transpose(0, 2, 1, 3).reshape(nl, 2, 1, 2 * W_GROUP)
    return w1p, twice(hy_b1), twice(hy_f1), w2p, twice(hy_b2), twice(hy_f2), w3d, decd


def _hyena_feats(seq):
    n = jnp.arange(seq, dtype=jnp.int32)
    posi = jnp.concatenate([seq - n, n])
    valid = (posi < seq).astype(F32)
    posi = jnp.minimum(posi, seq - 1)
    tl = jnp.linspace(0.0, 1.0, seq, dtype=F32)
    tlin = jnp.concatenate([tl[seq - 1:], tl[:0:-1], tl])
    bands = jnp.linspace(1e-4, POS_BANDS - 1, POS_BANDS, dtype=F32)
    ang = (2.0 * math.pi / seq) * posi.astype(F32)[:, None] * bands[None, :]
    feats = jnp.concatenate([tlin[:, None], jnp.cos(ang), -jnp.sin(ang)], axis=-1)
    pad = jnp.zeros((2 * seq, FILT_H - POS_EMB - 1), F32)
    f64 = jnp.concatenate([feats, pad, valid[:, None]], axis=-1)
    half = HY_P // 2
    return f64.reshape(-1, 2, half, FILT_H).transpose(0, 2, 1, 3).reshape(seq, 2 * FILT_H)


def _dft_mats(p):
    r = jnp.arange(p, dtype=jnp.int32)
    q = (r[:, None] * r[None, :]) % (2 * p)
    ang = q.astype(F32) * (math.pi / p)
    cosm = jnp.cos(ang)
    sinm = jnp.sin(ang)
    alt = jnp.where(r % 2 == 0, 1.0, -1.0).astype(F32)
    im_rows = jnp.where((r == 0)[:, None], alt[None, :], -sinm)
    fwd = jnp.concatenate([cosm, im_rows], axis=0)
    wre = jnp.where((r == 0)[None, :], 0.5, 1.0) / p
    inv_re = cosm * wre
    inv_im = jnp.where((r == 0)[None, :], alt[:, None] * (0.5 / p), -sinm / p)
    inv = jnp.concatenate([inv_re, inv_im], axis=1)
    return fwd.astype(BF16), inv.astype(BF16)


def kernel(x_prompt, x_sample, c_prompt, c_sample, ada_w, ada_b, norm1_g, w_in, gm_vnorm_g, gm_spatial_w, gm_spatial_b, hy_conv_w, hy_conv_b, hy_w1, hy_b1, hy_f1, hy_w2, hy_b2, hy_f2, hy_w3, hy_decay, hy_skip, gqa_qn_g, gqa_kn_g, mla_q_a_g, mla_w_q_b, mla_kv_a_g, mla_w_kv_b, mla_qn_g, mla_kn_g, group_norm_g, w_out, norm2_g, ffn_w_up, ffn_conv_w, ffn_conv_b, ffn_w_down):
    nl = DEPTH
    w_in_b = jnp.pad(w_in.astype(BF16), ((0, 0), (0, 0), (0, IN_PAD - IN_COLS)))
    w_out_b = w_out.astype(BF16)
    w_up_b = ffn_w_up.astype(BF16)
    w_down_b = ffn_w_down.astype(BF16)
    ws_b = gm_spatial_w.astype(BF16)
    gm_bias = jnp.broadcast_to(jnp.swapaxes(gm_spatial_b, 1, 2)[:, :, :, None],
                               (nl, CHUNK, A_HEADS, LANES)).reshape(nl, CHUNK, W_GROUP)
    row3 = lambda a: a.reshape(nl, 1, a.shape[-1])
    norm1 = row3(norm1_g)
    norm2 = row3(norm2_g)
    gm_g = row3(gm_vnorm_g)
    gg = row3(group_norm_g)
    hy_cw = hy_conv_w.reshape(nl, 3, 3, W_GROUP).transpose(0, 2, 1, 3)
    hy_cb = hy_conv_b.reshape(nl, 3, W_GROUP)
    hy_filter_params = _hy_filter_params(hy_w1, hy_b1, hy_f1, hy_w2, hy_b2, hy_f2, hy_w3, hy_decay)
    gqa_qg = row3(gqa_qn_g)
    gqa_kg = row3(gqa_kn_g)
    mla_qag = row3(mla_q_a_g)
    mla_kvg = row3(mla_kv_a_g)
    slot_pad = MLA_SLOT - NOPE - ROPE_D
    wq_b = jnp.pad(mla_w_q_b.reshape(nl, Q_LORA, D_HEADS, NOPE + ROPE_D),
                   ((0, 0), (0, 0), (0, 0), (0, slot_pad))).reshape(nl, Q_LORA, D_HEADS * MLA_SLOT).astype(BF16)
    wkv_b = mla_w_kv_b.astype(BF16)
    mla_qn = jnp.pad(row3(mla_qn_g), ((0, 0), (0, 0), (0, slot_pad)))
    mla_kn = jnp.pad(row3(mla_kn_g), ((0, 0), (0, 0), (0, slot_pad)))
    ffn_cb = row3(ffn_conv_b)

    nbp = x_prompt.shape[0]
    c_all = jnp.concatenate([c_prompt, c_sample], axis=0)
    mod = _ada_mod(c_all, ada_w, ada_b).reshape(nl, c_all.shape[0], 6, D_MODEL)

    fwd_b, inv_b = _dft_mats(HY_P)

    def trunk(x3, b0):
        bsz, seq, _ = x3.shape
        x = x3.reshape(bsz * seq, D_MODEL)
        gcos, gsin = _axial_tables(seq, C_HD // 2, C_HD)
        mcos, msin = _axial_tables(seq, ROPE_D // 2, ROPE_D)
        feats = _hyena_feats(seq)
        sub = ATTN_SCORE_ELEMS // seq
        for l in range(nl):
            mod_l = mod[l]
            proj = _in_proj(x, mod_l, b0, seq, norm1, w_in_b, l)
            ya = _gmlp(proj, gm_g, ws_b, gm_bias, l)
            hfilt = _hy_filters(feats, *hy_filter_params, fwd_b, l, seq)
            yb = _hyena(proj, hy_cw, hy_cb, hy_skip, hfilt, fwd_b, inv_b, l, seq)
            yc = _gqa_attention(proj, gcos, gsin, gqa_qg, gqa_kg, l, seq, sub)
            mq, mk, mv = _mla_prep(proj, mcos, msin, mla_qag, mla_kvg, wq_b, wkv_b, mla_qn, mla_kn, l, seq)
            yd = _attention(mq, mk, mv, 0, seq, D_HEADS, 1, MLA_SLOT, V_HD, 4 * sub, sub, "mla_attn")
            x = _out_proj(ya, yb, yc, yd, x, mod_l, b0, seq, gg, w_out_b, l)
            x = _ffn(x, mod_l, b0, seq, norm2, w_up_b, ffn_conv_w, ffn_cb, w_down_b, l)
        return x.reshape(bsz, seq, D_MODEL)

    return trunk(x_prompt, 0), trunk(x_sample, nbp)
```
